```python
import jax, jax.numpy as jnp
from jax import lax
import numpy as np

D_MODEL = 1024
BATCH = 2
SEQ = 8192
DEPTH = 4

CHUNK = 64
Q_BLOCK = 128
N_A_LAYERS = DEPTH // 2
N_B_LAYERS = DEPTH - N_A_LAYERS
SB_HEADS = 16
SB_HEAD_DIM = D_MODEL // SB_HEADS
MLA_HEADS = 16
MLA_NOPE_DIM = 64
MLA_ROPE_DIM = 32
MLA_V_DIM = 64
MLA_Q_RANK = 384
MLA_KV_RANK = 256
D_FF = 4 * D_MODEL
ROPE_THETA = 10000.0
NORM_EPS = 1e-6
MAX_POS_OFFSET = 4096

kernel_name = "yoco_stickbreak_mla_hybrid"


def rms_norm(x, g):
    xf = x.astype(jnp.float32)
    y = xf * lax.rsqrt(jnp.mean(xf * xf, axis=-1, keepdims=True) + NORM_EPS)
    return (y * g.astype(jnp.float32)).astype(x.dtype)


def rope_tables(positions):
    inv_freq = ROPE_THETA ** (-jnp.arange(0, MLA_ROPE_DIM, 2, dtype=jnp.float32) / MLA_ROPE_DIM)
    ang = positions.astype(jnp.float32)[..., None] * inv_freq
    return jnp.cos(ang), jnp.sin(ang)


def apply_rope(t, cos, sin):
    half = MLA_ROPE_DIM // 2
    t1 = t[..., :half].astype(jnp.float32)
    t2 = t[..., half:].astype(jnp.float32)
    out = jnp.concatenate([t1 * cos - t2 * sin, t2 * cos + t1 * sin], axis=-1)
    return out.astype(t.dtype)


def to_query_blocks(t):
    b, s, h, d = t.shape
    return t.reshape(b, s // Q_BLOCK, Q_BLOCK, h, d).transpose(1, 0, 3, 2, 4)


def from_query_blocks(t):
    nb, b, qb, h, d = t.shape
    return t.transpose(1, 0, 2, 3, 4).reshape(b, nb * qb, h * d)


def stick_breaking_attention(h, w_qkv, w_o):
    b, s, _ = h.shape
    qkv = (h @ w_qkv).reshape(b, s, 3, SB_HEADS, SB_HEAD_DIM)
    q, k, v = qkv[:, :, 0], qkv[:, :, 1], qkv[:, :, 2]
    k_h = k.transpose(0, 2, 1, 3)
    v_h = v.transpose(0, 2, 1, 3)
    q_blocks = to_query_blocks(q)
    scale = SB_HEAD_DIM ** -0.5
    key_pos = jnp.arange(s)

    def block(args):
        q_blk, blk_idx = args
        q_pos = blk_idx * Q_BLOCK + jnp.arange(Q_BLOCK)
        z = jnp.einsum('bhqd,bhkd->bhqk', q_blk, k_h).astype(jnp.float32) * scale
        causal = key_pos[None, :] < q_pos[:, None]
        log_beta = jax.nn.log_sigmoid(z)
        log_keep = jnp.where(causal, jax.nn.log_sigmoid(-z), 0.0)
        shifted = jnp.concatenate([log_keep[..., 1:], jnp.zeros_like(log_keep[..., :1])], axis=-1)
        log_survive = lax.cumsum(shifted, axis=3, reverse=True)
        weights = jnp.where(causal, jnp.exp(log_beta + log_survive), 0.0)
        return jnp.einsum('bhqk,bhkd->bqhd', weights.astype(v_h.dtype), v_h)

    out = lax.map(block, (q_blocks, jnp.arange(s // Q_BLOCK)))
    return from_query_blocks(out) @ w_o


def mla_shared_kv(h, w_dkv, kv_lat_norm, w_ukv, cos, sin):
    b, s, _ = h.shape
    down = h @ w_dkv
    c_kv = rms_norm(down[..., :MLA_KV_RANK], kv_lat_norm)
    k_rope = apply_rope(down[..., MLA_KV_RANK:], cos, sin)
    kv = (c_kv @ w_ukv).reshape(b, s, MLA_HEADS, MLA_NOPE_DIM + MLA_V_DIM)
    k_nope = kv[..., :MLA_NOPE_DIM].transpose(0, 2, 1, 3)
    v = kv[..., MLA_NOPE_DIM:].transpose(0, 2, 1, 3)
    return k_nope, k_rope, v


def mla_attention(h, w_dq, q_lat_norm, w_uq, w_o, k_nope, k_rope, v, cos, sin):
    b, s, _ = h.shape
    c_q = rms_norm(h @ w_dq, q_lat_norm)
    q = (c_q @ w_uq).reshape(b, s, MLA_HEADS, MLA_NOPE_DIM + MLA_ROPE_DIM)
    q_nope = q[..., :MLA_NOPE_DIM]
    q_rope = apply_rope(q[..., MLA_NOPE_DIM:], cos[:, :, None, :], sin[:, :, None, :])
    qn_blocks = to_query_blocks(q_nope)
    qr_blocks = to_query_blocks(q_rope)
    scale = (MLA_NOPE_DIM + MLA_ROPE_DIM) ** -0.5
    key_chunk = jnp.arange(s) // CHUNK

    def block(args):
        qn, qr, blk_idx = args
        q_chunk = (blk_idx * Q_BLOCK + jnp.arange(Q_BLOCK)) // CHUNK
        scores = (jnp.einsum('bhqd,bhkd->bhqk', qn, k_nope)
                  + jnp.einsum('bhqr,bkr->bhqk', qr, k_rope)).astype(jnp.float32) * scale
        allowed = key_chunk[None, :] <= q_chunk[:, None]
        probs = jax.nn.softmax(jnp.where(allowed, scores, -jnp.inf), axis=-1)
        return jnp.einsum('bhqk,bhkd->bqhd', probs.astype(v.dtype), v)

    out = lax.map(block, (qn_blocks, qr_blocks, jnp.arange(s // Q_BLOCK)))
    return from_query_blocks(out) @ w_o


def squared_relu_mlp(h, w1, w2):
    return jnp.square(jax.nn.relu(h @ w1)) @ w2


def setup_inputs(seed: int = 0) -> dict:
    key = jax.random.key(seed)
    ks = jax.random.split(key, 20)

    def w(k, shape, fan_in, gain=1.0):
        return jax.random.normal(k, shape, jnp.float32) * (gain * fan_in ** -0.5)

    def gains(k, shape):
        return 1.0 + 0.02 * jax.random.normal(k, shape, jnp.float32)

    out_gain = (2.0 * DEPTH) ** -0.5
    x = jax.random.normal(ks[0], (BATCH, SEQ, D_MODEL), jnp.float32)
    offsets = jax.random.randint(ks[1], (BATCH, 1), 0, MAX_POS_OFFSET, dtype=jnp.int32)
    positions = offsets + jnp.arange(SEQ, dtype=jnp.int32)[None, :]
    return {
        "x": x,
        "positions": positions,
        "attn_norm": gains(ks[2], (DEPTH, D_MODEL)),
        "mlp_norm": gains(ks[3], (DEPTH, D_MODEL)),
        "sb_w_qkv": w(ks[4], (N_A_LAYERS, D_MODEL, 3 * D_MODEL), D_MODEL),
        "sb_w_o": w(ks[5], (N_A_LAYERS, D_MODEL, D_MODEL), D_MODEL, out_gain),
        "kv_norm": gains(ks[6], (D_MODEL,)),
        "mla_w_dkv": w(ks[7], (D_MODEL, MLA_KV_RANK + MLA_ROPE_DIM), D_MODEL),
        "mla_kv_lat_norm": gains(ks[8], (MLA_KV_RANK,)),
        "mla_w_ukv": w(ks[9], (MLA_KV_RANK, MLA_HEADS * (MLA_NOPE_DIM + MLA_V_DIM)), MLA_KV_RANK),
        "mla_w_dq": w(ks[10], (N_B_LAYERS, D_MODEL, MLA_Q_RANK), D_MODEL),
        "mla_q_lat_norm": gains(ks[11], (N_B_LAYERS, MLA_Q_RANK)),
        "mla_w_uq": w(ks[12], (N_B_LAYERS, MLA_Q_RANK, MLA_HEADS * (MLA_NOPE_DIM + MLA_ROPE_DIM)), MLA_Q_RANK),
        "mla_w_o": w(ks[13], (N_B_LAYERS, MLA_HEADS * MLA_V_DIM, D_MODEL), MLA_HEADS * MLA_V_DIM, out_gain),
        "mlp_w1": w(ks[14], (DEPTH, D_MODEL, D_FF), D_MODEL),
        "mlp_w2": w(ks[15], (DEPTH, D_FF, D_MODEL), D_FF, out_gain),
        "final_norm": gains(ks[16], (D_MODEL,)),
    }


def reference(x, positions, attn_norm, mlp_norm, sb_w_qkv, sb_w_o, kv_norm, mla_w_dkv,
              mla_kv_lat_norm, mla_w_ukv, mla_w_dq, mla_q_lat_norm, mla_w_uq, mla_w_o,
              mlp_w1, mlp_w2, final_norm):
    cos, sin = rope_tables(positions)
    k_nope = k_rope = v = None
    for layer in range(DEPTH):
        if layer < N_A_LAYERS:
            x = x + stick_breaking_attention(rms_norm(x, attn_norm[layer]),
                                             sb_w_qkv[layer], sb_w_o[layer])
        else:
            if layer == N_A_LAYERS:
                k_nope, k_rope, v = mla_shared_kv(rms_norm(x, kv_norm), mla_w_dkv,
                                                  mla_kv_lat_norm, mla_w_ukv, cos, sin)
            j = layer - N_A_LAYERS
            x = x + mla_attention(rms_norm(x, attn_norm[layer]), mla_w_dq[j], mla_q_lat_norm[j],
                                  mla_w_uq[j], mla_w_o[j], k_nope, k_rope, v, cos, sin)
        x = x + squared_relu_mlp(rms_norm(x, mlp_norm[layer]), mlp_w1[layer], mlp_w2[layer])
    return rms_norm(x, final_norm)
```

```python
import functools

import jax
import jax.numpy as jnp
from jax import lax
from jax.experimental import pallas as pl
from jax.experimental.pallas import tpu as pltpu

D_MODEL = 1024
DEPTH = 4
CHUNK = 64
N_A_LAYERS = DEPTH // 2
SB_HEADS = 16
SB_HEAD_DIM = D_MODEL // SB_HEADS
MLA_HEADS = 16
MLA_NOPE_DIM = 64
MLA_ROPE_DIM = 32
MLA_V_DIM = 64
MLA_Q_RANK = 384
MLA_KV_RANK = 256
D_FF = 4 * D_MODEL
ROPE_THETA = 10000.0
NORM_EPS = 1e-6

LANES = 128
ROW_TILE = 512
SB_TILE = 256
MLA_TILE = 512
VMEM_LIMIT = 56 * 1024 * 1024
MASK_VALUE = -1e30

F32 = jnp.float32
BF16 = jnp.bfloat16


def _rms(x, g):
    return x * lax.rsqrt(jnp.mean(x * x, axis=-1, keepdims=True) + NORM_EPS) * g


def _dot(a, b):
    return jnp.dot(a, b, preferred_element_type=F32)


def _dot_nt(a, b):
    return lax.dot_general(a, b, (((1,), (1,)), ((), ())), preferred_element_type=F32)


def _params():
    return pltpu.CompilerParams(
        dimension_semantics=None, vmem_limit_bytes=VMEM_LIMIT)


def _resident(shape):
    return pl.BlockSpec(shape, lambda *_: (0,) * len(shape),
                        pipeline_mode=pl.Buffered(1))


def _norm_matmul_kernel(x_ref, g_ref, w_ref, o_ref, *, col_chunk):
    h = _rms(x_ref[...], g_ref[...]).astype(BF16)
    n_out = o_ref.shape[1]
    for c in range(n_out // col_chunk):
        sl = slice(c * col_chunk, (c + 1) * col_chunk)
        o_ref[:, sl] = _dot(h, w_ref[:, sl]).astype(o_ref.dtype)


def _norm_matmul(x, g, w):
    n, d = x.shape
    n_out = w.shape[1]
    return pl.pallas_call(
        functools.partial(_norm_matmul_kernel, col_chunk=1024),
        grid=(n // ROW_TILE,),
        in_specs=[pl.BlockSpec((ROW_TILE, d), lambda i: (i, 0)),
                  _resident((1, d)),
                  _resident((d, n_out))],
        out_specs=pl.BlockSpec((ROW_TILE, n_out), lambda i: (i, 0)),
        out_shape=jax.ShapeDtypeStruct((n, n_out), BF16),
        compiler_params=_params(),
        name="norm_qkv",
    )(x, g, w)


def _sb_attn_kernel(q_ref, k_ref, v_ref, o_ref, *, tile):
    qi = pl.program_id(2)
    lane = lax.broadcasted_iota(jnp.int32, (1, LANES), 1)
    first = lane < SB_HEAD_DIM
    q = q_ref[0] * (SB_HEAD_DIM ** -0.5)
    zero = jnp.zeros_like(q)
    q_heads = (jnp.where(first, q, zero), jnp.where(first, zero, q))

    row = lax.broadcasted_iota(jnp.int32, (tile, tile), 0)
    col = lax.broadcasted_iota(jnp.int32, (tile, tile), 1)
    later = (row > col).astype(BF16)
    causal = col < row

    def block(kb, carry, diag):
        acc, c0, c1 = carry
        start = pl.multiple_of(kb * tile, tile)
        k = k_ref[0, pl.ds(start, tile), :]
        v = v_ref[0, pl.ds(start, tile), :]
        vzero = jnp.zeros_like(v)
        v_heads = (jnp.where(first, v, vzero), jnp.where(first, vzero, v))
        new_c = []
        for qh, vh, c in zip(q_heads, v_heads, (c0, c1)):
            z = _dot_nt(qh, k)
            soft = jnp.log1p(jnp.exp(-jnp.abs(z)))
            log_keep = jnp.minimum(-z, 0.0) - soft
            log_beta = jnp.minimum(z, 0.0) - soft
            if diag:
                log_keep = jnp.where(causal, log_keep, 0.0)
            hi = log_keep.astype(BF16)
            lo = (log_keep - hi.astype(F32)).astype(BF16)
            survive = _dot(hi, later) + _dot(lo, later)
            w = jnp.exp(log_beta + survive + c)
            if diag:
                w = jnp.where(causal, w, 0.0)
            acc = acc + _dot(w.astype(BF16), vh)
            new_c.append(c + jnp.sum(log_keep, axis=-1, keepdims=True))
        return acc, new_c[0], new_c[1]

    init = (jnp.zeros((tile, LANES), F32),
            jnp.zeros((tile, 1), F32), jnp.zeros((tile, 1), F32))
    carry = block(qi, init, True)
    carry = lax.fori_loop(
        0, qi, lambda i, cr: block(qi - 1 - i, cr, False), carry)
    o_ref[0] = carry[0].astype(o_ref.dtype)


def _sb_attention(qkv, batch, seq):
    pairs = D_MODEL // LANES
    t = SB_TILE
    return pl.pallas_call(
        functools.partial(_sb_attn_kernel, tile=t),
        grid=(batch, pairs, seq // t),
        in_specs=[pl.BlockSpec((1, t, LANES), lambda b, p, i: (b, i, p)),
                  pl.BlockSpec((1, seq, LANES), lambda b, p, i: (b, 0, pairs + p)),
                  pl.BlockSpec((1, seq, LANES), lambda b, p, i: (b, 0, 2 * pairs + p))],
        out_specs=pl.BlockSpec((1, t, LANES), lambda b, p, i: (b, i, p)),
        out_shape=jax.ShapeDtypeStruct((batch, seq, D_MODEL), BF16),
        compiler_params=_params(),
        name="sb_attention",
    )(qkv, qkv, qkv)


def _proj_mlp_kernel(x_ref, a_ref, wo_ref, g_ref, w1_ref, w2_ref, *rest, ff_chunk, final):
    if final:
        gf_ref, o_ref = rest
    else:
        (o_ref,) = rest
    x1 = x_ref[...] + _dot(a_ref[...], wo_ref[...])
    h = _rms(x1, g_ref[...]).astype(BF16)
    acc = x1
    for c in range(D_FF // ff_chunk):
        sl = slice(c * ff_chunk, (c + 1) * ff_chunk)
        u = jnp.maximum(_dot(h, w1_ref[:, sl]), 0.0)
        acc = acc + _dot((u * u).astype(BF16), w2_ref[sl, :])
    if final:
        acc = _rms(acc, gf_ref[...])
    o_ref[...] = acc


def _proj_mlp(x, a, wo, g, w1, w2, final_g=None):
    n, d = x.shape
    final = final_g is not None
    row = lambda i: (i, 0)
    in_specs = [pl.BlockSpec((ROW_TILE, d), row),
                pl.BlockSpec((ROW_TILE, a.shape[1]), row),
                _resident(wo.shape), _resident((1, d)),
                _resident(w1.shape), _resident(w2.shape)]
    args = [x, a, wo, g, w1, w2]
    if final:
        in_specs.append(_resident((1, d)))
        args.append(final_g)
    return pl.pallas_call(
        functools.partial(_proj_mlp_kernel, ff_chunk=1024, final=final),
        grid=(n // ROW_TILE,),
        in_specs=in_specs,
        out_specs=pl.BlockSpec((ROW_TILE, d), row),
        out_shape=jax.ShapeDtypeStruct((n, d), F32),
        compiler_params=_params(),
        name="proj_mlp_final" if final else "proj_mlp",
    )(*args)


def _rope_tables(pos, inv_lane):
    lane = lax.broadcasted_iota(jnp.int32, (1, LANES), 1)
    rope = jnp.logical_and(lane >= MLA_NOPE_DIM, lane < MLA_NOPE_DIM + MLA_ROPE_DIM)
    ang = pos * inv_lane
    cos_t = jnp.where(lane < MLA_NOPE_DIM, 1.0, jnp.where(rope, jnp.cos(ang), 0.0))
    sin_t = jnp.where(rope, jnp.sin(ang), 0.0)
    return cos_t, sin_t


def _shared_kv_kernel(x_ref, pos_ref, inv_ref, g_ref, wd_ref, gl_ref, wk_ref, wv_ref,
                      k_ref, v_ref):
    h = _rms(x_ref[...], g_ref[...]).astype(BF16)
    down = _dot(h, wd_ref[...])
    c = _rms(down[:, :MLA_KV_RANK], gl_ref[...]).astype(BF16)
    cos_t, sin_t = _rope_tables(pos_ref[...], inv_ref[...])
    k_rope = (down[:, MLA_KV_RANK:MLA_KV_RANK + LANES] * cos_t
              + down[:, MLA_KV_RANK + LANES:] * sin_t)
    k_nope = _dot(c, wk_ref[...])
    for hd in range(MLA_HEADS):
        sl = slice(hd * LANES, (hd + 1) * LANES)
        k_ref[:, sl] = (k_nope[:, sl] + k_rope).astype(k_ref.dtype)
    v_ref[...] = _dot(c, wv_ref[...]).astype(v_ref.dtype)


def _shared_kv(x, pos, inv_lane, g, wd, gl, wk, wv):
    n, d = x.shape
    row = lambda i: (i, 0)
    return pl.pallas_call(
        _shared_kv_kernel,
        grid=(n // ROW_TILE,),
        in_specs=[pl.BlockSpec((ROW_TILE, d), row),
                  pl.BlockSpec((ROW_TILE, 1), row),
                  _resident((1, LANES)), _resident((1, d)), _resident(wd.shape),
                  _resident((1, MLA_KV_RANK)), _resident(wk.shape), _resident(wv.shape)],
        out_specs=[pl.BlockSpec((ROW_TILE, MLA_HEADS * LANES), row),
                   pl.BlockSpec((ROW_TILE, MLA_HEADS * MLA_V_DIM), row)],
        out_shape=[jax.ShapeDtypeStruct((n, MLA_HEADS * LANES), BF16),
                   jax.ShapeDtypeStruct((n, MLA_HEADS * MLA_V_DIM), BF16)],
        compiler_params=_params(),
        name="mla_shared_kv",
    )(x, pos, inv_lane, g, wd, gl, wk, wv)


def _mla_q_kernel(x_ref, pos_ref, inv_ref, g_ref, wdq_ref, gq_ref, wa_ref, wb_ref, q_ref,
                  *, heads_per_chunk):
    h = _rms(x_ref[...], g_ref[...]).astype(BF16)
    cq = _rms(_dot(h, wdq_ref[...]), gq_ref[...]).astype(BF16)
    cos_t, sin_t = _rope_tables(pos_ref[...], inv_ref[...])
    scale = (MLA_NOPE_DIM + MLA_ROPE_DIM) ** -0.5
    cos_t = cos_t * scale
    sin_t = sin_t * scale
    width = heads_per_chunk * LANES
    for c in range(MLA_HEADS // heads_per_chunk):
        sl = slice(c * width, (c + 1) * width)
        qa = _dot(cq, wa_ref[:, sl])
        qb = _dot(cq, wb_ref[:, sl])
        for j in range(heads_per_chunk):
            hs = slice(j * LANES, (j + 1) * LANES)
            q_ref[:, c * width + j * LANES:c * width + (j + 1) * LANES] = (
                qa[:, hs] * cos_t + qb[:, hs] * sin_t).astype(q_ref.dtype)


def _mla_q(x, pos, inv_lane, g, wdq, gq, wa, wb):
    n, d = x.shape
    row = lambda i: (i, 0)
    return pl.pallas_call(
        functools.partial(_mla_q_kernel, heads_per_chunk=4),
        grid=(n // ROW_TILE,),
        in_specs=[pl.BlockSpec((ROW_TILE, d), row),
                  pl.BlockSpec((ROW_TILE, 1), row),
                  _resident((1, LANES)), _resident((1, d)), _resident(wdq.shape),
                  _resident((1, MLA_Q_RANK)), _resident(wa.shape), _resident(wb.shape)],
        out_specs=pl.BlockSpec((ROW_TILE, MLA_HEADS * LANES), row),
        out_shape=jax.ShapeDtypeStruct((n, MLA_HEADS * LANES), BF16),
        compiler_params=_params(),
        name="mla_q",
    )(x, pos, inv_lane, g, wdq, gq, wa, wb)


def _mla_attn_kernel(q_ref, k_ref, v_ref, o_ref, *, tile):
    qi = pl.program_id(2)
    lane = lax.broadcasted_iota(jnp.int32, (1, LANES), 1)
    first = lane < MLA_V_DIM
    q = q_ref[0]
    q_heads = (q[:, :LANES], q[:, LANES:])
    row = lax.broadcasted_iota(jnp.int32, (tile, tile), 0)
    col = lax.broadcasted_iota(jnp.int32, (tile, tile), 1)
    allowed = (col // CHUNK) <= (row // CHUNK)

    def block(kb, carry, diag):
        m0, l0, m1, l1, acc = carry
        start = pl.multiple_of(kb * tile, tile)
        k = k_ref[0, pl.ds(start, tile), :]
        v = v_ref[0, pl.ds(start, tile), :]
        stats = []
        for j, (m, l) in enumerate(((m0, l0), (m1, l1))):
            s = _dot_nt(q_heads[j], k[:, j * LANES:(j + 1) * LANES])
            if diag:
                s = jnp.where(allowed, s, MASK_VALUE)
            m_new = jnp.maximum(m, jnp.max(s, axis=-1, keepdims=True))
            alpha = jnp.exp(m - m_new)
            p = jnp.exp(s - m_new)
            l_new = alpha * l + jnp.sum(p, axis=-1, keepdims=True)
            stats.append((m_new, l_new, alpha, _dot(p.astype(BF16), v)))
        (m0, l0, a0, pv0), (m1, l1, a1, pv1) = stats
        acc = jnp.where(first, a0, a1) * acc + jnp.where(first, pv0, pv1)
        return m0, l0, m1, l1, acc

    neg = jnp.full((tile, 1), MASK_VALUE, F32)
    zero = jnp.zeros((tile, 1), F32)
    carry = (neg, zero, neg, zero, jnp.zeros((tile, LANES), F32))
    carry = block(qi, carry, True)
    carry = lax.fori_loop(0, qi, lambda i, cr: block(i, cr, False), carry)
    _, l0, _, l1, acc = carry
    o_ref[0] = (acc / jnp.where(first, l0, l1)).astype(o_ref.dtype)


def _mla_attention(q, k, v, batch, seq):
    pairs = MLA_HEADS // 2
    t = MLA_TILE
    return pl.pallas_call(
        functools.partial(_mla_attn_kernel, tile=t),
        grid=(batch, pairs, seq // t),
        in_specs=[pl.BlockSpec((1, t, 2 * LANES), lambda b, p, i: (b, i, p)),
                  pl.BlockSpec((1, seq, 2 * LANES), lambda b, p, i: (b, 0, p)),
                  pl.BlockSpec((1, seq, LANES), lambda b, p, i: (b, 0, p))],
        out_specs=pl.BlockSpec((1, t, LANES), lambda b, p, i: (b, i, p)),
        out_shape=jax.ShapeDtypeStruct((batch, seq, MLA_HEADS * MLA_V_DIM), BF16),
        compiler_params=_params(),
        name="mla_attention",
    )(q, k, v)


def _rope_lane_freqs():
    inv_freq = ROPE_THETA ** (-jnp.arange(0, MLA_ROPE_DIM, 2, dtype=F32) / MLA_ROPE_DIM)
    half = MLA_ROPE_DIM // 2
    out = jnp.zeros((1, LANES), F32)
    out = out.at[0, MLA_NOPE_DIM:MLA_NOPE_DIM + half].set(inv_freq)
    out = out.at[0, MLA_NOPE_DIM + half:MLA_NOPE_DIM + 2 * half].set(inv_freq)
    return out


def _layout_w_dkv(w_dkv):
    half = MLA_ROPE_DIM // 2
    d = w_dkv.shape[0]
    t1 = w_dkv[:, MLA_KV_RANK:MLA_KV_RANK + half]
    t2 = w_dkv[:, MLA_KV_RANK + half:]
    pad_lo = jnp.zeros((d, MLA_NOPE_DIM), w_dkv.dtype)
    pad_hi = jnp.zeros((d, LANES - MLA_NOPE_DIM - MLA_ROPE_DIM), w_dkv.dtype)
    blk_a = jnp.concatenate([pad_lo, t1, t2, pad_hi], axis=1)
    blk_b = jnp.concatenate([pad_lo, -t2, t1, pad_hi], axis=1)
    return jnp.concatenate([w_dkv[:, :MLA_KV_RANK], blk_a, blk_b], axis=1).astype(BF16)


def _layout_w_ukv(w_ukv):
    r = w_ukv.shape[0]
    w = w_ukv.reshape(r, MLA_HEADS, MLA_NOPE_DIM + MLA_V_DIM)
    wk = jnp.concatenate(
        [w[:, :, :MLA_NOPE_DIM], jnp.zeros((r, MLA_HEADS, LANES - MLA_NOPE_DIM), w.dtype)],
        axis=2).reshape(r, MLA_HEADS * LANES)
    wv = w[:, :, MLA_NOPE_DIM:].reshape(r, MLA_HEADS * MLA_V_DIM)
    return wk.astype(BF16), wv.astype(BF16)


def _layout_w_uq(w_uq):
    r = w_uq.shape[0]
    half = MLA_ROPE_DIM // 2
    w = w_uq.reshape(r, MLA_HEADS, MLA_NOPE_DIM + MLA_ROPE_DIM)
    nope = w[:, :, :MLA_NOPE_DIM]
    t1 = w[:, :, MLA_NOPE_DIM:MLA_NOPE_DIM + half]
    t2 = w[:, :, MLA_NOPE_DIM + half:]
    pad_hi = jnp.zeros((r, MLA_HEADS, LANES - MLA_NOPE_DIM - MLA_ROPE_DIM), w.dtype)
    wa = jnp.concatenate([nope, t1, t2, pad_hi], axis=2)
    wb = jnp.concatenate([jnp.zeros_like(nope), -t2, t1, pad_hi], axis=2)
    shape = (r, MLA_HEADS * LANES)
    return wa.reshape(shape).astype(BF16), wb.reshape(shape).astype(BF16)


def kernel(x, positions, attn_norm, mlp_norm, sb_w_qkv, sb_w_o, kv_norm, mla_w_dkv,
           mla_kv_lat_norm, mla_w_ukv, mla_w_dq, mla_q_lat_norm, mla_w_uq, mla_w_o,
           mlp_w1, mlp_w2, final_norm):
    batch, seq, d = x.shape
    n = batch * seq
    xs = x.reshape(n, d)
    pos = positions.astype(F32).reshape(n, 1)
    inv_lane = _rope_lane_freqs()

    for layer in range(N_A_LAYERS):
        qkv = _norm_matmul(xs, attn_norm[layer].reshape(1, d), sb_w_qkv[layer].astype(BF16))
        attn = _sb_attention(qkv.reshape(batch, seq, 3 * d), batch, seq)
        xs = _proj_mlp(xs, attn.reshape(n, d), sb_w_o[layer].astype(BF16),
                       mlp_norm[layer].reshape(1, d),
                       mlp_w1[layer].astype(BF16), mlp_w2[layer].astype(BF16))

    wk, wv = _layout_w_ukv(mla_w_ukv)
    k_all, v_all = _shared_kv(xs, pos, inv_lane, kv_norm.reshape(1, d),
                              _layout_w_dkv(mla_w_dkv),
                              mla_kv_lat_norm.reshape(1, MLA_KV_RANK), wk, wv)
    k_all = k_all.reshape(batch, seq, MLA_HEADS * LANES)
    v_all = v_all.reshape(batch, seq, MLA_HEADS * MLA_V_DIM)

    for layer in range(N_A_LAYERS, DEPTH):
        j = layer - N_A_LAYERS
        wa, wb = _layout_w_uq(mla_w_uq[j])
        q = _mla_q(xs, pos, inv_lane, attn_norm[layer].reshape(1, d),
                   mla_w_dq[j].astype(BF16), mla_q_lat_norm[j].reshape(1, MLA_Q_RANK), wa, wb)
        attn = _mla_attention(q.reshape(batch, seq, MLA_HEADS * LANES), k_all, v_all, batch, seq)
        xs = _proj_mlp(xs, attn.reshape(n, MLA_HEADS * MLA_V_DIM), mla_w_o[j].astype(BF16),
                       mlp_norm[layer].reshape(1, d),
                       mlp_w1[layer].astype(BF16), mlp_w2[layer].astype(BF16),
                       final_g=final_norm.reshape(1, d) if layer == DEPTH - 1 else None)
    return xs.reshape(batch, seq, d)
```

```python
import functools

import jax
import jax.numpy as jnp
from jax import lax
from jax.experimental import pallas as pl
from jax.experimental.pallas import tpu as pltpu

D_MODEL = 1024
DEPTH = 4
CHUNK = 64
N_A_LAYERS = DEPTH // 2
SB_HEADS = 16
SB_HEAD_DIM = D_MODEL // SB_HEADS
MLA_HEADS = 16
MLA_NOPE_DIM = 64
MLA_ROPE_DIM = 32
MLA_V_DIM = 64
MLA_Q_RANK = 384
MLA_KV_RANK = 256
D_FF = 4 * D_MODEL
ROPE_THETA = 10000.0
NORM_EPS = 1e-6

LANES = 128
ROW_TILE = 512
SB_TILE = 128
SB_DEAD_LOG = -104.0
MLA_TILE = 512
VMEM_LIMIT = 56 * 1024 * 1024
MASK_VALUE = -1e30

F32 = jnp.float32
BF16 = jnp.bfloat16


def _rms(x, g):
    return x * lax.rsqrt(jnp.mean(x * x, axis=-1, keepdims=True) + NORM_EPS) * g


def _dot(a, b):
    return jnp.dot(a, b, preferred_element_type=F32)


def _dot_nt(a, b):
    return lax.dot_general(a, b, (((1,), (1,)), ((), ())), preferred_element_type=F32)


def _params():
    return pltpu.CompilerParams(
        dimension_semantics=None, vmem_limit_bytes=VMEM_LIMIT)


def _resident(shape):
    return pl.BlockSpec(shape, lambda *_: (0,) * len(shape),
                        pipeline_mode=pl.Buffered(1))


def _norm_matmul_kernel(x_ref, g_ref, w_ref, o_ref, *, col_chunk):
    h = _rms(x_ref[...], g_ref[...]).astype(BF16)
    n_out = o_ref.shape[1]
    for c in range(n_out // col_chunk):
        sl = slice(c * col_chunk, (c + 1) * col_chunk)
        o_ref[:, sl] = _dot(h, w_ref[:, sl]).astype(o_ref.dtype)


def _norm_matmul(x, g, w):
    n, d = x.shape
    n_out = w.shape[1]
    return pl.pallas_call(
        functools.partial(_norm_matmul_kernel, col_chunk=1024),
        grid=(n // ROW_TILE,),
        in_specs=[pl.BlockSpec((ROW_TILE, d), lambda i: (i, 0)),
                  _resident((1, d)),
                  _resident((d, n_out))],
        out_specs=pl.BlockSpec((ROW_TILE, n_out), lambda i: (i, 0)),
        out_shape=jax.ShapeDtypeStruct((n, n_out), BF16),
        compiler_params=_params(),
        name="norm_qkv",
    )(x, g, w)


def _sb_attn_kernel(q_ref, k_ref, v_ref, o_ref, *, tile):
    qi = pl.program_id(2)
    lane = lax.broadcasted_iota(jnp.int32, (1, LANES), 1)
    first = lane < SB_HEAD_DIM
    q = q_ref[0] * (SB_HEAD_DIM ** -0.5)
    row = lax.broadcasted_iota(jnp.int32, (tile, tile), 0)
    col = lax.broadcasted_iota(jnp.int32, (tile, tile), 1)
    later = (row > col).astype(BF16)
    causal = col < row

    def block(kb, qh, keep, acc, c, diag):
        start = pl.multiple_of(kb * tile, tile)
        k = k_ref[0, pl.ds(start, tile), :]
        v = v_ref[0, pl.ds(start, tile), :]
        vh = jnp.where(keep, v, jnp.zeros_like(v))
        z = _dot_nt(qh, k)
        soft = jnp.log1p(jnp.exp(-jnp.abs(z)))
        log_keep = jnp.minimum(-z, 0.0) - soft
        log_beta = jnp.minimum(z, 0.0) - soft
        if diag:
            log_keep = jnp.where(causal, log_keep, 0.0)
        hi = log_keep.astype(BF16)
        lo = (log_keep - hi.astype(F32)).astype(BF16)
        survive = _dot(hi, later) + _dot(lo, later)
        w = jnp.exp(log_beta + survive + c)
        if diag:
            w = jnp.where(causal, w, 0.0)
        acc = acc + _dot(w.astype(BF16), vh)
        return acc, c + jnp.sum(log_keep, axis=-1, keepdims=True)

    out = jnp.zeros((tile, LANES), F32)
    for keep in (first, jnp.logical_not(first)):
        qh = jnp.where(keep, q, jnp.zeros_like(q))
        acc, c = block(qi, qh, keep, jnp.zeros((tile, LANES), F32),
                       jnp.zeros((tile, 1), F32), True)

        def cond(state):
            kb, _, _, alive = state
            return jnp.logical_and(kb >= 0, alive > 0)

        def body(state, qh=qh, keep=keep):
            kb, acc, c, _ = state
            acc, c = block(kb, qh, keep, acc, c, False)
            alive = (jnp.max(c) > SB_DEAD_LOG).astype(jnp.int32)
            return kb - 1, acc, c, alive

        state = lax.while_loop(cond, body, (qi - 1, acc, c, jnp.int32(1)))
        out = out + state[1]
    o_ref[0] = out.astype(o_ref.dtype)


def _sb_attention(qkv, batch, seq):
    pairs = D_MODEL // LANES
    t = SB_TILE
    return pl.pallas_call(
        functools.partial(_sb_attn_kernel, tile=t),
        grid=(batch, pairs, seq // t),
        in_specs=[pl.BlockSpec((1, t, LANES), lambda b, p, i: (b, i, p)),
                  pl.BlockSpec((1, seq, LANES), lambda b, p, i: (b, 0, pairs + p)),
                  pl.BlockSpec((1, seq, LANES), lambda b, p, i: (b, 0, 2 * pairs + p))],
        out_specs=pl.BlockSpec((1, t, LANES), lambda b, p, i: (b, i, p)),
        out_shape=jax.ShapeDtypeStruct((batch, seq, D_MODEL), BF16),
        compiler_params=_params(),
        name="sb_attention",
    )(qkv, qkv, qkv)


def _proj_mlp_kernel(x_ref, a_ref, wo_ref, g_ref, w1_ref, w2_ref, *rest, ff_chunk, final):
    if final:
        gf_ref, o_ref = rest
    else:
        (o_ref,) = rest
    x1 = x_ref[...] + _dot(a_ref[...], wo_ref[...])
    h = _rms(x1, g_ref[...]).astype(BF16)
    acc = x1
    for c in range(D_FF // ff_chunk):
        sl = slice(c * ff_chunk, (c + 1) * ff_chunk)
        u = jnp.maximum(_dot(h, w1_ref[:, sl]), 0.0)
        acc = acc + _dot((u * u).astype(BF16), w2_ref[sl, :])
    if final:
        acc = _rms(acc, gf_ref[...])
    o_ref[...] = acc


def _proj_mlp(x, a, wo, g, w1, w2, final_g=None):
    n, d = x.shape
    final = final_g is not None
    row = lambda i: (i, 0)
    in_specs = [pl.BlockSpec((ROW_TILE, d), row),
                pl.BlockSpec((ROW_TILE, a.shape[1]), row),
                _resident(wo.shape), _resident((1, d)),
                _resident(w1.shape), _resident(w2.shape)]
    args = [x, a, wo, g, w1, w2]
    if final:
        in_specs.append(_resident((1, d)))
        args.append(final_g)
    return pl.pallas_call(
        functools.partial(_proj_mlp_kernel, ff_chunk=1024, final=final),
        grid=(n // ROW_TILE,),
        in_specs=in_specs,
        out_specs=pl.BlockSpec((ROW_TILE, d), row),
        out_shape=jax.ShapeDtypeStruct((n, d), F32),
        compiler_params=_params(),
        name="proj_mlp_final" if final else "proj_mlp",
    )(*args)


def _rope_tables(pos, inv_lane):
    lane = lax.broadcasted_iota(jnp.int32, (1, LANES), 1)
    rope = jnp.logical_and(lane >= MLA_NOPE_DIM, lane < MLA_NOPE_DIM + MLA_ROPE_DIM)
    ang = pos * inv_lane
    cos_t = jnp.where(lane < MLA_NOPE_DIM, 1.0, jnp.where(rope, jnp.cos(ang), 0.0))
    sin_t = jnp.where(rope, jnp.sin(ang), 0.0)
    return cos_t, sin_t


def _shared_kv_kernel(x_ref, pos_ref, inv_ref, g_ref, wd_ref, gl_ref, wk_ref, wv_ref,
                      k_ref, v_ref):
    h = _rms(x_ref[...], g_ref[...]).astype(BF16)
    down = _dot(h, wd_ref[...])
    c = _rms(down[:, :MLA_KV_RANK], gl_ref[...]).astype(BF16)
    cos_t, sin_t = _rope_tables(pos_ref[...], inv_ref[...])
    k_rope = (down[:, MLA_KV_RANK:MLA_KV_RANK + LANES] * cos_t
              + down[:, MLA_KV_RANK + LANES:] * sin_t)
    k_nope = _dot(c, wk_ref[...])
    for hd in range(MLA_HEADS):
        sl = slice(hd * LANES, (hd + 1) * LANES)
        k_ref[:, sl] = (k_nope[:, sl] + k_rope).astype(k_ref.dtype)
    v_ref[...] = _dot(c, wv_ref[...]).astype(v_ref.dtype)


def _shared_kv(x, pos, inv_lane, g, wd, gl, wk, wv):
    n, d = x.shape
    row = lambda i: (i, 0)
    return pl.pallas_call(
        _shared_kv_kernel,
        grid=(n // ROW_TILE,),
        in_specs=[pl.BlockSpec((ROW_TILE, d), row),
                  pl.BlockSpec((ROW_TILE, 1), row),
                  _resident((1, LANES)), _resident((1, d)), _resident(wd.shape),
                  _resident((1, MLA_KV_RANK)), _resident(wk.shape), _resident(wv.shape)],
        out_specs=[pl.BlockSpec((ROW_TILE, MLA_HEADS * LANES), row),
                   pl.BlockSpec((ROW_TILE, MLA_HEADS * MLA_V_DIM), row)],
        out_shape=[jax.ShapeDtypeStruct((n, MLA_HEADS * LANES), BF16),
                   jax.ShapeDtypeStruct((n, MLA_HEADS * MLA_V_DIM), BF16)],
        compiler_params=_params(),
        name="mla_shared_kv",
    )(x, pos, inv_lane, g, wd, gl, wk, wv)


def _mla_q_kernel(x_ref, pos_ref, inv_ref, g_ref, wdq_ref, gq_ref, wa_ref, wb_ref, q_ref,
                  *, heads_per_chunk):
    h = _rms(x_ref[...], g_ref[...]).astype(BF16)
    cq = _rms(_dot(h, wdq_ref[...]), gq_ref[...]).astype(BF16)
    cos_t, sin_t = _rope_tables(pos_ref[...], inv_ref[...])
    scale = (MLA_NOPE_DIM + MLA_ROPE_DIM) ** -0.5
    cos_t = cos_t * scale
    sin_t = sin_t * scale
    width = heads_per_chunk * LANES
    for c in range(MLA_HEADS // heads_per_chunk):
        sl = slice(c * width, (c + 1) * width)
        qa = _dot(cq, wa_ref[:, sl])
        qb = _dot(cq, wb_ref[:, sl])
        for j in range(heads_per_chunk):
            hs = slice(j * LANES, (j + 1) * LANES)
            q_ref[:, c * width + j * LANES:c * width + (j + 1) * LANES] = (
                qa[:, hs] * cos_t + qb[:, hs] * sin_t).astype(q_ref.dtype)


def _mla_q(x, pos, inv_lane, g, wdq, gq, wa, wb):
    n, d = x.shape
    row = lambda i: (i, 0)
    return pl.pallas_call(
        functools.partial(_mla_q_kernel, heads_per_chunk=4),
        grid=(n // ROW_TILE,),
        in_specs=[pl.BlockSpec((ROW_TILE, d), row),
                  pl.BlockSpec((ROW_TILE, 1), row),
                  _resident((1, LANES)), _resident((1, d)), _resident(wdq.shape),
                  _resident((1, MLA_Q_RANK)), _resident(wa.shape), _resident(wb.shape)],
        out_specs=pl.BlockSpec((ROW_TILE, MLA_HEADS * LANES), row),
        out_shape=jax.ShapeDtypeStruct((n, MLA_HEADS * LANES), BF16),
        compiler_params=_params(),
        name="mla_q",
    )(x, pos, inv_lane, g, wdq, gq, wa, wb)


def _mla_attn_kernel(q_ref, k_ref, v_ref, o_ref, *, tile):
    qi = pl.program_id(2)
    lane = lax.broadcasted_iota(jnp.int32, (1, LANES), 1)
    first = lane < MLA_V_DIM
    q = q_ref[0]
    q_heads = (q[:, :LANES], q[:, LANES:])
    row = lax.broadcasted_iota(jnp.int32, (tile, tile), 0)
    col = lax.broadcasted_iota(jnp.int32, (tile, tile), 1)
    allowed = (col // CHUNK) <= (row // CHUNK)

    def block(kb, carry, diag):
        m0, l0, m1, l1, acc = carry
        start = pl.multiple_of(kb * tile, tile)
        k = k_ref[0, pl.ds(start, tile), :]
        v = v_ref[0, pl.ds(start, tile), :]
        stats = []
        for j, (m, l) in enumerate(((m0, l0), (m1, l1))):
            s = _dot_nt(q_heads[j], k[:, j * LANES:(j + 1) * LANES])
            if diag:
                s = jnp.where(allowed, s, MASK_VALUE)
            m_new = jnp.maximum(m, jnp.max(s, axis=-1, keepdims=True))
            alpha = jnp.exp(m - m_new)
            p = jnp.exp(s - m_new)
            l_new = alpha * l + jnp.sum(p, axis=-1, keepdims=True)
            stats.append((m_new, l_new, alpha, _dot(p.astype(BF16), v)))
        (m0, l0, a0, pv0), (m1, l1, a1, pv1) = stats
        acc = jnp.where(first, a0, a1) * acc + jnp.where(first, pv0, pv1)
        return m0, l0, m1, l1, acc

    neg = jnp.full((tile, 1), MASK_VALUE, F32)
    zero = jnp.zeros((tile, 1), F32)
    carry = (neg, zero, neg, zero, jnp.zeros((tile, LANES), F32))
    carry = block(qi, carry, True)
    carry = lax.fori_loop(0, qi, lambda i, cr: block(i, cr, False), carry)
    _, l0, _, l1, acc = carry
    o_ref[0] = (acc / jnp.where(first, l0, l1)).astype(o_ref.dtype)


def _mla_attention(q, k, v, batch, seq):
    pairs = MLA_HEADS // 2
    t = MLA_TILE
    return pl.pallas_call(
        functools.partial(_mla_attn_kernel, tile=t),
        grid=(batch, pairs, seq // t),
        in_specs=[pl.BlockSpec((1, t, 2 * LANES), lambda b, p, i: (b, i, p)),
                  pl.BlockSpec((1, seq, 2 * LANES), lambda b, p, i: (b, 0, p)),
                  pl.BlockSpec((1, seq, LANES), lambda b, p, i: (b, 0, p))],
        out_specs=pl.BlockSpec((1, t, LANES), lambda b, p, i: (b, i, p)),
        out_shape=jax.ShapeDtypeStruct((batch, seq, MLA_HEADS * MLA_V_DIM), BF16),
        compiler_params=_params(),
        name="mla_attention",
    )(q, k, v)


def _rope_lane_freqs():
    inv_freq = ROPE_THETA ** (-jnp.arange(0, MLA_ROPE_DIM, 2, dtype=F32) / MLA_ROPE_DIM)
    half = MLA_ROPE_DIM // 2
    out = jnp.zeros((1, LANES), F32)
    out = out.at[0, MLA_NOPE_DIM:MLA_NOPE_DIM + half].set(inv_freq)
    out = out.at[0, MLA_NOPE_DIM + half:MLA_NOPE_DIM + 2 * half].set(inv_freq)
    return out


def _layout_w_dkv(w_dkv):
    half = MLA_ROPE_DIM // 2
    d = w_dkv.shape[0]
    t1 = w_dkv[:, MLA_KV_RANK:MLA_KV_RANK + half]
    t2 = w_dkv[:, MLA_KV_RANK + half:]
    pad_lo = jnp.zeros((d, MLA_NOPE_DIM), w_dkv.dtype)
    pad_hi = jnp.zeros((d, LANES - MLA_NOPE_DIM - MLA_ROPE_DIM), w_dkv.dtype)
    blk_a = jnp.concatenate([pad_lo, t1, t2, pad_hi], axis=1)
    blk_b = jnp.concatenate([pad_lo, -t2, t1, pad_hi], axis=1)
    return jnp.concatenate([w_dkv[:, :MLA_KV_RANK], blk_a, blk_b], axis=1).astype(BF16)


def _layout_w_ukv(w_ukv):
    r = w_ukv.shape[0]
    w = w_ukv.reshape(r, MLA_HEADS, MLA_NOPE_DIM + MLA_V_DIM)
    wk = jnp.concatenate(
        [w[:, :, :MLA_NOPE_DIM], jnp.zeros((r, MLA_HEADS, LANES - MLA_NOPE_DIM), w.dtype)],
        axis=2).reshape(r, MLA_HEADS * LANES)
    wv = w[:, :, MLA_NOPE_DIM:].reshape(r, MLA_HEADS * MLA_V_DIM)
    return wk.astype(BF16), wv.astype(BF16)


def _layout_w_uq(w_uq):
    r = w_uq.shape[0]
    half = MLA_ROPE_DIM // 2
    w = w_uq.reshape(r, MLA_HEADS, MLA_NOPE_DIM + MLA_ROPE_DIM)
    nope = w[:, :, :MLA_NOPE_DIM]
    t1 = w[:, :, MLA_NOPE_DIM:MLA_NOPE_DIM + half]
    t2 = w[:, :, MLA_NOPE_DIM + half:]
    pad_hi = jnp.zeros((r, MLA_HEADS, LANES - MLA_NOPE_DIM - MLA_ROPE_DIM), w.dtype)
    wa = jnp.concatenate([nope, t1, t2, pad_hi], axis=2)
    wb = jnp.concatenate([jnp.zeros_like(nope), -t2, t1, pad_hi], axis=2)
    shape = (r, MLA_HEADS * LANES)
    return wa.reshape(shape).astype(BF16), wb.reshape(shape).astype(BF16)


def kernel(x, positions, attn_norm, mlp_norm, sb_w_qkv, sb_w_o, kv_norm, mla_w_dkv,
           mla_kv_lat_norm, mla_w_ukv, mla_w_dq, mla_q_lat_norm, mla_w_uq, mla_w_o,
           mlp_w1, mlp_w2, final_norm):
    batch, seq, d = x.shape
    n = batch * seq
    xs = x.reshape(n, d)
    pos = positions.astype(F32).reshape(n, 1)
    inv_lane = _rope_lane_freqs()

    for layer in range(N_A_LAYERS):
        qkv = _norm_matmul(xs, attn_norm[layer].reshape(1, d), sb_w_qkv[layer].astype(BF16))
        attn = _sb_attention(qkv.reshape(batch, seq, 3 * d), batch, seq)
        xs = _proj_mlp(xs, attn.reshape(n, d), sb_w_o[layer].astype(BF16),
                       mlp_norm[layer].reshape(1, d),
                       mlp_w1[layer].astype(BF16), mlp_w2[layer].astype(BF16))

    wk, wv = _layout_w_ukv(mla_w_ukv)
    k_all, v_all = _shared_kv(xs, pos, inv_lane, kv_norm.reshape(1, d),
                              _layout_w_dkv(mla_w_dkv),
                              mla_kv_lat_norm.reshape(1, MLA_KV_RANK), wk, wv)
    k_all = k_all.reshape(batch, seq, MLA_HEADS * LANES)
    v_all = v_all.reshape(batch, seq, MLA_HEADS * MLA_V_DIM)

    for layer in range(N_A_LAYERS, DEPTH):
        j = layer - N_A_LAYERS
        wa, wb = _layout_w_uq(mla_w_uq[j])
        q = _mla_q(xs, pos, inv_lane, attn_norm[layer].reshape(1, d),
                   mla_w_dq[j].astype(BF16), mla_q_lat_norm[j].reshape(1, MLA_Q_RANK), wa, wb)
        attn = _mla_attention(q.reshape(batch, seq, MLA_HEADS * LANES), k_all, v_all, batch, seq)
        xs = _proj_mlp(xs, attn.reshape(n, MLA_HEADS * MLA_V_DIM), mla_w_o[j].astype(BF16),
                       mlp_norm[layer].reshape(1, d),
                       mlp_w1[layer].astype(BF16), mlp_w2[layer].astype(BF16),
                       final_g=final_norm.reshape(1, d) if layer == DEPTH - 1 else None)
    return xs.reshape(batch, seq, d)
```

```python
import functools

import jax
import jax.numpy as jnp
from jax import lax
from jax.experimental import pallas as pl
from jax.experimental.pallas import tpu as pltpu

D_MODEL = 1024
DEPTH = 4
CHUNK = 64
N_A_LAYERS = DEPTH // 2
SB_HEADS = 16
SB_HEAD_DIM = D_MODEL // SB_HEADS
MLA_HEADS = 16
MLA_NOPE_DIM = 64
MLA_ROPE_DIM = 32
MLA_V_DIM = 64
MLA_Q_RANK = 384
MLA_KV_RANK = 256
D_FF = 4 * D_MODEL
ROPE_THETA = 10000.0
NORM_EPS = 1e-6

LANES = 128
ROW_TILE = 512
SB_TILE = 128
SB_HEADS_PER_STEP = 8
SB_DEAD_DROP = 104.0
SB_VOID = 1e30
SB_FIRST_SWEEP = 3
SB_LOOP_SWEEP = 2
MLA_TILE = 512
VMEM_LIMIT = 56 * 1024 * 1024
MASK_VALUE = -1e30

F32 = jnp.float32
BF16 = jnp.bfloat16


def _rms(x, g):
    return x * lax.rsqrt(jnp.mean(x * x, axis=-1, keepdims=True) + NORM_EPS) * g


def _dot(a, b):
    return jnp.dot(a, b, preferred_element_type=F32)


def _dot_nt(a, b):
    return lax.dot_general(a, b, (((1,), (1,)), ((), ())), preferred_element_type=F32)


def _params():
    return pltpu.CompilerParams(
        dimension_semantics=None, vmem_limit_bytes=VMEM_LIMIT)


def _resident(shape):
    return pl.BlockSpec(shape, lambda *_: (0,) * len(shape),
                        pipeline_mode=pl.Buffered(1))


def _norm_matmul_kernel(x_ref, g_ref, w_ref, o_ref, *, col_chunk):
    h = _rms(x_ref[...], g_ref[...]).astype(BF16)
    n_out = o_ref.shape[1]
    for c in range(n_out // col_chunk):
        sl = slice(c * col_chunk, (c + 1) * col_chunk)
        o_ref[:, sl] = _dot(h, w_ref[:, sl]).astype(o_ref.dtype)


def _norm_matmul(x, g, w):
    n, d = x.shape
    n_out = w.shape[1]
    return pl.pallas_call(
        functools.partial(_norm_matmul_kernel, col_chunk=1024),
        grid=(n // ROW_TILE,),
        in_specs=[pl.BlockSpec((ROW_TILE, d), lambda i: (i, 0)),
                  _resident((1, d)),
                  _resident((d, n_out))],
        out_specs=pl.BlockSpec((ROW_TILE, n_out), lambda i: (i, 0)),
        out_shape=jax.ShapeDtypeStruct((n, n_out), BF16),
        compiler_params=_params(),
        name="norm_qkv",
    )(x, g, w)


def _sb_attn_kernel(q_ref, k_ref, v_ref, o_ref, *, tile, heads):
    qi = pl.program_id(2)
    pairs = heads // 2
    lane = lax.broadcasted_iota(jnp.int32, (1, LANES), 1)
    halves = (lane < SB_HEAD_DIM, lane >= SB_HEAD_DIM)
    row = lax.broadcasted_iota(jnp.int32, (tile, tile), 0)
    col = lax.broadcasted_iota(jnp.int32, (tile, tile), 1)
    from_key = (row >= col).astype(BF16)
    from_key2 = jnp.concatenate([from_key, from_key], axis=0)
    causal = col < row
    causal2 = jnp.concatenate([causal, causal], axis=0)

    def masked(x, keep):
        return jnp.where(keep, x, jnp.zeros_like(x))

    def blk(j):
        return slice(j * LANES, (j + 1) * LANES)

    q_pairs = []
    for j in range(pairs):
        q = q_ref[0, :, blk(j)] * (SB_HEAD_DIM ** -0.5)
        q_pairs.append(jnp.concatenate([masked(q, halves[0]), masked(q, halves[1])], axis=0))

    def sweep(kb0, count, accs, cs, diag_first):
        kbs = [kb0 - i for i in range(count)]
        starts = [pl.multiple_of(jnp.maximum(kb, 0) * tile, tile) for kb in kbs]
        voids = [jnp.where(kb >= 0, 0.0, SB_VOID) for kb in kbs]
        zs = [[_dot_nt(q_pairs[j], k_ref[0, pl.ds(st, tile), blk(j)])
               for j in range(pairs)] for st in starts]
        splits = []
        for i in range(count):
            for z in zs[i]:
                neg_abs = lax.bitcast_convert_type(
                    lax.bitcast_convert_type(z, jnp.uint32) | jnp.uint32(0x80000000), F32)
                drop = jnp.maximum(z, 0.0) + jnp.log(1.0 + jnp.exp(neg_abs))
                if diag_first and i == 0:
                    drop = jnp.where(causal2, drop, 0.0)
                hi = drop.astype(BF16)
                lo = (drop - hi.astype(F32)).astype(BF16)
                splits.append(jnp.concatenate([hi, lo], axis=1))
        totals = _dot(jnp.concatenate(splits, axis=0), from_key2)
        accs, cs = list(accs), list(cs)
        for i in range(count):
            for j in range(pairs):
                base = (i * pairs + j) * 2 * tile
                total = totals[base:base + 2 * tile]
                cs[j] = cs[j] + voids[i]
                w = jnp.exp(zs[i][j] - total - cs[j])
                if diag_first and i == 0:
                    w = jnp.where(causal2, w, 0.0)
                w = w.astype(BF16)
                v = v_ref[0, pl.ds(starts[i], tile), blk(j)]
                accs[j] = accs[j] + _dot(
                    jnp.concatenate([w[:tile], w[tile:]], axis=1),
                    jnp.concatenate([masked(v, halves[0]), masked(v, halves[1])], axis=0))
                cs[j] = cs[j] + total[:, 0:1]
        return tuple(accs), tuple(cs)

    def alive_flag(cs):
        low = functools.reduce(jnp.minimum, cs)
        return (jnp.min(low) < SB_DEAD_DROP).astype(jnp.int32)

    accs = tuple(jnp.zeros((tile, LANES), F32) for _ in range(pairs))
    cs = tuple(jnp.zeros((2 * tile, 1), F32) for _ in range(pairs))
    accs, cs = sweep(qi, SB_FIRST_SWEEP, accs, cs, True)

    def cond(state):
        kb, _, _, alive = state
        return jnp.logical_and(kb >= 0, alive > 0)

    def body(state):
        kb, accs, cs, _ = state
        accs, cs = sweep(kb, SB_LOOP_SWEEP, accs, cs, False)
        return kb - SB_LOOP_SWEEP, accs, cs, alive_flag(cs)

    _, accs, _, _ = lax.while_loop(
        cond, body, (qi - SB_FIRST_SWEEP, accs, cs, alive_flag(cs)))
    for j, acc in enumerate(accs):
        o_ref[0, :, blk(j)] = acc.astype(o_ref.dtype)


def _sb_attention(qkv, batch, seq):
    width = SB_HEADS_PER_STEP * SB_HEAD_DIM
    groups = D_MODEL // width
    t = SB_TILE
    return pl.pallas_call(
        functools.partial(_sb_attn_kernel, tile=t, heads=SB_HEADS_PER_STEP),
        grid=(batch, groups, seq // t),
        in_specs=[pl.BlockSpec((1, t, width), lambda b, p, i: (b, i, p)),
                  pl.BlockSpec((1, seq, width), lambda b, p, i: (b, 0, groups + p)),
                  pl.BlockSpec((1, seq, width), lambda b, p, i: (b, 0, 2 * groups + p))],
        out_specs=pl.BlockSpec((1, t, width), lambda b, p, i: (b, i, p)),
        out_shape=jax.ShapeDtypeStruct((batch, seq, D_MODEL), BF16),
        compiler_params=_params(),
        name="sb_attention",
    )(qkv, qkv, qkv)


def _proj_mlp_kernel(x_ref, a_ref, wo_ref, g_ref, w1_ref, w2_ref, *rest, ff_chunk, final):
    if final:
        gf_ref, o_ref = rest
    else:
        (o_ref,) = rest
    x1 = x_ref[...] + _dot(a_ref[...], wo_ref[...])
    h = _rms(x1, g_ref[...]).astype(BF16)
    acc = x1
    for c in range(D_FF // ff_chunk):
        sl = slice(c * ff_chunk, (c + 1) * ff_chunk)
        u = jnp.maximum(_dot(h, w1_ref[:, sl]), 0.0)
        acc = acc + _dot((u * u).astype(BF16), w2_ref[sl, :])
    if final:
        acc = _rms(acc, gf_ref[...])
    o_ref[...] = acc


def _proj_mlp(x, a, wo, g, w1, w2, final_g=None):
    n, d = x.shape
    final = final_g is not None
    row = lambda i: (i, 0)
    in_specs = [pl.BlockSpec((ROW_TILE, d), row),
                pl.BlockSpec((ROW_TILE, a.shape[1]), row),
                _resident(wo.shape), _resident((1, d)),
                _resident(w1.shape), _resident(w2.shape)]
    args = [x, a, wo, g, w1, w2]
    if final:
        in_specs.append(_resident((1, d)))
        args.append(final_g)
    return pl.pallas_call(
        functools.partial(_proj_mlp_kernel, ff_chunk=1024, final=final),
        grid=(n // ROW_TILE,),
        in_specs=in_specs,
        out_specs=pl.BlockSpec((ROW_TILE, d), row),
        out_shape=jax.ShapeDtypeStruct((n, d), F32),
        compiler_params=_params(),
        name="proj_mlp_final" if final else "proj_mlp",
    )(*args)


def _rope_tables(pos, inv_lane):
    lane = lax.broadcasted_iota(jnp.int32, (1, LANES), 1)
    rope = jnp.logical_and(lane >= MLA_NOPE_DIM, lane < MLA_NOPE_DIM + MLA_ROPE_DIM)
    ang = pos * inv_lane
    cos_t = jnp.where(lane < MLA_NOPE_DIM, 1.0, jnp.where(rope, jnp.cos(ang), 0.0))
    sin_t = jnp.where(rope, jnp.sin(ang), 0.0)
    return cos_t, sin_t


def _shared_kv_kernel(x_ref, pos_ref, inv_ref, g_ref, wd_ref, gl_ref, wk_ref, wv_ref,
                      k_ref, v_ref):
    h = _rms(x_ref[...], g_ref[...]).astype(BF16)
    down = _dot(h, wd_ref[...])
    c = _rms(down[:, :MLA_KV_RANK], gl_ref[...]).astype(BF16)
    cos_t, sin_t = _rope_tables(pos_ref[...], inv_ref[...])
    k_rope = (down[:, MLA_KV_RANK:MLA_KV_RANK + LANES] * cos_t
              + down[:, MLA_KV_RANK + LANES:] * sin_t)
    k_nope = _dot(c, wk_ref[...])
    for hd in range(MLA_HEADS):
        sl = slice(hd * LANES, (hd + 1) * LANES)
        k_ref[:, sl] = (k_nope[:, sl] + k_rope).astype(k_ref.dtype)
    v_ref[...] = _dot(c, wv_ref[...]).astype(v_ref.dtype)


def _shared_kv(x, pos, inv_lane, g, wd, gl, wk, wv):
    n, d = x.shape
    row = lambda i: (i, 0)
    return pl.pallas_call(
        _shared_kv_kernel,
        grid=(n // ROW_TILE,),
        in_specs=[pl.BlockSpec((ROW_TILE, d), row),
                  pl.BlockSpec((ROW_TILE, 1), row),
                  _resident((1, LANES)), _resident((1, d)), _resident(wd.shape),
                  _resident((1, MLA_KV_RANK)), _resident(wk.shape), _resident(wv.shape)],
        out_specs=[pl.BlockSpec((ROW_TILE, MLA_HEADS * LANES), row),
                   pl.BlockSpec((ROW_TILE, MLA_HEADS * MLA_V_DIM), row)],
        out_shape=[jax.ShapeDtypeStruct((n, MLA_HEADS * LANES), BF16),
                   jax.ShapeDtypeStruct((n, MLA_HEADS * MLA_V_DIM), BF16)],
        compiler_params=_params(),
        name="mla_shared_kv",
    )(x, pos, inv_lane, g, wd, gl, wk, wv)


def _mla_q_kernel(x_ref, pos_ref, inv_ref, g_ref, wdq_ref, gq_ref, wa_ref, wb_ref, q_ref,
                  *, heads_per_chunk):
    h = _rms(x_ref[...], g_ref[...]).astype(BF16)
    cq = _rms(_dot(h, wdq_ref[...]), gq_ref[...]).astype(BF16)
    cos_t, sin_t = _rope_tables(pos_ref[...], inv_ref[...])
    scale = (MLA_NOPE_DIM + MLA_ROPE_DIM) ** -0.5
    cos_t = cos_t * scale
    sin_t = sin_t * scale
    width = heads_per_chunk * LANES
    for c in range(MLA_HEADS // heads_per_chunk):
        sl = slice(c * width, (c + 1) * width)
        qa = _dot(cq, wa_ref[:, sl])
        qb = _dot(cq, wb_ref[:, sl])
        for j in range(heads_per_chunk):
            hs = slice(j * LANES, (j + 1) * LANES)
            q_ref[:, c * width + j * LANES:c * width + (j + 1) * LANES] = (
                qa[:, hs] * cos_t + qb[:, hs] * sin_t).astype(q_ref.dtype)


def _mla_q(x, pos, inv_lane, g, wdq, gq, wa, wb):
    n, d = x.shape
    row = lambda i: (i, 0)
    return pl.pallas_call(
        functools.partial(_mla_q_kernel, heads_per_chunk=4),
        grid=(n // ROW_TILE,),
        in_specs=[pl.BlockSpec((ROW_TILE, d), row),
                  pl.BlockSpec((ROW_TILE, 1), row),
                  _resident((1, LANES)), _resident((1, d)), _resident(wdq.shape),
                  _resident((1, MLA_Q_RANK)), _resident(wa.shape), _resident(wb.shape)],
        out_specs=pl.BlockSpec((ROW_TILE, MLA_HEADS * LANES), row),
        out_shape=jax.ShapeDtypeStruct((n, MLA_HEADS * LANES), BF16),
        compiler_params=_params(),
        name="mla_q",
    )(x, pos, inv_lane, g, wdq, gq, wa, wb)


def _mla_attn_kernel(q_ref, k_ref, v_ref, o_ref, *, tile):
    qi = pl.program_id(2)
    lane = lax.broadcasted_iota(jnp.int32, (1, LANES), 1)
    first = lane < MLA_V_DIM
    q = q_ref[0]
    q_heads = (q[:, :LANES], q[:, LANES:])
    row = lax.broadcasted_iota(jnp.int32, (tile, tile), 0)
    col = lax.broadcasted_iota(jnp.int32, (tile, tile), 1)
    allowed = (col // CHUNK) <= (row // CHUNK)

    def block(kb, carry, diag):
        m0, l0, m1, l1, acc = carry
        start = pl.multiple_of(kb * tile, tile)
        k = k_ref[0, pl.ds(start, tile), :]
        v = v_ref[0, pl.ds(start, tile), :]
        stats = []
        for j, (m, l) in enumerate(((m0, l0), (m1, l1))):
            s = _dot_nt(q_heads[j], k[:, j * LANES:(j + 1) * LANES])
            if diag:
                s = jnp.where(allowed, s, MASK_VALUE)
            m_new = jnp.maximum(m, jnp.max(s, axis=-1, keepdims=True))
            alpha = jnp.exp(m - m_new)
            p = jnp.exp(s - m_new)
            l_new = alpha * l + jnp.sum(p, axis=-1, keepdims=True)
            stats.append((m_new, l_new, alpha, _dot(p.astype(BF16), v)))
        (m0, l0, a0, pv0), (m1, l1, a1, pv1) = stats
        acc = jnp.where(first, a0, a1) * acc + jnp.where(first, pv0, pv1)
        return m0, l0, m1, l1, acc

    neg = jnp.full((tile, 1), MASK_VALUE, F32)
    zero = jnp.zeros((tile, 1), F32)
    carry = (neg, zero, neg, zero, jnp.zeros((tile, LANES), F32))
    carry = block(qi, carry, True)
    carry = lax.fori_loop(0, qi, lambda i, cr: block(i, cr, False), carry)
    _, l0, _, l1, acc = carry
    o_ref[0] = (acc / jnp.where(first, l0, l1)).astype(o_ref.dtype)


def _mla_attention(q, k, v, batch, seq):
    pairs = MLA_HEADS // 2
    t = MLA_TILE
    return pl.pallas_call(
        functools.partial(_mla_attn_kernel, tile=t),
        grid=(batch, pairs, seq // t),
        in_specs=[pl.BlockSpec((1, t, 2 * LANES), lambda b, p, i: (b, i, p)),
                  pl.BlockSpec((1, seq, 2 * LANES), lambda b, p, i: (b, 0, p)),
                  pl.BlockSpec((1, seq, LANES), lambda b, p, i: (b, 0, p))],
        out_specs=pl.BlockSpec((1, t, LANES), lambda b, p, i: (b, i, p)),
        out_shape=jax.ShapeDtypeStruct((batch, seq, MLA_HEADS * MLA_V_DIM), BF16),
        compiler_params=_params(),
        name="mla_attention",
    )(q, k, v)


def _rope_lane_freqs():
    inv_freq = ROPE_THETA ** (-jnp.arange(0, MLA_ROPE_DIM, 2, dtype=F32) / MLA_ROPE_DIM)
    half = MLA_ROPE_DIM // 2
    out = jnp.zeros((1, LANES), F32)
    out = out.at[0, MLA_NOPE_DIM:MLA_NOPE_DIM + half].set(inv_freq)
    out = out.at[0, MLA_NOPE_DIM + half:MLA_NOPE_DIM + 2 * half].set(inv_freq)
    return out


def _layout_w_dkv(w_dkv):
    half = MLA_ROPE_DIM // 2
    d = w_dkv.shape[0]
    t1 = w_dkv[:, MLA_KV_RANK:MLA_KV_RANK + half]
    t2 = w_dkv[:, MLA_KV_RANK + half:]
    pad_lo = jnp.zeros((d, MLA_NOPE_DIM), w_dkv.dtype)
    pad_hi = jnp.zeros((d, LANES - MLA_NOPE_DIM - MLA_ROPE_DIM), w_dkv.dtype)
    blk_a = jnp.concatenate([pad_lo, t1, t2, pad_hi], axis=1)
    blk_b = jnp.concatenate([pad_lo, -t2, t1, pad_hi], axis=1)
    return jnp.concatenate([w_dkv[:, :MLA_KV_RANK], blk_a, blk_b], axis=1).astype(BF16)


def _layout_w_ukv(w_ukv):
    r = w_ukv.shape[0]
    w = w_ukv.reshape(r, MLA_HEADS, MLA_NOPE_DIM + MLA_V_DIM)
    wk = jnp.concatenate(
        [w[:, :, :MLA_NOPE_DIM], jnp.zeros((r, MLA_HEADS, LANES - MLA_NOPE_DIM), w.dtype)],
        axis=2).reshape(r, MLA_HEADS * LANES)
    wv = w[:, :, MLA_NOPE_DIM:].reshape(r, MLA_HEADS * MLA_V_DIM)
    return wk.astype(BF16), wv.astype(BF16)


def _layout_w_uq(w_uq):
    r = w_uq.shape[0]
    half = MLA_ROPE_DIM // 2
    w = w_uq.reshape(r, MLA_HEADS, MLA_NOPE_DIM + MLA_ROPE_DIM)
    nope = w[:, :, :MLA_NOPE_DIM]
    t1 = w[:, :, MLA_NOPE_DIM:MLA_NOPE_DIM + half]
    t2 = w[:, :, MLA_NOPE_DIM + half:]
    pad_hi = jnp.zeros((r, MLA_HEADS, LANES - MLA_NOPE_DIM - MLA_ROPE_DIM), w.dtype)
    wa = jnp.concatenate([nope, t1, t2, pad_hi], axis=2)
    wb = jnp.concatenate([jnp.zeros_like(nope), -t2, t1, pad_hi], axis=2)
    shape = (r, MLA_HEADS * LANES)
    return wa.reshape(shape).astype(BF16), wb.reshape(shape).astype(BF16)


def kernel(x, positions, attn_norm, mlp_norm, sb_w_qkv, sb_w_o, kv_norm, mla_w_dkv,
           mla_kv_lat_norm, mla_w_ukv, mla_w_dq, mla_q_lat_norm, mla_w_uq, mla_w_o,
           mlp_w1, mlp_w2, final_norm):
    batch, seq, d = x.shape
    n = batch * seq
    xs = x.reshape(n, d)
    pos = positions.astype(F32).reshape(n, 1)
    inv_lane = _rope_lane_freqs()

    for layer in range(N_A_LAYERS):
        qkv = _norm_matmul(xs, attn_norm[layer].reshape(1, d), sb_w_qkv[layer].astype(BF16))
        attn = _sb_attention(qkv.reshape(batch, seq, 3 * d), batch, seq)
        xs = _proj_mlp(xs, attn.reshape(n, d), sb_w_o[layer].astype(BF16),
                       mlp_norm[layer].reshape(1, d),
                       mlp_w1[layer].astype(BF16), mlp_w2[layer].astype(BF16))

    wk, wv = _layout_w_ukv(mla_w_ukv)
    k_all, v_all = _shared_kv(xs, pos, inv_lane, kv_norm.reshape(1, d),
                              _layout_w_dkv(mla_w_dkv),
                              mla_kv_lat_norm.reshape(1, MLA_KV_RANK), wk, wv)
    k_all = k_all.reshape(batch, seq, MLA_HEADS * LANES)
    v_all = v_all.reshape(batch, seq, MLA_HEADS * MLA_V_DIM)

    for layer in range(N_A_LAYERS, DEPTH):
        j = layer - N_A_LAYERS
        wa, wb = _layout_w_uq(mla_w_uq[j])
        q = _mla_q(xs, pos, inv_lane, attn_norm[layer].reshape(1, d),
                   mla_w_dq[j].astype(BF16), mla_q_lat_norm[j].reshape(1, MLA_Q_RANK), wa, wb)
        attn = _mla_attention(q.reshape(batch, seq, MLA_HEADS * LANES), k_all, v_all, batch, seq)
        xs = _proj_mlp(xs, attn.reshape(n, MLA_HEADS * MLA_V_DIM), mla_w_o[j].astype(BF16),
                       mlp_norm[layer].reshape(1, d),
                       mlp_w1[layer].astype(BF16), mlp_w2[layer].astype(BF16),
                       final_g=final_norm.reshape(1, d) if layer == DEPTH - 1 else None)
    return xs.reshape(batch, seq, d)
```

```python
import functools
import math

import jax
import jax.numpy as jnp
from jax import lax
from jax.experimental import pallas as pl
from jax.experimental.pallas import tpu as pltpu

D_MODEL = 1024
DEPTH = 4
CHUNK = 64
N_A_LAYERS = DEPTH // 2
SB_HEADS = 16
SB_HEAD_DIM = D_MODEL // SB_HEADS
MLA_HEADS = 16
MLA_NOPE_DIM = 64
MLA_ROPE_DIM = 32
MLA_V_DIM = 64
MLA_Q_RANK = 384
MLA_KV_RANK = 256
D_FF = 4 * D_MODEL
ROPE_THETA = 10000.0
NORM_EPS = 1e-6

LANES = 128
ROW_TILE = 512
SB_TILE = 128
SB_HEADS_PER_STEP = 8
SB_DEAD_DROP = 104.0
SB_VOID = 1e30
SB_FIRST_SWEEP = 3
SB_LOOP_SWEEP = 2
MLA_TILE = 512
MLA_HEADS_PER_STEP = 4
VMEM_LIMIT = 56 * 1024 * 1024
MASK_VALUE = -1e30

F32 = jnp.float32
BF16 = jnp.bfloat16


def _rms(x, g):
    return x * lax.rsqrt(jnp.mean(x * x, axis=-1, keepdims=True) + NORM_EPS) * g


def _dot(a, b):
    return jnp.dot(a, b, preferred_element_type=F32)


def _dot_nt(a, b):
    return lax.dot_general(a, b, (((1,), (1,)), ((), ())), preferred_element_type=F32)


def _params():
    return pltpu.CompilerParams(
        dimension_semantics=None, vmem_limit_bytes=VMEM_LIMIT)


def _resident(shape):
    return pl.BlockSpec(shape, lambda *_: (0,) * len(shape),
                        pipeline_mode=pl.Buffered(1))


def _norm_matmul_kernel(x_ref, g_ref, w_ref, o_ref, *, col_chunk):
    h = _rms(x_ref[...], g_ref[...]).astype(BF16)
    n_out = o_ref.shape[1]
    for c in range(n_out // col_chunk):
        sl = slice(c * col_chunk, (c + 1) * col_chunk)
        o_ref[:, sl] = _dot(h, w_ref[:, sl]).astype(o_ref.dtype)


def _norm_matmul(x, g, w):
    n, d = x.shape
    n_out = w.shape[1]
    return pl.pallas_call(
        functools.partial(_norm_matmul_kernel, col_chunk=1024),
        grid=(n // ROW_TILE,),
        in_specs=[pl.BlockSpec((ROW_TILE, d), lambda i: (i, 0)),
                  _resident((1, d)),
                  _resident((d, n_out))],
        out_specs=pl.BlockSpec((ROW_TILE, n_out), lambda i: (i, 0)),
        out_shape=jax.ShapeDtypeStruct((n, n_out), BF16),
        compiler_params=_params(),
        name="norm_qkv",
    )(x, g, w)


def _sb_attn_kernel(q_ref, k_ref, v_ref, o_ref, *, tile, heads):
    qi = pl.program_id(2)
    pairs = heads // 2
    lane = lax.broadcasted_iota(jnp.int32, (1, LANES), 1)
    halves = (lane < SB_HEAD_DIM, lane >= SB_HEAD_DIM)
    row = lax.broadcasted_iota(jnp.int32, (tile, tile), 0)
    col = lax.broadcasted_iota(jnp.int32, (tile, tile), 1)
    from_key = (row >= col).astype(BF16)
    from_key2 = jnp.concatenate([from_key, from_key], axis=0)
    causal = col < row
    causal2 = jnp.concatenate([causal, causal], axis=0)

    def masked(x, keep):
        return jnp.where(keep, x, jnp.zeros_like(x))

    def blk(j):
        return slice(j * LANES, (j + 1) * LANES)

    q_pairs = []
    for j in range(pairs):
        q = q_ref[0, :, blk(j)] * (SB_HEAD_DIM ** -0.5)
        q_pairs.append(jnp.concatenate([masked(q, halves[0]), masked(q, halves[1])], axis=0))

    def sweep(kb0, count, accs, cs, diag_first):
        kbs = [kb0 - i for i in range(count)]
        starts = [pl.multiple_of(jnp.maximum(kb, 0) * tile, tile) for kb in kbs]
        voids = [jnp.where(kb >= 0, 0.0, SB_VOID) for kb in kbs]
        zs = [[_dot_nt(q_pairs[j], k_ref[0, pl.ds(st, tile), blk(j)])
               for j in range(pairs)] for st in starts]
        splits = []
        for i in range(count):
            for z in zs[i]:
                drop = jnp.maximum(z, 0.0) + jnp.log(1.0 + jnp.exp(-jnp.abs(z)))
                if diag_first and i == 0:
                    drop = jnp.where(causal2, drop, 0.0)
                hi = drop.astype(BF16)
                lo = (drop - hi.astype(F32)).astype(BF16)
                splits.append(jnp.concatenate([hi, lo], axis=1))
        totals = _dot(jnp.concatenate(splits, axis=0), from_key2)
        accs, cs = list(accs), list(cs)
        for i in range(count):
            for j in range(pairs):
                base = (i * pairs + j) * 2 * tile
                total = totals[base:base + 2 * tile]
                cs[j] = cs[j] + voids[i]
                w = jnp.exp(zs[i][j] - total - cs[j])
                if diag_first and i == 0:
                    w = jnp.where(causal2, w, 0.0)
                w = w.astype(BF16)
                v = v_ref[0, pl.ds(starts[i], tile), blk(j)]
                accs[j] = accs[j] + _dot(
                    jnp.concatenate([w[:tile], w[tile:]], axis=1),
                    jnp.concatenate([masked(v, halves[0]), masked(v, halves[1])], axis=0))
                cs[j] = cs[j] + total[:, 0:1]
        return tuple(accs), tuple(cs)

    def alive_flag(cs):
        low = functools.reduce(jnp.minimum, cs)
        return (jnp.min(low) < SB_DEAD_DROP).astype(jnp.int32)

    accs = tuple(jnp.zeros((tile, LANES), F32) for _ in range(pairs))
    cs = tuple(jnp.zeros((2 * tile, 1), F32) for _ in range(pairs))
    accs, cs = sweep(qi, SB_FIRST_SWEEP, accs, cs, True)

    def cond(state):
        kb, _, _, alive = state
        return jnp.logical_and(kb >= 0, alive > 0)

    def body(state):
        kb, accs, cs, _ = state
        accs, cs = sweep(kb, SB_LOOP_SWEEP, accs, cs, False)
        return kb - SB_LOOP_SWEEP, accs, cs, alive_flag(cs)

    _, accs, _, _ = lax.while_loop(
        cond, body, (qi - SB_FIRST_SWEEP, accs, cs, alive_flag(cs)))
    for j, acc in enumerate(accs):
        o_ref[0, :, blk(j)] = acc.astype(o_ref.dtype)


def _sb_attention(qkv, batch, seq):
    width = SB_HEADS_PER_STEP * SB_HEAD_DIM
    groups = D_MODEL // width
    t = SB_TILE
    return pl.pallas_call(
        functools.partial(_sb_attn_kernel, tile=t, heads=SB_HEADS_PER_STEP),
        grid=(batch, groups, seq // t),
        in_specs=[pl.BlockSpec((1, t, width), lambda b, p, i: (b, i, p)),
                  pl.BlockSpec((1, seq, width), lambda b, p, i: (b, 0, groups + p)),
                  pl.BlockSpec((1, seq, width), lambda b, p, i: (b, 0, 2 * groups + p))],
        out_specs=pl.BlockSpec((1, t, width), lambda b, p, i: (b, i, p)),
        out_shape=jax.ShapeDtypeStruct((batch, seq, D_MODEL), BF16),
        compiler_params=_params(),
        name="sb_attention",
    )(qkv, qkv, qkv)


def _proj_mlp_kernel(x_ref, a_ref, wo_ref, g_ref, w1_ref, w2_ref, *rest, ff_chunk, final):
    if final:
        gf_ref, o_ref = rest
    else:
        (o_ref,) = rest
    x1 = x_ref[...] + _dot(a_ref[...], wo_ref[...])
    h = _rms(x1, g_ref[...]).astype(BF16)
    acc = x1
    for c in range(D_FF // ff_chunk):
        sl = slice(c * ff_chunk, (c + 1) * ff_chunk)
        u = jnp.maximum(_dot(h, w1_ref[:, sl]), 0.0)
        acc = acc + _dot((u * u).astype(BF16), w2_ref[sl, :])
    if final:
        acc = _rms(acc, gf_ref[...])
    o_ref[...] = acc


def _proj_mlp(x, a, wo, g, w1, w2, final_g=None):
    n, d = x.shape
    final = final_g is not None
    row = lambda i: (i, 0)
    in_specs = [pl.BlockSpec((ROW_TILE, d), row),
                pl.BlockSpec((ROW_TILE, a.shape[1]), row),
                _resident(wo.shape), _resident((1, d)),
                _resident(w1.shape), _resident(w2.shape)]
    args = [x, a, wo, g, w1, w2]
    if final:
        in_specs.append(_resident((1, d)))
        args.append(final_g)
    return pl.pallas_call(
        functools.partial(_proj_mlp_kernel, ff_chunk=1024, final=final),
        grid=(n // ROW_TILE,),
        in_specs=in_specs,
        out_specs=pl.BlockSpec((ROW_TILE, d), row),
        out_shape=jax.ShapeDtypeStruct((n, d), F32),
        compiler_params=_params(),
        name="proj_mlp_final" if final else "proj_mlp",
    )(*args)


def _rope_tables(pos, inv_lane):
    lane = lax.broadcasted_iota(jnp.int32, (1, LANES), 1)
    rope = jnp.logical_and(lane >= MLA_NOPE_DIM, lane < MLA_NOPE_DIM + MLA_ROPE_DIM)
    ang = pos * inv_lane
    cos_t = jnp.where(lane < MLA_NOPE_DIM, 1.0, jnp.where(rope, jnp.cos(ang), 0.0))
    sin_t = jnp.where(rope, jnp.sin(ang), 0.0)
    return cos_t, sin_t


def _shared_kv_kernel(x_ref, pos_ref, inv_ref, g_ref, wd_ref, gl_ref, wk_ref, wv_ref,
                      k_ref, v_ref):
    h = _rms(x_ref[...], g_ref[...]).astype(BF16)
    down = _dot(h, wd_ref[...])
    c = _rms(down[:, :MLA_KV_RANK], gl_ref[...]).astype(BF16)
    cos_t, sin_t = _rope_tables(pos_ref[...], inv_ref[...])
    k_rope = (down[:, MLA_KV_RANK:MLA_KV_RANK + LANES] * cos_t
              + down[:, MLA_KV_RANK + LANES:] * sin_t)
    k_nope = _dot(c, wk_ref[...])
    for hd in range(MLA_HEADS):
        sl = slice(hd * LANES, (hd + 1) * LANES)
        k_ref[:, sl] = (k_nope[:, sl] + k_rope).astype(k_ref.dtype)
    v_ref[...] = _dot(c, wv_ref[...]).astype(v_ref.dtype)


def _shared_kv(x, pos, inv_lane, g, wd, gl, wk, wv):
    n, d = x.shape
    row = lambda i: (i, 0)
    return pl.pallas_call(
        _shared_kv_kernel,
        grid=(n // ROW_TILE,),
        in_specs=[pl.BlockSpec((ROW_TILE, d), row),
                  pl.BlockSpec((ROW_TILE, 1), row),
                  _resident((1, LANES)), _resident((1, d)), _resident(wd.shape),
                  _resident((1, MLA_KV_RANK)), _resident(wk.shape), _resident(wv.shape)],
        out_specs=[pl.BlockSpec((ROW_TILE, MLA_HEADS * LANES), row),
                   pl.BlockSpec((ROW_TILE, MLA_HEADS * MLA_V_DIM), row)],
        out_shape=[jax.ShapeDtypeStruct((n, MLA_HEADS * LANES), BF16),
                   jax.ShapeDtypeStruct((n, MLA_HEADS * MLA_V_DIM), BF16)],
        compiler_params=_params(),
        name="mla_shared_kv",
    )(x, pos, inv_lane, g, wd, gl, wk, wv)


def _mla_q_kernel(x_ref, pos_ref, inv_ref, g_ref, wdq_ref, gq_ref, wa_ref, wb_ref, q_ref,
                  *, heads_per_chunk):
    h = _rms(x_ref[...], g_ref[...]).astype(BF16)
    cq = _rms(_dot(h, wdq_ref[...]), gq_ref[...]).astype(BF16)
    cos_t, sin_t = _rope_tables(pos_ref[...], inv_ref[...])
    scale = (MLA_NOPE_DIM + MLA_ROPE_DIM) ** -0.5 * math.log2(math.e)
    cos_t = cos_t * scale
    sin_t = sin_t * scale
    width = heads_per_chunk * LANES
    for c in range(MLA_HEADS // heads_per_chunk):
        sl = slice(c * width, (c + 1) * width)
        qa = _dot(cq, wa_ref[:, sl])
        qb = _dot(cq, wb_ref[:, sl])
        for j in range(heads_per_chunk):
            hs = slice(j * LANES, (j + 1) * LANES)
            q_ref[:, c * width + j * LANES:c * width + (j + 1) * LANES] = (
                qa[:, hs] * cos_t + qb[:, hs] * sin_t).astype(q_ref.dtype)


def _mla_q(x, pos, inv_lane, g, wdq, gq, wa, wb):
    n, d = x.shape
    row = lambda i: (i, 0)
    return pl.pallas_call(
        functools.partial(_mla_q_kernel, heads_per_chunk=4),
        grid=(n // ROW_TILE,),
        in_specs=[pl.BlockSpec((ROW_TILE, d), row),
                  pl.BlockSpec((ROW_TILE, 1), row),
                  _resident((1, LANES)), _resident((1, d)), _resident(wdq.shape),
                  _resident((1, MLA_Q_RANK)), _resident(wa.shape), _resident(wb.shape)],
        out_specs=pl.BlockSpec((ROW_TILE, MLA_HEADS * LANES), row),
        out_shape=jax.ShapeDtypeStruct((n, MLA_HEADS * LANES), BF16),
        compiler_params=_params(),
        name="mla_q",
    )(x, pos, inv_lane, g, wdq, gq, wa, wb)


def _mla_attn_kernel(q_ref, k_ref, v_ref, o_ref, *, tile, heads):
    qi = pl.program_id(2)
    pairs = heads // 2
    lane = lax.broadcasted_iota(jnp.int32, (1, LANES), 1)
    halves = (lane < MLA_V_DIM, lane >= MLA_V_DIM)
    row = lax.broadcasted_iota(jnp.int32, (tile, tile), 0)
    col = lax.broadcasted_iota(jnp.int32, (tile, tile), 1)
    allowed = (col // CHUNK) <= (row // CHUNK)

    def masked(x, keep):
        return jnp.where(keep, x, jnp.zeros_like(x))

    def blk(j):
        return slice(j * LANES, (j + 1) * LANES)

    key_row = lax.broadcasted_iota(jnp.int32, (2 * tile, 1), 0)
    ones_cols = jnp.where((key_row < tile) == halves[0], 1.0, 0.0).astype(BF16)

    def block(kb, carry, diag):
        ms, accs = carry
        start = pl.multiple_of(kb * tile, tile)
        ss = [_dot_nt(q_ref[0, :, blk(h)], k_ref[0, pl.ds(start, tile), blk(h)])
              for h in range(heads)]
        ps, alphas, new_ms = [], [], []
        for h in range(heads):
            s = ss[h]
            if diag:
                s = jnp.where(allowed, s, MASK_VALUE)
            m_new = jnp.maximum(ms[h], jnp.max(s, axis=-1, keepdims=True))
            alphas.append(jnp.exp2(ms[h] - m_new))
            new_ms.append(m_new)
            ps.append(jnp.exp2(s - m_new).astype(BF16))
        new_accs = []
        for j in range(pairs):
            v = v_ref[0, pl.ds(start, tile), blk(j)]
            vals = jnp.concatenate([masked(v, halves[0]), masked(v, halves[1])], axis=0)
            pv = _dot(jnp.concatenate([ps[2 * j], ps[2 * j + 1]], axis=1),
                      jnp.concatenate([vals, ones_cols], axis=1))
            alpha = jnp.where(halves[0], alphas[2 * j], alphas[2 * j + 1])
            new_accs.append(jnp.concatenate([alpha, alpha], axis=1) * accs[j] + pv)
        return tuple(new_ms), tuple(new_accs)

    carry = (tuple(jnp.full((tile, 1), MASK_VALUE, F32) for _ in range(heads)),
             tuple(jnp.zeros((tile, 2 * LANES), F32) for _ in range(pairs)))
    carry = block(qi, carry, True)
    carry = lax.fori_loop(0, qi, lambda i, cr: block(i, cr, False), carry)
    _, accs = carry
    for j in range(pairs):
        o_ref[0, :, blk(j)] = (accs[j][:, :LANES] / accs[j][:, LANES:]).astype(o_ref.dtype)


def _mla_attention(q, k, v, batch, seq):
    groups = MLA_HEADS // MLA_HEADS_PER_STEP
    t = MLA_TILE
    qk_width = MLA_HEADS_PER_STEP * LANES
    v_width = MLA_HEADS_PER_STEP * MLA_V_DIM
    return pl.pallas_call(
        functools.partial(_mla_attn_kernel, tile=t, heads=MLA_HEADS_PER_STEP),
        grid=(batch, groups, seq // t),
        in_specs=[pl.BlockSpec((1, t, qk_width), lambda b, p, i: (b, i, p)),
                  pl.BlockSpec((1, seq, qk_width), lambda b, p, i: (b, 0, p)),
                  pl.BlockSpec((1, seq, v_width), lambda b, p, i: (b, 0, p))],
        out_specs=pl.BlockSpec((1, t, v_width), lambda b, p, i: (b, i, p)),
        out_shape=jax.ShapeDtypeStruct((batch, seq, MLA_HEADS * MLA_V_DIM), BF16),
        compiler_params=_params(),
        name="mla_attention",
    )(q, k, v)


def _rope_lane_freqs():
    inv_freq = ROPE_THETA ** (-jnp.arange(0, MLA_ROPE_DIM, 2, dtype=F32) / MLA_ROPE_DIM)
    half = MLA_ROPE_DIM // 2
    out = jnp.zeros((1, LANES), F32)
    out = out.at[0, MLA_NOPE_DIM:MLA_NOPE_DIM + half].set(inv_freq)
    out = out.at[0, MLA_NOPE_DIM + half:MLA_NOPE_DIM + 2 * half].set(inv_freq)
    return out


def _layout_w_dkv(w_dkv):
    half = MLA_ROPE_DIM // 2
    d = w_dkv.shape[0]
    t1 = w_dkv[:, MLA_KV_RANK:MLA_KV_RANK + half]
    t2 = w_dkv[:, MLA_KV_RANK + half:]
    pad_lo = jnp.zeros((d, MLA_NOPE_DIM), w_dkv.dtype)
    pad_hi = jnp.zeros((d, LANES - MLA_NOPE_DIM - MLA_ROPE_DIM), w_dkv.dtype)
    blk_a = jnp.concatenate([pad_lo, t1, t2, pad_hi], axis=1)
    blk_b = jnp.concatenate([pad_lo, -t2, t1, pad_hi], axis=1)
    return jnp.concatenate([w_dkv[:, :MLA_KV_RANK], blk_a, blk_b], axis=1).astype(BF16)


def _layout_w_ukv(w_ukv):
    r = w_ukv.shape[0]
    w = w_ukv.reshape(r, MLA_HEADS, MLA_NOPE_DIM + MLA_V_DIM)
    wk = jnp.concatenate(
        [w[:, :, :MLA_NOPE_DIM], jnp.zeros((r, MLA_HEADS, LANES - MLA_NOPE_DIM), w.dtype)],
        axis=2).reshape(r, MLA_HEADS * LANES)
    wv = w[:, :, MLA_NOPE_DIM:].reshape(r, MLA_HEADS * MLA_V_DIM)
    return wk.astype(BF16), wv.astype(BF16)


def _layout_w_uq(w_uq):
    r = w_uq.shape[0]
    half = MLA_ROPE_DIM // 2
    w = w_uq.reshape(r, MLA_HEADS, MLA_NOPE_DIM + MLA_ROPE_DIM)
    nope = w[:, :, :MLA_NOPE_DIM]
    t1 = w[:, :, MLA_NOPE_DIM:MLA_NOPE_DIM + half]
    t2 = w[:, :, MLA_NOPE_DIM + half:]
    pad_hi = jnp.zeros((r, MLA_HEADS, LANES - MLA_NOPE_DIM - MLA_ROPE_DIM), w.dtype)
    wa = jnp.concatenate([nope, t1, t2, pad_hi], axis=2)
    wb = jnp.concatenate([jnp.zeros_like(nope), -t2, t1, pad_hi], axis=2)
    shape = (r, MLA_HEADS * LANES)
    return wa.reshape(shape).astype(BF16), wb.reshape(shape).astype(BF16)


def kernel(x, positions, attn_norm, mlp_norm, sb_w_qkv, sb_w_o, kv_norm, mla_w_dkv,
           mla_kv_lat_norm, mla_w_ukv, mla_w_dq, mla_q_lat_norm, mla_w_uq, mla_w_o,
           mlp_w1, mlp_w2, final_norm):
    batch, seq, d = x.shape
    n = batch * seq
    xs = x.reshape(n, d)
    pos = positions.astype(F32).reshape(n, 1)
    inv_lane = _rope_lane_freqs()

    for layer in range(N_A_LAYERS):
        qkv = _norm_matmul(xs, attn_norm[layer].reshape(1, d), sb_w_qkv[layer].astype(BF16))
        attn = _sb_attention(qkv.reshape(batch, seq, 3 * d), batch, seq)
        xs = _proj_mlp(xs, attn.reshape(n, d), sb_w_o[layer].astype(BF16),
                       mlp_norm[layer].reshape(1, d),
                       mlp_w1[layer].astype(BF16), mlp_w2[layer].astype(BF16))

    wk, wv = _layout_w_ukv(mla_w_ukv)
    k_all, v_all = _shared_kv(xs, pos, inv_lane, kv_norm.reshape(1, d),
                              _layout_w_dkv(mla_w_dkv),
                              mla_kv_lat_norm.reshape(1, MLA_KV_RANK), wk, wv)
    k_all = k_all.reshape(batch, seq, MLA_HEADS * LANES)
    v_all = v_all.reshape(batch, seq, MLA_HEADS * MLA_V_DIM)

    for layer in range(N_A_LAYERS, DEPTH):
        j = layer - N_A_LAYERS
        wa, wb = _layout_w_uq(mla_w_uq[j])
        q = _mla_q(xs, pos, inv_lane, attn_norm[layer].reshape(1, d),
                   mla_w_dq[j].astype(BF16), mla_q_lat_norm[j].reshape(1, MLA_Q_RANK), wa, wb)
        attn = _mla_attention(q.reshape(batch, seq, MLA_HEADS * LANES), k_all, v_all, batch, seq)
        xs = _proj_mlp(xs, attn.reshape(n, MLA_HEADS * MLA_V_DIM), mla_w_o[j].astype(BF16),
                       mlp_norm[layer].reshape(1, d),
                       mlp_w1[layer].astype(BF16), mlp_w2[layer].astype(BF16),
                       final_g=final_norm.reshape(1, d) if layer == DEPTH - 1 else None)
    return xs.reshape(batch, seq, d)
```

```python
import functools
import math

import jax
import jax.numpy as jnp
from jax import lax
from jax.experimental import pallas as pl
from jax.experimental.pallas import tpu as pltpu

D_MODEL = 1024
DEPTH = 4
CHUNK = 64
N_A_LAYERS = DEPTH // 2
SB_HEADS = 16
SB_HEAD_DIM = D_MODEL // SB_HEADS
MLA_HEADS = 16
MLA_NOPE_DIM = 64
MLA_ROPE_DIM = 32
MLA_V_DIM = 64
MLA_Q_RANK = 384
MLA_KV_RANK = 256
D_FF = 4 * D_MODEL
ROPE_THETA = 10000.0
NORM_EPS = 1e-6

LANES = 128
ROW_TILE = 512
SB_TILE = 128
SB_HEADS_PER_STEP = 8
SB_DEAD_DROP = 104.0
SB_VOID = 1e30
SB_FIRST_SWEEP = 3
SB_LOOP_SWEEP = 2
MLA_TILE = 512
MLA_HEADS_PER_STEP = 4
VMEM_LIMIT = 56 * 1024 * 1024
MASK_VALUE = -1e30

F32 = jnp.float32
BF16 = jnp.bfloat16


def _rms(x, g):
    return x * lax.rsqrt(jnp.mean(x * x, axis=-1, keepdims=True) + NORM_EPS) * g


def _dot(a, b):
    return jnp.dot(a, b, preferred_element_type=F32)


def _dot_nt(a, b):
    return lax.dot_general(a, b, (((1,), (1,)), ((), ())), preferred_element_type=F32)


def _params():
    return pltpu.CompilerParams(
        dimension_semantics=None, vmem_limit_bytes=VMEM_LIMIT)


def _resident(shape):
    return pl.BlockSpec(shape, lambda *_: (0,) * len(shape),
                        pipeline_mode=pl.Buffered(1))


def _norm_matmul_kernel(x_ref, g_ref, w_ref, o_ref, *, col_chunk):
    h = _rms(x_ref[...], g_ref[...]).astype(BF16)
    n_out = o_ref.shape[1]
    for c in range(n_out // col_chunk):
        sl = slice(c * col_chunk, (c + 1) * col_chunk)
        o_ref[:, sl] = _dot(h, w_ref[:, sl]).astype(o_ref.dtype)


def _norm_matmul(x, g, w):
    n, d = x.shape
    n_out = w.shape[1]
    return pl.pallas_call(
        functools.partial(_norm_matmul_kernel, col_chunk=1024),
        grid=(n // ROW_TILE,),
        in_specs=[pl.BlockSpec((ROW_TILE, d), lambda i: (i, 0)),
                  _resident((1, d)),
                  _resident((d, n_out))],
        out_specs=pl.BlockSpec((ROW_TILE, n_out), lambda i: (i, 0)),
        out_shape=jax.ShapeDtypeStruct((n, n_out), BF16),
        compiler_params=_params(),
        name="norm_qkv",
    )(x, g, w)


def _sb_attn_kernel(q_ref, k_ref, v_ref, o_ref, *, tile, heads):
    qi = pl.program_id(2)
    pairs = heads // 2
    lane = lax.broadcasted_iota(jnp.int32, (1, LANES), 1)
    halves = (lane < SB_HEAD_DIM, lane >= SB_HEAD_DIM)
    row = lax.broadcasted_iota(jnp.int32, (tile, tile), 0)
    col = lax.broadcasted_iota(jnp.int32, (tile, tile), 1)
    from_key = (row >= col).astype(BF16)
    from_key2 = jnp.concatenate([from_key, from_key], axis=0)
    causal = col < row
    causal2 = jnp.concatenate([causal, causal], axis=0)

    def masked(x, keep):
        return jnp.where(keep, x, jnp.zeros_like(x))

    def blk(j):
        return slice(j * LANES, (j + 1) * LANES)

    q_pairs = []
    for j in range(pairs):
        q = q_ref[0, :, blk(j)] * (SB_HEAD_DIM ** -0.5)
        q_pairs.append(jnp.concatenate([masked(q, halves[0]), masked(q, halves[1])], axis=0))

    def sweep(kb0, count, accs, cs, diag_first):
        kbs = [kb0 - i for i in range(count)]
        starts = [pl.multiple_of(jnp.maximum(kb, 0) * tile, tile) for kb in kbs]
        voids = [jnp.where(kb >= 0, 0.0, SB_VOID) for kb in kbs]
        zs = [[_dot_nt(q_pairs[j], k_ref[0, pl.ds(st, tile), blk(j)])
               for j in range(pairs)] for st in starts]
        splits = []
        for i in range(count):
            for z in zs[i]:
                drop = jnp.maximum(z, 0.0) + jnp.log(1.0 + jnp.exp(-jnp.abs(z)))
                if diag_first and i == 0:
                    drop = jnp.where(causal2, drop, 0.0)
                hi = drop.astype(BF16)
                lo = (drop - hi.astype(F32)).astype(BF16)
                splits.append(jnp.concatenate([hi, lo], axis=1))
        totals = _dot(jnp.concatenate(splits, axis=0), from_key2)
        accs, cs = list(accs), list(cs)
        for i in range(count):
            for j in range(pairs):
                base = (i * pairs + j) * 2 * tile
                total = totals[base:base + 2 * tile]
                cs[j] = cs[j] + voids[i]
                w = jnp.exp(zs[i][j] - total - cs[j])
                if diag_first and i == 0:
                    w = jnp.where(causal2, w, 0.0)
                w = w.astype(BF16)
                v = v_ref[0, pl.ds(starts[i], tile), blk(j)]
                accs[j] = accs[j] + _dot(
                    jnp.concatenate([w[:tile], w[tile:]], axis=1),
                    jnp.concatenate([masked(v, halves[0]), masked(v, halves[1])], axis=0))
                cs[j] = cs[j] + total[:, 0:1]
        return tuple(accs), tuple(cs)

    def alive_flag(cs):
        low = functools.reduce(jnp.minimum, cs)
        return (jnp.min(low) < SB_DEAD_DROP).astype(jnp.int32)

    accs = tuple(jnp.zeros((tile, LANES), F32) for _ in range(pairs))
    cs = tuple(jnp.zeros((2 * tile, 1), F32) for _ in range(pairs))
    accs, cs = sweep(qi, SB_FIRST_SWEEP, accs, cs, True)

    def cond(state):
        kb, _, _, alive = state
        return jnp.logical_and(kb >= 0, alive > 0)

    def body(state):
        kb, accs, cs, _ = state
        accs, cs = sweep(kb, SB_LOOP_SWEEP, accs, cs, False)
        return kb - SB_LOOP_SWEEP, accs, cs, alive_flag(cs)

    _, accs, _, _ = lax.while_loop(
        cond, body, (qi - SB_FIRST_SWEEP, accs, cs, alive_flag(cs)))
    for j, acc in enumerate(accs):
        o_ref[0, :, blk(j)] = acc.astype(o_ref.dtype)


def _sb_attention(qkv, batch, seq):
    width = SB_HEADS_PER_STEP * SB_HEAD_DIM
    groups = D_MODEL // width
    t = SB_TILE
    return pl.pallas_call(
        functools.partial(_sb_attn_kernel, tile=t, heads=SB_HEADS_PER_STEP),
        grid=(batch, groups, seq // t),
        in_specs=[pl.BlockSpec((1, t, width), lambda b, p, i: (b, i, p)),
                  pl.BlockSpec((1, seq, width), lambda b, p, i: (b, 0, groups + p)),
                  pl.BlockSpec((1, seq, width), lambda b, p, i: (b, 0, 2 * groups + p))],
        out_specs=pl.BlockSpec((1, t, width), lambda b, p, i: (b, i, p)),
        out_shape=jax.ShapeDtypeStruct((batch, seq, D_MODEL), BF16),
        compiler_params=_params(),
        name="sb_attention",
    )(qkv, qkv, qkv)


def _proj_mlp_kernel(x_ref, a_ref, wo_ref, g_ref, w1_ref, w2_ref, *rest, ff_chunk, final):
    if final:
        gf_ref, o_ref = rest
    else:
        (o_ref,) = rest
    x1 = x_ref[...] + _dot(a_ref[...], wo_ref[...])
    h = _rms(x1, g_ref[...]).astype(BF16)
    acc = x1
    for c in range(D_FF // ff_chunk):
        sl = slice(c * ff_chunk, (c + 1) * ff_chunk)
        u = jnp.maximum(_dot(h, w1_ref[:, sl]), 0.0)
        acc = acc + _dot((u * u).astype(BF16), w2_ref[sl, :])
    if final:
        acc = _rms(acc, gf_ref[...])
    o_ref[...] = acc


def _proj_mlp(x, a, wo, g, w1, w2, final_g=None):
    n, d = x.shape
    final = final_g is not None
    row = lambda i: (i, 0)
    in_specs = [pl.BlockSpec((ROW_TILE, d), row),
                pl.BlockSpec((ROW_TILE, a.shape[1]), row),
                _resident(wo.shape), _resident((1, d)),
                _resident(w1.shape), _resident(w2.shape)]
    args = [x, a, wo, g, w1, w2]
    if final:
        in_specs.append(_resident((1, d)))
        args.append(final_g)
    return pl.pallas_call(
        functools.partial(_proj_mlp_kernel, ff_chunk=1024, final=final),
        grid=(n // ROW_TILE,),
        in_specs=in_specs,
        out_specs=pl.BlockSpec((ROW_TILE, d), row),
        out_shape=jax.ShapeDtypeStruct((n, d), F32),
        compiler_params=_params(),
        name="proj_mlp_final" if final else "proj_mlp",
    )(*args)


def _rope_tables(pos, inv_lane):
    lane = lax.broadcasted_iota(jnp.int32, (1, LANES), 1)
    rope = jnp.logical_and(lane >= MLA_NOPE_DIM, lane < MLA_NOPE_DIM + MLA_ROPE_DIM)
    ang = pos * inv_lane
    cos_t = jnp.where(lane < MLA_NOPE_DIM, 1.0, jnp.where(rope, jnp.cos(ang), 0.0))
    sin_t = jnp.where(rope, jnp.sin(ang), 0.0)
    return cos_t, sin_t


def _shared_kv_kernel(x_ref, pos_ref, inv_ref, g_ref, wd_ref, gl_ref, wk_ref, wv_ref,
                      k_ref, v_ref):
    h = _rms(x_ref[...], g_ref[...]).astype(BF16)
    down = _dot(h, wd_ref[...])
    c = _rms(down[:, :MLA_KV_RANK], gl_ref[...]).astype(BF16)
    cos_t, sin_t = _rope_tables(pos_ref[...], inv_ref[...])
    k_rope = (down[:, MLA_KV_RANK:MLA_KV_RANK + LANES] * cos_t
              + down[:, MLA_KV_RANK + LANES:] * sin_t)
    k_nope = _dot(c, wk_ref[...])
    for hd in range(MLA_HEADS):
        sl = slice(hd * LANES, (hd + 1) * LANES)
        k_ref[:, sl] = (k_nope[:, sl] + k_rope).astype(k_ref.dtype)
    v_ref[...] = _dot(c, wv_ref[...]).astype(v_ref.dtype)


def _shared_kv(x, pos, inv_lane, g, wd, gl, wk, wv):
    n, d = x.shape
    row = lambda i: (i, 0)
    return pl.pallas_call(
        _shared_kv_kernel,
        grid=(n // ROW_TILE,),
        in_specs=[pl.BlockSpec((ROW_TILE, d), row),
                  pl.BlockSpec((ROW_TILE, 1), row),
                  _resident((1, LANES)), _resident((1, d)), _resident(wd.shape),
                  _resident((1, MLA_KV_RANK)), _resident(wk.shape), _resident(wv.shape)],
        out_specs=[pl.BlockSpec((ROW_TILE, MLA_HEADS * LANES), row),
                   pl.BlockSpec((ROW_TILE, MLA_HEADS * MLA_V_DIM), row)],
        out_shape=[jax.ShapeDtypeStruct((n, MLA_HEADS * LANES), BF16),
                   jax.ShapeDtypeStruct((n, MLA_HEADS * MLA_V_DIM), BF16)],
        compiler_params=_params(),
        name="mla_shared_kv",
    )(x, pos, inv_lane, g, wd, gl, wk, wv)


def _mla_q_kernel(x_ref, pos_ref, inv_ref, g_ref, wdq_ref, gq_ref, wa_ref, wb_ref, q_ref,
                  *, heads_per_chunk):
    h = _rms(x_ref[...], g_ref[...]).astype(BF16)
    cq = _rms(_dot(h, wdq_ref[...]), gq_ref[...]).astype(BF16)
    cos_t, sin_t = _rope_tables(pos_ref[...], inv_ref[...])
    scale = (MLA_NOPE_DIM + MLA_ROPE_DIM) ** -0.5 * math.log2(math.e)
    cos_t = cos_t * scale
    sin_t = sin_t * scale
    width = heads_per_chunk * LANES
    for c in range(MLA_HEADS // heads_per_chunk):
        sl = slice(c * width, (c + 1) * width)
        qa = _dot(cq, wa_ref[:, sl])
        qb = _dot(cq, wb_ref[:, sl])
        for j in range(heads_per_chunk):
            hs = slice(j * LANES, (j + 1) * LANES)
            q_ref[:, c * width + j * LANES:c * width + (j + 1) * LANES] = (
                qa[:, hs] * cos_t + qb[:, hs] * sin_t).astype(q_ref.dtype)


def _mla_q(x, pos, inv_lane, g, wdq, gq, wa, wb):
    n, d = x.shape
    row = lambda i: (i, 0)
    return pl.pallas_call(
        functools.partial(_mla_q_kernel, heads_per_chunk=4),
        grid=(n // ROW_TILE,),
        in_specs=[pl.BlockSpec((ROW_TILE, d), row),
                  pl.BlockSpec((ROW_TILE, 1), row),
                  _resident((1, LANES)), _resident((1, d)), _resident(wdq.shape),
                  _resident((1, MLA_Q_RANK)), _resident(wa.shape), _resident(wb.shape)],
        out_specs=pl.BlockSpec((ROW_TILE, MLA_HEADS * LANES), row),
        out_shape=jax.ShapeDtypeStruct((n, MLA_HEADS * LANES), BF16),
        compiler_params=_params(),
        name="mla_q",
    )(x, pos, inv_lane, g, wdq, gq, wa, wb)


def _mla_attn_kernel(q_ref, k_ref, v_ref, o_ref, *, tile, heads):
    qi = pl.program_id(2)
    pairs = heads // 2
    lane = lax.broadcasted_iota(jnp.int32, (1, LANES), 1)
    halves = (lane < MLA_V_DIM, lane >= MLA_V_DIM)
    row = lax.broadcasted_iota(jnp.int32, (tile, tile), 0)
    col = lax.broadcasted_iota(jnp.int32, (tile, tile), 1)
    allowed = (col // CHUNK) <= (row // CHUNK)

    def masked(x, keep):
        return jnp.where(keep, x, jnp.zeros_like(x))

    def blk(j):
        return slice(j * LANES, (j + 1) * LANES)

    key_row = lax.broadcasted_iota(jnp.int32, (2 * tile, 1), 0)
    ones_cols = jnp.where((key_row < tile) == halves[0], 1.0, 0.0).astype(BF16)

    def sweep(kbs, diag_last, carry):
        ms, accs = carry
        starts = [pl.multiple_of(kb * tile, tile) for kb in kbs]
        ss = [[_dot_nt(q_ref[0, :, blk(h)], k_ref[0, pl.ds(st, tile), blk(h)])
               for h in range(heads)] for st in starts]
        if diag_last:
            ss[-1] = [jnp.where(allowed, s, MASK_VALUE) for s in ss[-1]]
        for n, st in enumerate(starts):
            ps, alphas, new_ms = [], [], []
            for h in range(heads):
                m_new = jnp.maximum(ms[h], jnp.max(ss[n][h], axis=-1, keepdims=True))
                alphas.append(jnp.exp2(ms[h] - m_new))
                new_ms.append(m_new)
                ps.append(jnp.exp2(ss[n][h] - m_new).astype(BF16))
            new_accs = []
            for j in range(pairs):
                v = v_ref[0, pl.ds(st, tile), blk(j)]
                vals = jnp.concatenate([masked(v, halves[0]), masked(v, halves[1])], axis=0)
                pv = _dot(jnp.concatenate([ps[2 * j], ps[2 * j + 1]], axis=1),
                          jnp.concatenate([vals, ones_cols], axis=1))
                alpha = jnp.where(halves[0], alphas[2 * j], alphas[2 * j + 1])
                new_accs.append(jnp.concatenate([alpha, alpha], axis=1) * accs[j] + pv)
            ms, accs = tuple(new_ms), tuple(new_accs)
        return ms, accs

    carry = (tuple(jnp.full((tile, 1), MASK_VALUE, F32) for _ in range(heads)),
             tuple(jnp.zeros((tile, 2 * LANES), F32) for _ in range(pairs)))
    carry = lax.fori_loop(0, qi // 2, lambda i, cr: sweep([2 * i, 2 * i + 1], False, cr), carry)
    _, accs = lax.cond(qi % 2 == 1,
                       lambda cr: sweep([qi - 1, qi], True, cr),
                       lambda cr: sweep([qi], True, cr), carry)
    for j in range(pairs):
        o_ref[0, :, blk(j)] = (accs[j][:, :LANES] / accs[j][:, LANES:]).astype(o_ref.dtype)


def _mla_attention(q, k, v, batch, seq):
    groups = MLA_HEADS // MLA_HEADS_PER_STEP
    t = MLA_TILE
    qk_width = MLA_HEADS_PER_STEP * LANES
    v_width = MLA_HEADS_PER_STEP * MLA_V_DIM
    return pl.pallas_call(
        functools.partial(_mla_attn_kernel, tile=t, heads=MLA_HEADS_PER_STEP),
        grid=(batch, groups, seq // t),
        in_specs=[pl.BlockSpec((1, t, qk_width), lambda b, p, i: (b, i, p)),
                  pl.BlockSpec((1, seq, qk_width), lambda b, p, i: (b, 0, p)),
                  pl.BlockSpec((1, seq, v_width), lambda b, p, i: (b, 0, p))],
        out_specs=pl.BlockSpec((1, t, v_width), lambda b, p, i: (b, i, p)),
        out_shape=jax.ShapeDtypeStruct((batch, seq, MLA_HEADS * MLA_V_DIM), BF16),
        compiler_params=_params(),
        name="mla_attention",
    )(q, k, v)


def _rope_lane_freqs():
    inv_freq = ROPE_THETA ** (-jnp.arange(0, MLA_ROPE_DIM, 2, dtype=F32) / MLA_ROPE_DIM)
    half = MLA_ROPE_DIM // 2
    out = jnp.zeros((1, LANES), F32)
    out = out.at[0, MLA_NOPE_DIM:MLA_NOPE_DIM + half].set(inv_freq)
    out = out.at[0, MLA_NOPE_DIM + half:MLA_NOPE_DIM + 2 * half].set(inv_freq)
    return out


def _layout_w_dkv(w_dkv):
    half = MLA_ROPE_DIM // 2
    d = w_dkv.shape[0]
    t1 = w_dkv[:, MLA_KV_RANK:MLA_KV_RANK + half]
    t2 = w_dkv[:, MLA_KV_RANK + half:]
    pad_lo = jnp.zeros((d, MLA_NOPE_DIM), w_dkv.dtype)
    pad_hi = jnp.zeros((d, LANES - MLA_NOPE_DIM - MLA_ROPE_DIM), w_dkv.dtype)
    blk_a = jnp.concatenate([pad_lo, t1, t2, pad_hi], axis=1)
    blk_b = jnp.concatenate([pad_lo, -t2, t1, pad_hi], axis=1)
    return jnp.concatenate([w_dkv[:, :MLA_KV_RANK], blk_a, blk_b], axis=1).astype(BF16)


def _layout_w_ukv(w_ukv):
    r = w_ukv.shape[0]
    w = w_ukv.reshape(r, MLA_HEADS, MLA_NOPE_DIM + MLA_V_DIM)
    wk = jnp.concatenate(
        [w[:, :, :MLA_NOPE_DIM], jnp.zeros((r, MLA_HEADS, LANES - MLA_NOPE_DIM), w.dtype)],
        axis=2).reshape(r, MLA_HEADS * LANES)
    wv = w[:, :, MLA_NOPE_DIM:].reshape(r, MLA_HEADS * MLA_V_DIM)
    return wk.astype(BF16), wv.astype(BF16)


def _layout_w_uq(w_uq):
    r = w_uq.shape[0]
    half = MLA_ROPE_DIM // 2
    w = w_uq.reshape(r, MLA_HEADS, MLA_NOPE_DIM + MLA_ROPE_DIM)
    nope = w[:, :, :MLA_NOPE_DIM]
    t1 = w[:, :, MLA_NOPE_DIM:MLA_NOPE_DIM + half]
    t2 = w[:, :, MLA_NOPE_DIM + half:]
    pad_hi = jnp.zeros((r, MLA_HEADS, LANES - MLA_NOPE_DIM - MLA_ROPE_DIM), w.dtype)
    wa = jnp.concatenate([nope, t1, t2, pad_hi], axis=2)
    wb = jnp.concatenate([jnp.zeros_like(nope), -t2, t1, pad_hi], axis=2)
    shape = (r, MLA_HEADS * LANES)
    return wa.reshape(shape).astype(BF16), wb.reshape(shape).astype(BF16)


def kernel(x, positions, attn_norm, mlp_norm, sb_w_qkv, sb_w_o, kv_norm, mla_w_dkv,
           mla_kv_lat_norm, mla_w_ukv, mla_w_dq, mla_q_lat_norm, mla_w_uq, mla_w_o,
           mlp_w1, mlp_w2, final_norm):
    batch, seq, d = x.shape
    n = batch * seq
    xs = x.reshape(n, d)
    pos = positions.astype(F32).reshape(n, 1)
    inv_lane = _rope_lane_freqs()

    for layer in range(N_A_LAYERS):
        qkv = _norm_matmul(xs, attn_norm[layer].reshape(1, d), sb_w_qkv[layer].astype(BF16))
        attn = _sb_attention(qkv.reshape(batch, seq, 3 * d), batch, seq)
        xs = _proj_mlp(xs, attn.reshape(n, d), sb_w_o[layer].astype(BF16),
                       mlp_norm[layer].reshape(1, d),
                       mlp_w1[layer].astype(BF16), mlp_w2[layer].astype(BF16))

    wk, wv = _layout_w_ukv(mla_w_ukv)
    k_all, v_all = _shared_kv(xs, pos, inv_lane, kv_norm.reshape(1, d),
                              _layout_w_dkv(mla_w_dkv),
                              mla_kv_lat_norm.reshape(1, MLA_KV_RANK), wk, wv)
    k_all = k_all.reshape(batch, seq, MLA_HEADS * LANES)
    v_all = v_all.reshape(batch, seq, MLA_HEADS * MLA_V_DIM)

    for layer in range(N_A_LAYERS, DEPTH):
        j = layer - N_A_LAYERS
        wa, wb = _layout_w_uq(mla_w_uq[j])
        q = _mla_q(xs, pos, inv_lane, attn_norm[layer].reshape(1, d),
                   mla_w_dq[j].astype(BF16), mla_q_lat_norm[j].reshape(1, MLA_Q_RANK), wa, wb)
        attn = _mla_attention(q.reshape(batch, seq, MLA_HEADS * LANES), k_all, v_all, batch, seq)
        xs = _proj_mlp(xs, attn.reshape(n, MLA_HEADS * MLA_V_DIM), mla_w_o[j].astype(BF16),
                       mlp_norm[layer].reshape(1, d),
                       mlp_w1[layer].astype(BF16), mlp_w2[layer].astype(BF16),
                       final_g=final_norm.reshape(1, d) if layer == DEPTH - 1 else None)
    return xs.reshape(batch, seq, d)
```

```python
import functools
import math

import jax
import jax.numpy as jnp
from jax import lax
from jax.experimental import pallas as pl
from jax.experimental.pallas import tpu as pltpu

D_MODEL = 1024
DEPTH = 4
CHUNK = 64
N_A_LAYERS = DEPTH // 2
SB_HEADS = 16
SB_HEAD_DIM = D_MODEL // SB_HEADS
MLA_HEADS = 16
MLA_NOPE_DIM = 64
MLA_ROPE_DIM = 32
MLA_V_DIM = 64
MLA_Q_RANK = 384
MLA_KV_RANK = 256
D_FF = 4 * D_MODEL
ROPE_THETA = 10000.0
NORM_EPS = 1e-6

LANES = 128
ROW_TILE = 512
SB_TILE = 128
SB_HEADS_PER_STEP = 8
SB_DEAD_DROP = 104.0
SB_VOID = 1e30
SB_FIRST_SWEEP = 3
SB_LOOP_SWEEP = 2
MLA_TILE = 512
MLA_HEADS_PER_STEP = 4
VMEM_LIMIT = 56 * 1024 * 1024
MASK_VALUE = -1e30

F32 = jnp.float32
BF16 = jnp.bfloat16


def _rms(x, g):
    return x * lax.rsqrt(jnp.mean(x * x, axis=-1, keepdims=True) + NORM_EPS) * g


def _dot(a, b):
    return jnp.dot(a, b, preferred_element_type=F32)


def _dot_nt(a, b):
    return lax.dot_general(a, b, (((1,), (1,)), ((), ())), preferred_element_type=F32)


def _params():
    return pltpu.CompilerParams(
        dimension_semantics=None, vmem_limit_bytes=VMEM_LIMIT)


def _resident(shape):
    return pl.BlockSpec(shape, lambda *_: (0,) * len(shape),
                        pipeline_mode=pl.Buffered(1))


def _norm_matmul_kernel(x_ref, g_ref, w_ref, o_ref, *, col_chunk):
    h = _rms(x_ref[...], g_ref[...]).astype(BF16)
    n_out = o_ref.shape[1]
    for c in range(n_out // col_chunk):
        sl = slice(c * col_chunk, (c + 1) * col_chunk)
        o_ref[:, sl] = _dot(h, w_ref[:, sl]).astype(o_ref.dtype)


def _norm_matmul(x, g, w):
    n, d = x.shape
    n_out = w.shape[1]
    return pl.pallas_call(
        functools.partial(_norm_matmul_kernel, col_chunk=1024),
        grid=(n // ROW_TILE,),
        in_specs=[pl.BlockSpec((ROW_TILE, d), lambda i: (i, 0)),
                  _resident((1, d)),
                  _resident((d, n_out))],
        out_specs=pl.BlockSpec((ROW_TILE, n_out), lambda i: (i, 0)),
        out_shape=jax.ShapeDtypeStruct((n, n_out), BF16),
        compiler_params=_params(),
        name="norm_qkv",
    )(x, g, w)


def _sb_attn_kernel(q_ref, k_ref, v_ref, o_ref, *, tile, heads):
    qi = pl.program_id(2)
    pairs = heads // 2
    lane = lax.broadcasted_iota(jnp.int32, (1, LANES), 1)
    halves = (lane < SB_HEAD_DIM, lane >= SB_HEAD_DIM)
    row = lax.broadcasted_iota(jnp.int32, (tile, tile), 0)
    col = lax.broadcasted_iota(jnp.int32, (tile, tile), 1)
    from_key = (row >= col).astype(BF16)
    from_key2 = jnp.concatenate([from_key, from_key], axis=0)
    causal = col < row
    causal2 = jnp.concatenate([causal, causal], axis=0)

    def masked(x, keep):
        return jnp.where(keep, x, jnp.zeros_like(x))

    def blk(j):
        return slice(j * LANES, (j + 1) * LANES)

    q_pairs = []
    for j in range(pairs):
        q = q_ref[0, :, blk(j)] * (SB_HEAD_DIM ** -0.5)
        q_pairs.append(jnp.concatenate([masked(q, halves[0]), masked(q, halves[1])], axis=0))

    def sweep(kb0, count, accs, cs, diag_first):
        kbs = [kb0 - i for i in range(count)]
        starts = [pl.multiple_of(jnp.maximum(kb, 0) * tile, tile) for kb in kbs]
        voids = [jnp.where(kb >= 0, 0.0, SB_VOID) for kb in kbs]
        zs = [[_dot_nt(q_pairs[j], k_ref[0, pl.ds(st, tile), blk(j)])
               for j in range(pairs)] for st in starts]
        splits = []
        for i in range(count):
            for z in zs[i]:
                drop = jnp.maximum(z, 0.0) + jnp.log(1.0 + jnp.exp(-jnp.abs(z)))
                if diag_first and i == 0:
                    drop = jnp.where(causal2, drop, 0.0)
                hi = drop.astype(BF16)
                lo = (drop - hi.astype(F32)).astype(BF16)
                splits.append(jnp.concatenate([hi, lo], axis=1))
        totals = _dot(jnp.concatenate(splits, axis=0), from_key2)
        accs, cs = list(accs), list(cs)
        for i in range(count):
            for j in range(pairs):
                base = (i * pairs + j) * 2 * tile
                total = totals[base:base + 2 * tile]
                cs[j] = cs[j] + voids[i]
                w = jnp.exp(zs[i][j] - total - cs[j])
                if diag_first and i == 0:
                    w = jnp.where(causal2, w, 0.0)
                w = w.astype(BF16)
                v = v_ref[0, pl.ds(starts[i], tile), blk(j)]
                accs[j] = accs[j] + _dot(
                    jnp.concatenate([w[:tile], w[tile:]], axis=1),
                    jnp.concatenate([masked(v, halves[0]), masked(v, halves[1])], axis=0))
                cs[j] = cs[j] + total[:, 0:1]
        return tuple(accs), tuple(cs)

    def alive_flag(cs):
        low = functools.reduce(jnp.minimum, cs)
        return (jnp.min(low) < SB_DEAD_DROP).astype(jnp.int32)

    accs = tuple(jnp.zeros((tile, LANES), F32) for _ in range(pairs))
    cs = tuple(jnp.zeros((2 * tile, 1), F32) for _ in range(pairs))
    accs, cs = sweep(qi, SB_FIRST_SWEEP, accs, cs, True)

    def cond(state):
        kb, _, _, alive = state
        return jnp.logical_and(kb >= 0, alive > 0)

    def body(state):
        kb, accs, cs, _ = state
        accs, cs = sweep(kb, SB_LOOP_SWEEP, accs, cs, False)
        return kb - SB_LOOP_SWEEP, accs, cs, alive_flag(cs)

    _, accs, _, _ = lax.while_loop(
        cond, body, (qi - SB_FIRST_SWEEP, accs, cs, alive_flag(cs)))
    for j, acc in enumerate(accs):
        o_ref[0, :, blk(j)] = acc.astype(o_ref.dtype)


def _sb_attention(qkv, batch, seq):
    width = SB_HEADS_PER_STEP * SB_HEAD_DIM
    groups = D_MODEL // width
    t = SB_TILE
    return pl.pallas_call(
        functools.partial(_sb_attn_kernel, tile=t, heads=SB_HEADS_PER_STEP),
        grid=(batch, groups, seq // t),
        in_specs=[pl.BlockSpec((1, t, width), lambda b, p, i: (b, i, p)),
                  pl.BlockSpec((1, seq, width), lambda b, p, i: (b, 0, groups + p)),
                  pl.BlockSpec((1, seq, width), lambda b, p, i: (b, 0, 2 * groups + p))],
        out_specs=pl.BlockSpec((1, t, width), lambda b, p, i: (b, i, p)),
        out_shape=jax.ShapeDtypeStruct((batch, seq, D_MODEL), BF16),
        compiler_params=_params(),
        name="sb_attention",
    )(qkv, qkv, qkv)


def _proj_mlp_kernel(x_ref, a_ref, wo_ref, g_ref, w1_ref, w2_ref, *rest, ff_chunk, final):
    if final:
        gf_ref, o_ref = rest
    else:
        (o_ref,) = rest
    x1 = x_ref[...] + _dot(a_ref[...], wo_ref[...])
    h = _rms(x1, g_ref[...]).astype(BF16)
    acc = x1
    for c in range(D_FF // ff_chunk):
        sl = slice(c * ff_chunk, (c + 1) * ff_chunk)
        u = jnp.maximum(_dot(h, w1_ref[:, sl]), 0.0)
        acc = acc + _dot((u * u).astype(BF16), w2_ref[sl, :])
    if final:
        acc = _rms(acc, gf_ref[...])
    o_ref[...] = acc


def _proj_mlp(x, a, wo, g, w1, w2, final_g=None):
    n, d = x.shape
    final = final_g is not None
    row = lambda i: (i, 0)
    in_specs = [pl.BlockSpec((ROW_TILE, d), row),
                pl.BlockSpec((ROW_TILE, a.shape[1]), row),
                _resident(wo.shape), _resident((1, d)),
                _resident(w1.shape), _resident(w2.shape)]
    args = [x, a, wo, g, w1, w2]
    if final:
        in_specs.append(_resident((1, d)))
        args.append(final_g)
    return pl.pallas_call(
        functools.partial(_proj_mlp_kernel, ff_chunk=1024, final=final),
        grid=(n // ROW_TILE,),
        in_specs=in_specs,
        out_specs=pl.BlockSpec((ROW_TILE, d), row),
        out_shape=jax.ShapeDtypeStruct((n, d), F32),
        compiler_params=_params(),
        name="proj_mlp_final" if final else "proj_mlp",
    )(*args)


def _rope_tables(pos, inv_lane):
    lane = lax.broadcasted_iota(jnp.int32, (1, LANES), 1)
    rope = jnp.logical_and(lane >= MLA_NOPE_DIM, lane < MLA_NOPE_DIM + MLA_ROPE_DIM)
    ang = pos * inv_lane
    cos_t = jnp.where(lane < MLA_NOPE_DIM, 1.0, jnp.where(rope, jnp.cos(ang), 0.0))
    sin_t = jnp.where(rope, jnp.sin(ang), 0.0)
    return cos_t, sin_t


def _shared_kv_kernel(x_ref, pos_ref, inv_ref, g_ref, wd_ref, gl_ref, wk_ref, wv_ref,
                      k_ref, v_ref):
    h = _rms(x_ref[...], g_ref[...]).astype(BF16)
    down = _dot(h, wd_ref[...])
    c = _rms(down[:, :MLA_KV_RANK], gl_ref[...]).astype(BF16)
    cos_t, sin_t = _rope_tables(pos_ref[...], inv_ref[...])
    k_rope = (down[:, MLA_KV_RANK:MLA_KV_RANK + LANES] * cos_t
              + down[:, MLA_KV_RANK + LANES:] * sin_t)
    k_nope = _dot(c, wk_ref[...])
    for hd in range(MLA_HEADS):
        sl = slice(hd * LANES, (hd + 1) * LANES)
        k_ref[:, sl] = (k_nope[:, sl] + k_rope).astype(k_ref.dtype)
    v_ref[...] = _dot(c, wv_ref[...]).astype(v_ref.dtype)


def _shared_kv(x, pos, inv_lane, g, wd, gl, wk, wv):
    n, d = x.shape
    row = lambda i: (i, 0)
    return pl.pallas_call(
        _shared_kv_kernel,
        grid=(n // ROW_TILE,),
        in_specs=[pl.BlockSpec((ROW_TILE, d), row),
                  pl.BlockSpec((ROW_TILE, 1), row),
                  _resident((1, LANES)), _resident((1, d)), _resident(wd.shape),
                  _resident((1, MLA_KV_RANK)), _resident(wk.shape), _resident(wv.shape)],
        out_specs=[pl.BlockSpec((ROW_TILE, MLA_HEADS * LANES), row),
                   pl.BlockSpec((ROW_TILE, MLA_HEADS * MLA_V_DIM), row)],
        out_shape=[jax.ShapeDtypeStruct((n, MLA_HEADS * LANES), BF16),
                   jax.ShapeDtypeStruct((n, MLA_HEADS * MLA_V_DIM), BF16)],
        compiler_params=_params(),
        name="mla_shared_kv",
    )(x, pos, inv_lane, g, wd, gl, wk, wv)


def _mla_q_kernel(x_ref, pos_ref, inv_ref, g_ref, wdq_ref, gq_ref, wa_ref, wb_ref, q_ref,
                  *, heads_per_chunk):
    h = _rms(x_ref[...], g_ref[...]).astype(BF16)
    cq = _rms(_dot(h, wdq_ref[...]), gq_ref[...]).astype(BF16)
    cos_t, sin_t = _rope_tables(pos_ref[...], inv_ref[...])
    scale = (MLA_NOPE_DIM + MLA_ROPE_DIM) ** -0.5 * math.log2(math.e)
    cos_t = cos_t * scale
    sin_t = sin_t * scale
    width = heads_per_chunk * LANES
    for c in range(MLA_HEADS // heads_per_chunk):
        sl = slice(c * width, (c + 1) * width)
        qa = _dot(cq, wa_ref[:, sl])
        qb = _dot(cq, wb_ref[:, sl])
        for j in range(heads_per_chunk):
            hs = slice(j * LANES, (j + 1) * LANES)
            q_ref[:, c * width + j * LANES:c * width + (j + 1) * LANES] = (
                qa[:, hs] * cos_t + qb[:, hs] * sin_t).astype(q_ref.dtype)


def _mla_q(x, pos, inv_lane, g, wdq, gq, wa, wb):
    n, d = x.shape
    row = lambda i: (i, 0)
    return pl.pallas_call(
        functools.partial(_mla_q_kernel, heads_per_chunk=4),
        grid=(n // ROW_TILE,),
        in_specs=[pl.BlockSpec((ROW_TILE, d), row),
                  pl.BlockSpec((ROW_TILE, 1), row),
                  _resident((1, LANES)), _resident((1, d)), _resident(wdq.shape),
                  _resident((1, MLA_Q_RANK)), _resident(wa.shape), _resident(wb.shape)],
        out_specs=pl.BlockSpec((ROW_TILE, MLA_HEADS * LANES), row),
        out_shape=jax.ShapeDtypeStruct((n, MLA_HEADS * LANES), BF16),
        compiler_params=_params(),
        name="mla_q",
    )(x, pos, inv_lane, g, wdq, gq, wa, wb)


def _mla_attn_kernel(q_ref, k_ref, v_ref, o_ref, *, tile, heads):
    qi = pl.program_id(2)
    pairs = heads // 2
    lane = lax.broadcasted_iota(jnp.int32, (1, LANES), 1)
    halves = (lane < MLA_V_DIM, lane >= MLA_V_DIM)
    row = lax.broadcasted_iota(jnp.int32, (tile, tile), 0)
    col = lax.broadcasted_iota(jnp.int32, (tile, tile), 1)
    allowed = (col // CHUNK) <= (row // CHUNK)

    def masked(x, keep):
        return jnp.where(keep, x, jnp.zeros_like(x))

    def blk(j):
        return slice(j * LANES, (j + 1) * LANES)

    key_row = lax.broadcasted_iota(jnp.int32, (2 * tile, 1), 0)
    ones_cols = jnp.where((key_row < tile) == halves[0], 1.0, 0.0).astype(BF16)

    def sweep(kbs, diag_last, carry):
        ms, accs = carry
        starts = [pl.multiple_of(kb * tile, tile) for kb in kbs]
        ss = [[_dot_nt(q_ref[0, :, blk(h)], k_ref[0, pl.ds(st, tile), blk(h)])
               for h in range(heads)] for st in starts]
        if diag_last:
            ss[-1] = [jnp.where(allowed, s, MASK_VALUE) for s in ss[-1]]
        count = len(kbs)
        ps, alphas, new_ms = [], [], []
        for h in range(heads):
            top = functools.reduce(jnp.maximum, [ss[n][h] for n in range(count)])
            m_new = jnp.maximum(ms[h], jnp.max(top, axis=-1, keepdims=True))
            alphas.append(jnp.exp2(ms[h] - m_new))
            new_ms.append(m_new)
            ps.append([jnp.exp2(ss[n][h] - m_new).astype(BF16) for n in range(count)])
        new_accs = []
        for j in range(pairs):
            lhs, rhs = [], []
            for n, st in enumerate(starts):
                v = v_ref[0, pl.ds(st, tile), blk(j)]
                vals = jnp.concatenate([masked(v, halves[0]), masked(v, halves[1])], axis=0)
                lhs += [ps[2 * j][n], ps[2 * j + 1][n]]
                rhs.append(jnp.concatenate([vals, ones_cols], axis=1))
            pv = _dot(jnp.concatenate(lhs, axis=1),
                      jnp.concatenate(rhs, axis=0))
            alpha = jnp.where(halves[0], alphas[2 * j], alphas[2 * j + 1])
            new_accs.append(jnp.concatenate([alpha, alpha], axis=1) * accs[j] + pv)
        return tuple(new_ms), tuple(new_accs)

    carry = (tuple(jnp.full((tile, 1), MASK_VALUE, F32) for _ in range(heads)),
             tuple(jnp.zeros((tile, 2 * LANES), F32) for _ in range(pairs)))
    carry = lax.fori_loop(0, qi // 2, lambda i, cr: sweep([2 * i, 2 * i + 1], False, cr), carry)
    _, accs = lax.cond(qi % 2 == 1,
                       lambda cr: sweep([qi - 1, qi], True, cr),
                       lambda cr: sweep([qi], True, cr), carry)
    for j in range(pairs):
        o_ref[0, :, blk(j)] = (accs[j][:, :LANES] / accs[j][:, LANES:]).astype(o_ref.dtype)


def _mla_attention(q, k, v, batch, seq):
    groups = MLA_HEADS // MLA_HEADS_PER_STEP
    t = MLA_TILE
    qk_width = MLA_HEADS_PER_STEP * LANES
    v_width = MLA_HEADS_PER_STEP * MLA_V_DIM
    return pl.pallas_call(
        functools.partial(_mla_attn_kernel, tile=t, heads=MLA_HEADS_PER_STEP),
        grid=(batch, groups, seq // t),
        in_specs=[pl.BlockSpec((1, t, qk_width), lambda b, p, i: (b, i, p)),
                  pl.BlockSpec((1, seq, qk_width), lambda b, p, i: (b, 0, p)),
                  pl.BlockSpec((1, seq, v_width), lambda b, p, i: (b, 0, p))],
        out_specs=pl.BlockSpec((1, t, v_width), lambda b, p, i: (b, i, p)),
        out_shape=jax.ShapeDtypeStruct((batch, seq, MLA_HEADS * MLA_V_DIM), BF16),
        compiler_params=_params(),
        name="mla_attention",
    )(q, k, v)


def _rope_lane_freqs():
    inv_freq = ROPE_THETA ** (-jnp.arange(0, MLA_ROPE_DIM, 2, dtype=F32) / MLA_ROPE_DIM)
    half = MLA_ROPE_DIM // 2
    out = jnp.zeros((1, LANES), F32)
    out = out.at[0, MLA_NOPE_DIM:MLA_NOPE_DIM + half].set(inv_freq)
    out = out.at[0, MLA_NOPE_DIM + half:MLA_NOPE_DIM + 2 * half].set(inv_freq)
    return out


def _layout_w_dkv(w_dkv):
    half = MLA_ROPE_DIM // 2
    d = w_dkv.shape[0]
    t1 = w_dkv[:, MLA_KV_RANK:MLA_KV_RANK + half]
    t2 = w_dkv[:, MLA_KV_RANK + half:]
    pad_lo = jnp.zeros((d, MLA_NOPE_DIM), w_dkv.dtype)
    pad_hi = jnp.zeros((d, LANES - MLA_NOPE_DIM - MLA_ROPE_DIM), w_dkv.dtype)
    blk_a = jnp.concatenate([pad_lo, t1, t2, pad_hi], axis=1)
    blk_b = jnp.concatenate([pad_lo, -t2, t1, pad_hi], axis=1)
    return jnp.concatenate([w_dkv[:, :MLA_KV_RANK], blk_a, blk_b], axis=1).astype(BF16)


def _layout_w_ukv(w_ukv):
    r = w_ukv.shape[0]
    w = w_ukv.reshape(r, MLA_HEADS, MLA_NOPE_DIM + MLA_V_DIM)
    wk = jnp.concatenate(
        [w[:, :, :MLA_NOPE_DIM], jnp.zeros((r, MLA_HEADS, LANES - MLA_NOPE_DIM), w.dtype)],
        axis=2).reshape(r, MLA_HEADS * LANES)
    wv = w[:, :, MLA_NOPE_DIM:].reshape(r, MLA_HEADS * MLA_V_DIM)
    return wk.astype(BF16), wv.astype(BF16)


def _layout_w_uq(w_uq):
    r = w_uq.shape[0]
    half = MLA_ROPE_DIM // 2
    w = w_uq.reshape(r, MLA_HEADS, MLA_NOPE_DIM + MLA_ROPE_DIM)
    nope = w[:, :, :MLA_NOPE_DIM]
    t1 = w[:, :, MLA_NOPE_DIM:MLA_NOPE_DIM + half]
    t2 = w[:, :, MLA_NOPE_DIM + half:]
    pad_hi = jnp.zeros((r, MLA_HEADS, LANES - MLA_NOPE_DIM - MLA_ROPE_DIM), w.dtype)
    wa = jnp.concatenate([nope, t1, t2, pad_hi], axis=2)
    wb = jnp.concatenate([jnp.zeros_like(nope), -t2, t1, pad_hi], axis=2)
    shape = (r, MLA_HEADS * LANES)
    return wa.reshape(shape).astype(BF16), wb.reshape(shape).astype(BF16)


def kernel(x, positions, attn_norm, mlp_norm, sb_w_qkv, sb_w_o, kv_norm, mla_w_dkv,
           mla_kv_lat_norm, mla_w_ukv, mla_w_dq, mla_q_lat_norm, mla_w_uq, mla_w_o,
           mlp_w1, mlp_w2, final_norm):
    batch, seq, d = x.shape
    n = batch * seq
    xs = x.reshape(n, d)
    pos = positions.astype(F32).reshape(n, 1)
    inv_lane = _rope_lane_freqs()

    for layer in range(N_A_LAYERS):
        qkv = _norm_matmul(xs, attn_norm[layer].reshape(1, d), sb_w_qkv[layer].astype(BF16))
        attn = _sb_attention(qkv.reshape(batch, seq, 3 * d), batch, seq)
        xs = _proj_mlp(xs, attn.reshape(n, d), sb_w_o[layer].astype(BF16),
                       mlp_norm[layer].reshape(1, d),
                       mlp_w1[layer].astype(BF16), mlp_w2[layer].astype(BF16))

    wk, wv = _layout_w_ukv(mla_w_ukv)
    k_all, v_all = _shared_kv(xs, pos, inv_lane, kv_norm.reshape(1, d),
                              _layout_w_dkv(mla_w_dkv),
                              mla_kv_lat_norm.reshape(1, MLA_KV_RANK), wk, wv)
    k_all = k_all.reshape(batch, seq, MLA_HEADS * LANES)
    v_all = v_all.reshape(batch, seq, MLA_HEADS * MLA_V_DIM)

    for layer in range(N_A_LAYERS, DEPTH):
        j = layer - N_A_LAYERS
        wa, wb = _layout_w_uq(mla_w_uq[j])
        q = _mla_q(xs, pos, inv_lane, attn_norm[layer].reshape(1, d),
                   mla_w_dq[j].astype(BF16), mla_q_lat_norm[j].reshape(1, MLA_Q_RANK), wa, wb)
        attn = _mla_attention(q.reshape(batch, seq, MLA_HEADS * LANES), k_all, v_all, batch, seq)
        xs = _proj_mlp(xs, attn.reshape(n, MLA_HEADS * MLA_V_DIM), mla_w_o[j].astype(BF16),
                       mlp_norm[layer].reshape(1, d),
                       mlp_w1[layer].astype(BF16), mlp_w2[layer].astype(BF16),
                       final_g=final_norm.reshape(1, d) if layer == DEPTH - 1 else None)
    return xs.reshape(batch, seq, d)
```

```python
import functools
import math

import jax
import jax.numpy as jnp
from jax import lax
from jax.experimental import pallas as pl
from jax.experimental.pallas import tpu as pltpu

D_MODEL = 1024
DEPTH = 4
CHUNK = 64
N_A_LAYERS = DEPTH // 2
SB_HEADS = 16
SB_HEAD_DIM = D_MODEL // SB_HEADS
MLA_HEADS = 16
MLA_NOPE_DIM = 64
MLA_ROPE_DIM = 32
MLA_V_DIM = 64
MLA_Q_RANK = 384
MLA_KV_RANK = 256
D_FF = 4 * D_MODEL
ROPE_THETA = 10000.0
NORM_EPS = 1e-6

LANES = 128
ROW_TILE = 512
SB_TILE = 128
SB_HEADS_PER_STEP = 8
SB_DEAD_DROP = 104.0
SB_VOID = 1e30
SB_FIRST_SWEEP = 3
SB_LOOP_SWEEP = 2
MLA_TILE = 512
MLA_HEADS_PER_STEP = 4
MLA_BLOCK_TILES = 4
VMEM_LIMIT = 56 * 1024 * 1024
MASK_VALUE = -1e30

F32 = jnp.float32
BF16 = jnp.bfloat16


def _rms(x, g):
    return x * lax.rsqrt(jnp.mean(x * x, axis=-1, keepdims=True) + NORM_EPS) * g


def _dot(a, b):
    return jnp.dot(a, b, preferred_element_type=F32)


def _dot_nt(a, b):
    return lax.dot_general(a, b, (((1,), (1,)), ((), ())), preferred_element_type=F32)


def _params():
    return pltpu.CompilerParams(
        dimension_semantics=None, vmem_limit_bytes=VMEM_LIMIT)


def _resident(shape):
    return pl.BlockSpec(shape, lambda *_: (0,) * len(shape),
                        pipeline_mode=pl.Buffered(1))


def _norm_matmul_kernel(x_ref, g_ref, w_ref, o_ref, *, col_chunk):
    h = _rms(x_ref[...], g_ref[...]).astype(BF16)
    n_out = o_ref.shape[1]
    for c in range(n_out // col_chunk):
        sl = slice(c * col_chunk, (c + 1) * col_chunk)
        o_ref[:, sl] = _dot(h, w_ref[:, sl]).astype(o_ref.dtype)


def _norm_matmul(x, g, w):
    n, d = x.shape
    n_out = w.shape[1]
    return pl.pallas_call(
        functools.partial(_norm_matmul_kernel, col_chunk=1024),
        grid=(n // ROW_TILE,),
        in_specs=[pl.BlockSpec((ROW_TILE, d), lambda i: (i, 0)),
                  _resident((1, d)),
                  _resident((d, n_out))],
        out_specs=pl.BlockSpec((ROW_TILE, n_out), lambda i: (i, 0)),
        out_shape=jax.ShapeDtypeStruct((n, n_out), BF16),
        compiler_params=_params(),
        name="norm_qkv",
    )(x, g, w)


def _sb_attn_kernel(q_ref, k_ref, v_ref, o_ref, *, tile, heads):
    qi = pl.program_id(2)
    pairs = heads // 2
    lane = lax.broadcasted_iota(jnp.int32, (1, LANES), 1)
    halves = (lane < SB_HEAD_DIM, lane >= SB_HEAD_DIM)
    row = lax.broadcasted_iota(jnp.int32, (tile, tile), 0)
    col = lax.broadcasted_iota(jnp.int32, (tile, tile), 1)
    from_key = (row >= col).astype(BF16)
    from_key2 = jnp.concatenate([from_key, from_key], axis=0)
    causal = col < row
    causal2 = jnp.concatenate([causal, causal], axis=0)

    def masked(x, keep):
        return jnp.where(keep, x, jnp.zeros_like(x))

    def blk(j):
        return slice(j * LANES, (j + 1) * LANES)

    q_pairs = []
    for j in range(pairs):
        q = q_ref[0, :, blk(j)] * (SB_HEAD_DIM ** -0.5)
        q_pairs.append(jnp.concatenate([masked(q, halves[0]), masked(q, halves[1])], axis=0))

    def sweep(kb0, count, accs, cs, diag_first):
        kbs = [kb0 - i for i in range(count)]
        starts = [pl.multiple_of(jnp.maximum(kb, 0) * tile, tile) for kb in kbs]
        voids = [jnp.where(kb >= 0, 0.0, SB_VOID) for kb in kbs]
        zs = [[_dot_nt(q_pairs[j], k_ref[0, pl.ds(st, tile), blk(j)])
               for j in range(pairs)] for st in starts]
        splits = []
        for i in range(count):
            for z in zs[i]:
                drop = jnp.maximum(z, 0.0) + jnp.log(1.0 + jnp.exp(-jnp.abs(z)))
                if diag_first and i == 0:
                    drop = jnp.where(causal2, drop, 0.0)
                hi = drop.astype(BF16)
                lo = (drop - hi.astype(F32)).astype(BF16)
                splits.append(jnp.concatenate([hi, lo], axis=1))
        totals = _dot(jnp.concatenate(splits, axis=0), from_key2)
        accs, cs = list(accs), list(cs)
        for i in range(count):
            for j in range(pairs):
                base = (i * pairs + j) * 2 * tile
                total = totals[base:base + 2 * tile]
                cs[j] = cs[j] + voids[i]
                w = jnp.exp(zs[i][j] - total - cs[j])
                if diag_first and i == 0:
                    w = jnp.where(causal2, w, 0.0)
                w = w.astype(BF16)
                v = v_ref[0, pl.ds(starts[i], tile), blk(j)]
                accs[j] = accs[j] + _dot(
                    jnp.concatenate([w[:tile], w[tile:]], axis=1),
                    jnp.concatenate([masked(v, halves[0]), masked(v, halves[1])], axis=0))
                cs[j] = cs[j] + total[:, 0:1]
        return tuple(accs), tuple(cs)

    def alive_flag(cs):
        low = functools.reduce(jnp.minimum, cs)
        return (jnp.min(low) < SB_DEAD_DROP).astype(jnp.int32)

    accs = tuple(jnp.zeros((tile, LANES), F32) for _ in range(pairs))
    cs = tuple(jnp.zeros((2 * tile, 1), F32) for _ in range(pairs))
    accs, cs = sweep(qi, SB_FIRST_SWEEP, accs, cs, True)

    def cond(state):
        kb, _, _, alive = state
        return jnp.logical_and(kb >= 0, alive > 0)

    def body(state):
        kb, accs, cs, _ = state
        accs, cs = sweep(kb, SB_LOOP_SWEEP, accs, cs, False)
        return kb - SB_LOOP_SWEEP, accs, cs, alive_flag(cs)

    _, accs, _, _ = lax.while_loop(
        cond, body, (qi - SB_FIRST_SWEEP, accs, cs, alive_flag(cs)))
    for j, acc in enumerate(accs):
        o_ref[0, :, blk(j)] = acc.astype(o_ref.dtype)


def _sb_attention(qkv, batch, seq):
    width = SB_HEADS_PER_STEP * SB_HEAD_DIM
    groups = D_MODEL // width
    t = SB_TILE
    return pl.pallas_call(
        functools.partial(_sb_attn_kernel, tile=t, heads=SB_HEADS_PER_STEP),
        grid=(batch, groups, seq // t),
        in_specs=[pl.BlockSpec((1, t, width), lambda b, p, i: (b, i, p)),
                  pl.BlockSpec((1, seq, width), lambda b, p, i: (b, 0, groups + p)),
                  pl.BlockSpec((1, seq, width), lambda b, p, i: (b, 0, 2 * groups + p))],
        out_specs=pl.BlockSpec((1, t, width), lambda b, p, i: (b, i, p)),
        out_shape=jax.ShapeDtypeStruct((batch, seq, D_MODEL), BF16),
        compiler_params=_params(),
        name="sb_attention",
    )(qkv, qkv, qkv)


def _proj_mlp_kernel(x_ref, a_ref, wo_ref, g_ref, w1_ref, w2_ref, *rest, ff_chunk, final):
    if final:
        gf_ref, o_ref = rest
    else:
        (o_ref,) = rest
    x1 = x_ref[...] + _dot(a_ref[...], wo_ref[...])
    h = _rms(x1, g_ref[...]).astype(BF16)
    acc = x1
    for c in range(D_FF // ff_chunk):
        sl = slice(c * ff_chunk, (c + 1) * ff_chunk)
        u = jnp.maximum(_dot(h, w1_ref[:, sl]), 0.0)
        acc = acc + _dot((u * u).astype(BF16), w2_ref[sl, :])
    if final:
        acc = _rms(acc, gf_ref[...])
    o_ref[...] = acc


def _proj_mlp(x, a, wo, g, w1, w2, final_g=None):
    n, d = x.shape
    final = final_g is not None
    row = lambda i: (i, 0)
    in_specs = [pl.BlockSpec((ROW_TILE, d), row),
                pl.BlockSpec((ROW_TILE, a.shape[1]), row),
                _resident(wo.shape), _resident((1, d)),
                _resident(w1.shape), _resident(w2.shape)]
    args = [x, a, wo, g, w1, w2]
    if final:
        in_specs.append(_resident((1, d)))
        args.append(final_g)
    return pl.pallas_call(
        functools.partial(_proj_mlp_kernel, ff_chunk=1024, final=final),
        grid=(n // ROW_TILE,),
        in_specs=in_specs,
        out_specs=pl.BlockSpec((ROW_TILE, d), row),
        out_shape=jax.ShapeDtypeStruct((n, d), F32),
        compiler_params=_params(),
        name="proj_mlp_final" if final else "proj_mlp",
    )(*args)


def _rope_tables(pos, inv_lane):
    lane = lax.broadcasted_iota(jnp.int32, (1, LANES), 1)
    rope = jnp.logical_and(lane >= MLA_NOPE_DIM, lane < MLA_NOPE_DIM + MLA_ROPE_DIM)
    ang = pos * inv_lane
    cos_t = jnp.where(lane < MLA_NOPE_DIM, 1.0, jnp.where(rope, jnp.cos(ang), 0.0))
    sin_t = jnp.where(rope, jnp.sin(ang), 0.0)
    return cos_t, sin_t


def _shared_kv_kernel(x_ref, pos_ref, inv_ref, g_ref, wd_ref, gl_ref, wk_ref, wv_ref,
                      k_ref, v_ref):
    h = _rms(x_ref[...], g_ref[...]).astype(BF16)
    down = _dot(h, wd_ref[...])
    c = _rms(down[:, :MLA_KV_RANK], gl_ref[...]).astype(BF16)
    cos_t, sin_t = _rope_tables(pos_ref[...], inv_ref[...])
    k_rope = (down[:, MLA_KV_RANK:MLA_KV_RANK + LANES] * cos_t
              + down[:, MLA_KV_RANK + LANES:] * sin_t)
    k_nope = _dot(c, wk_ref[...])
    for hd in range(MLA_HEADS):
        sl = slice(hd * LANES, (hd + 1) * LANES)
        k_ref[:, sl] = (k_nope[:, sl] + k_rope).astype(k_ref.dtype)
    v_ref[...] = _dot(c, wv_ref[...]).astype(v_ref.dtype)


def _shared_kv(x, pos, inv_lane, g, wd, gl, wk, wv):
    n, d = x.shape
    row = lambda i: (i, 0)
    return pl.pallas_call(
        _shared_kv_kernel,
        grid=(n // ROW_TILE,),
        in_specs=[pl.BlockSpec((ROW_TILE, d), row),
                  pl.BlockSpec((ROW_TILE, 1), row),
                  _resident((1, LANES)), _resident((1, d)), _resident(wd.shape),
                  _resident((1, MLA_KV_RANK)), _resident(wk.shape), _resident(wv.shape)],
        out_specs=[pl.BlockSpec((ROW_TILE, MLA_HEADS * LANES), row),
                   pl.BlockSpec((ROW_TILE, MLA_HEADS * MLA_V_DIM), row)],
        out_shape=[jax.ShapeDtypeStruct((n, MLA_HEADS * LANES), BF16),
                   jax.ShapeDtypeStruct((n, MLA_HEADS * MLA_V_DIM), BF16)],
        compiler_params=_params(),
        name="mla_shared_kv",
    )(x, pos, inv_lane, g, wd, gl, wk, wv)


def _mla_q_kernel(x_ref, pos_ref, inv_ref, g_ref, wdq_ref, gq_ref, wa_ref, wb_ref, q_ref,
                  *, heads_per_chunk):
    h = _rms(x_ref[...], g_ref[...]).astype(BF16)
    cq = _rms(_dot(h, wdq_ref[...]), gq_ref[...]).astype(BF16)
    cos_t, sin_t = _rope_tables(pos_ref[...], inv_ref[...])
    scale = (MLA_NOPE_DIM + MLA_ROPE_DIM) ** -0.5 * math.log2(math.e)
    cos_t = cos_t * scale
    sin_t = sin_t * scale
    width = heads_per_chunk * LANES
    for c in range(MLA_HEADS // heads_per_chunk):
        sl = slice(c * width, (c + 1) * width)
        qa = _dot(cq, wa_ref[:, sl])
        qb = _dot(cq, wb_ref[:, sl])
        for j in range(heads_per_chunk):
            hs = slice(j * LANES, (j + 1) * LANES)
            q_ref[:, c * width + j * LANES:c * width + (j + 1) * LANES] = (
                qa[:, hs] * cos_t + qb[:, hs] * sin_t).astype(q_ref.dtype)


def _mla_q(x, pos, inv_lane, g, wdq, gq, wa, wb):
    n, d = x.shape
    row = lambda i: (i, 0)
    return pl.pallas_call(
        functools.partial(_mla_q_kernel, heads_per_chunk=4),
        grid=(n // ROW_TILE,),
        in_specs=[pl.BlockSpec((ROW_TILE, d), row),
                  pl.BlockSpec((ROW_TILE, 1), row),
                  _resident((1, LANES)), _resident((1, d)), _resident(wdq.shape),
                  _resident((1, MLA_Q_RANK)), _resident(wa.shape), _resident(wb.shape)],
        out_specs=pl.BlockSpec((ROW_TILE, MLA_HEADS * LANES), row),
        out_shape=jax.ShapeDtypeStruct((n, MLA_HEADS * LANES), BF16),
        compiler_params=_params(),
        name="mla_q",
    )(x, pos, inv_lane, g, wdq, gq, wa, wb)


def _mla_attn_kernel(q_ref, k_ref, v_ref, o_ref, *, tile, heads):
    qi = pl.program_id(2)
    pairs = heads // 2
    lane = lax.broadcasted_iota(jnp.int32, (1, LANES), 1)
    halves = (lane < MLA_V_DIM, lane >= MLA_V_DIM)
    row = lax.broadcasted_iota(jnp.int32, (tile, tile), 0)
    col = lax.broadcasted_iota(jnp.int32, (tile, tile), 1)
    allowed = (col // CHUNK) <= (row // CHUNK)

    def masked(x, keep):
        return jnp.where(keep, x, jnp.zeros_like(x))

    def blk(j):
        return slice(j * LANES, (j + 1) * LANES)

    key_row = lax.broadcasted_iota(jnp.int32, (2 * tile, 1), 0)
    ones_cols = jnp.where((key_row < tile) == halves[0], 1.0, 0.0).astype(BF16)

    def sweep(kbs, diag_last, carry):
        ms, accs = carry
        starts = [pl.multiple_of(kb * tile, tile) for kb in kbs]
        ss = [[_dot_nt(q_ref[0, :, blk(h)], k_ref[0, pl.ds(st, tile), blk(h)])
               for st in starts] for h in range(heads)]
        if diag_last:
            for s_head in ss:
                s_head[-1] = jnp.where(allowed, s_head[-1], MASK_VALUE)
        count = len(kbs)
        ps, alphas, new_ms = [], [], []
        for h in range(heads):
            top = functools.reduce(jnp.maximum, ss[h])
            m_new = jnp.maximum(ms[h], jnp.max(top, axis=-1, keepdims=True))
            alphas.append(jnp.exp2(ms[h] - m_new))
            new_ms.append(m_new)
            ps.append([jnp.exp2(s - m_new).astype(BF16) for s in ss[h]])
        new_accs = []
        for j in range(pairs):
            lhs, rhs = [], []
            for n, st in enumerate(starts):
                v = v_ref[0, pl.ds(st, tile), blk(j)]
                vals = jnp.concatenate([masked(v, halves[0]), masked(v, halves[1])], axis=0)
                lhs += [ps[2 * j][n], ps[2 * j + 1][n]]
                rhs.append(jnp.concatenate([vals, ones_cols], axis=1))
            pv = _dot(jnp.concatenate(lhs, axis=1),
                      jnp.concatenate(rhs, axis=0))
            alpha = jnp.where(halves[0], alphas[2 * j], alphas[2 * j + 1])
            new_accs.append(jnp.concatenate([alpha, alpha], axis=1) * accs[j] + pv)
        return tuple(new_ms), tuple(new_accs)

    carry = (tuple(jnp.full((tile, 1), MASK_VALUE, F32) for _ in range(heads)),
             tuple(jnp.zeros((tile, 2 * LANES), F32) for _ in range(pairs)))
    nblk = MLA_BLOCK_TILES
    carry = lax.fori_loop(
        0, qi // nblk, lambda i, cr: sweep([nblk * i + t for t in range(nblk)], False, cr), carry)

    def last_block(rest):
        return lambda cr: sweep([qi - rest + t for t in range(rest)] + [qi], True, cr)

    _, accs = lax.switch(qi % nblk, [last_block(rest) for rest in range(nblk)], carry)
    for j in range(pairs):
        o_ref[0, :, blk(j)] = (accs[j][:, :LANES] / accs[j][:, LANES:]).astype(o_ref.dtype)


def _mla_attention(q, k, v, batch, seq):
    groups = MLA_HEADS // MLA_HEADS_PER_STEP
    t = MLA_TILE
    qk_width = MLA_HEADS_PER_STEP * LANES
    v_width = MLA_HEADS_PER_STEP * MLA_V_DIM
    return pl.pallas_call(
        functools.partial(_mla_attn_kernel, tile=t, heads=MLA_HEADS_PER_STEP),
        grid=(batch, groups, seq // t),
        in_specs=[pl.BlockSpec((1, t, qk_width), lambda b, p, i: (b, i, p)),
                  pl.BlockSpec((1, seq, qk_width), lambda b, p, i: (b, 0, p)),
                  pl.BlockSpec((1, seq, v_width), lambda b, p, i: (b, 0, p))],
        out_specs=pl.BlockSpec((1, t, v_width), lambda b, p, i: (b, i, p)),
        out_shape=jax.ShapeDtypeStruct((batch, seq, MLA_HEADS * MLA_V_DIM), BF16),
        compiler_params=_params(),
        name="mla_attention",
    )(q, k, v)


def _rope_lane_freqs():
    inv_freq = ROPE_THETA ** (-jnp.arange(0, MLA_ROPE_DIM, 2, dtype=F32) / MLA_ROPE_DIM)
    half = MLA_ROPE_DIM // 2
    out = jnp.zeros((1, LANES), F32)
    out = out.at[0, MLA_NOPE_DIM:MLA_NOPE_DIM + half].set(inv_freq)
    out = out.at[0, MLA_NOPE_DIM + half:MLA_NOPE_DIM + 2 * half].set(inv_freq)
    return out


def _layout_w_dkv(w_dkv):
    half = MLA_ROPE_DIM // 2
    d = w_dkv.shape[0]
    t1 = w_dkv[:, MLA_KV_RANK:MLA_KV_RANK + half]
    t2 = w_dkv[:, MLA_KV_RANK + half:]
    pad_lo = jnp.zeros((d, MLA_NOPE_DIM), w_dkv.dtype)
    pad_hi = jnp.zeros((d, LANES - MLA_NOPE_DIM - MLA_ROPE_DIM), w_dkv.dtype)
    blk_a = jnp.concatenate([pad_lo, t1, t2, pad_hi], axis=1)
    blk_b = jnp.concatenate([pad_lo, -t2, t1, pad_hi], axis=1)
    return jnp.concatenate([w_dkv[:, :MLA_KV_RANK], blk_a, blk_b], axis=1).astype(BF16)


def _layout_w_ukv(w_ukv):
    r = w_ukv.shape[0]
    w = w_ukv.reshape(r, MLA_HEADS, MLA_NOPE_DIM + MLA_V_DIM)
    wk = jnp.concatenate(
        [w[:, :, :MLA_NOPE_DIM], jnp.zeros((r, MLA_HEADS, LANES - MLA_NOPE_DIM), w.dtype)],
        axis=2).reshape(r, MLA_HEADS * LANES)
    wv = w[:, :, MLA_NOPE_DIM:].reshape(r, MLA_HEADS * MLA_V_DIM)
    return wk.astype(BF16), wv.astype(BF16)


def _layout_w_uq(w_uq):
    r = w_uq.shape[0]
    half = MLA_ROPE_DIM // 2
    w = w_uq.reshape(r, MLA_HEADS, MLA_NOPE_DIM + MLA_ROPE_DIM)
    nope = w[:, :, :MLA_NOPE_DIM]
    t1 = w[:, :, MLA_NOPE_DIM:MLA_NOPE_DIM + half]
    t2 = w[:, :, MLA_NOPE_DIM + half:]
    pad_hi = jnp.zeros((r, MLA_HEADS, LANES - MLA_NOPE_DIM - MLA_ROPE_DIM), w.dtype)
    wa = jnp.concatenate([nope, t1, t2, pad_hi], axis=2)
    wb = jnp.concatenate([jnp.zeros_like(nope), -t2, t1, pad_hi], axis=2)
    shape = (r, MLA_HEADS * LANES)
    return wa.reshape(shape).astype(BF16), wb.reshape(shape).astype(BF16)


def kernel(x, positions, attn_norm, mlp_norm, sb_w_qkv, sb_w_o, kv_norm, mla_w_dkv,
           mla_kv_lat_norm, mla_w_ukv, mla_w_dq, mla_q_lat_norm, mla_w_uq, mla_w_o,
           mlp_w1, mlp_w2, final_norm):
    batch, seq, d = x.shape
    n = batch * seq
    xs = x.reshape(n, d)
    pos = positions.astype(F32).reshape(n, 1)
    inv_lane = _rope_lane_freqs()

    for layer in range(N_A_LAYERS):
        qkv = _norm_matmul(xs, attn_norm[layer].reshape(1, d), sb_w_qkv[layer].astype(BF16))
        attn = _sb_attention(qkv.reshape(batch, seq, 3 * d), batch, seq)
        xs = _proj_mlp(xs, attn.reshape(n, d), sb_w_o[layer].astype(BF16),
                       mlp_norm[layer].reshape(1, d),
                       mlp_w1[layer].astype(BF16), mlp_w2[layer].astype(BF16))

    wk, wv = _layout_w_ukv(mla_w_ukv)
    k_all, v_all = _shared_kv(xs, pos, inv_lane, kv_norm.reshape(1, d),
                              _layout_w_dkv(mla_w_dkv),
                              mla_kv_lat_norm.reshape(1, MLA_KV_RANK), wk, wv)
    k_all = k_all.reshape(batch, seq, MLA_HEADS * LANES)
    v_all = v_all.reshape(batch, seq, MLA_HEADS * MLA_V_DIM)

    for layer in range(N_A_LAYERS, DEPTH):
        j = layer - N_A_LAYERS
        wa, wb = _layout_w_uq(mla_w_uq[j])
        q = _mla_q(xs, pos, inv_lane, attn_norm[layer].reshape(1, d),
                   mla_w_dq[j].astype(BF16), mla_q_lat_norm[j].reshape(1, MLA_Q_RANK), wa, wb)
        attn = _mla_attention(q.reshape(batch, seq, MLA_HEADS * LANES), k_all, v_all, batch, seq)
        xs = _proj_mlp(xs, attn.reshape(n, MLA_HEADS * MLA_V_DIM), mla_w_o[j].astype(BF16),
                       mlp_norm[layer].reshape(1, d),
                       mlp_w1[layer].astype(BF16), mlp_w2[layer].astype(BF16),
                       final_g=final_norm.reshape(1, d) if layer == DEPTH - 1 else None)
    return xs.reshape(batch, seq, d)
```

```python
import functools
import math

import jax
import jax.numpy as jnp
from jax import lax
from jax.experimental import pallas as pl
from jax.experimental.pallas import tpu as pltpu

D_MODEL = 1024
DEPTH = 4
CHUNK = 64
N_A_LAYERS = DEPTH // 2
SB_HEADS = 16
SB_HEAD_DIM = D_MODEL // SB_HEADS
MLA_HEADS = 16
MLA_NOPE_DIM = 64
MLA_ROPE_DIM = 32
MLA_V_DIM = 64
MLA_Q_RANK = 384
MLA_KV_RANK = 256
D_FF = 4 * D_MODEL
ROPE_THETA = 10000.0
NORM_EPS = 1e-6

LANES = 128
ROW_TILE = 512
SB_TILE = 128
SB_HEADS_PER_STEP = 8
SB_DEAD_DROP = 104.0
SB_VOID = 1e30
SB_FIRST_SWEEP = 3
SB_LOOP_SWEEP = 2
MLA_TILE = 512
MLA_HEADS_PER_STEP = 4
MLA_BLOCK_TILES = 4
VMEM_LIMIT = 56 * 1024 * 1024
MASK_VALUE = -1e30
LOG2_E = math.log2(math.e)

F32 = jnp.float32
BF16 = jnp.bfloat16


def _rms(x, g):
    return x * lax.rsqrt(jnp.mean(x * x, axis=-1, keepdims=True) + NORM_EPS) * g


def _dot(a, b):
    return jnp.dot(a, b, preferred_element_type=F32)


def _dot_nt(a, b):
    return lax.dot_general(a, b, (((1,), (1,)), ((), ())), preferred_element_type=F32)


def _params():
    return pltpu.CompilerParams(
        dimension_semantics=None, vmem_limit_bytes=VMEM_LIMIT)


def _resident(shape):
    return pl.BlockSpec(shape, lambda *_: (0,) * len(shape),
                        pipeline_mode=pl.Buffered(1))


def _norm_matmul_kernel(x_ref, g_ref, w_ref, o_ref, *, col_chunk):
    h = _rms(x_ref[...], g_ref[...]).astype(BF16)
    n_out = o_ref.shape[1]
    for c in range(n_out // col_chunk):
        sl = slice(c * col_chunk, (c + 1) * col_chunk)
        o_ref[:, sl] = _dot(h, w_ref[:, sl]).astype(o_ref.dtype)


def _norm_matmul(x, g, w):
    n, d = x.shape
    n_out = w.shape[1]
    return pl.pallas_call(
        functools.partial(_norm_matmul_kernel, col_chunk=1024),
        grid=(n // ROW_TILE,),
        in_specs=[pl.BlockSpec((ROW_TILE, d), lambda i: (i, 0)),
                  _resident((1, d)),
                  _resident((d, n_out))],
        out_specs=pl.BlockSpec((ROW_TILE, n_out), lambda i: (i, 0)),
        out_shape=jax.ShapeDtypeStruct((n, n_out), BF16),
        compiler_params=_params(),
        name="norm_qkv",
    )(x, g, w)


def _sb_attn_kernel(q_ref, k_ref, v_ref, o_ref, *, tile, heads):
    qi = pl.program_id(2)
    pairs = heads // 2
    lane = lax.broadcasted_iota(jnp.int32, (1, LANES), 1)
    halves = (lane < SB_HEAD_DIM, lane >= SB_HEAD_DIM)
    row = lax.broadcasted_iota(jnp.int32, (tile, tile), 0)
    col = lax.broadcasted_iota(jnp.int32, (tile, tile), 1)
    from_key = (row >= col).astype(BF16)
    from_key2 = jnp.concatenate([from_key, from_key], axis=0)
    causal = col < row
    causal2 = jnp.concatenate([causal, causal], axis=0)

    def masked(x, keep):
        return jnp.where(keep, x, jnp.zeros_like(x))

    def blk(j):
        return slice(j * LANES, (j + 1) * LANES)

    q_pairs = []
    for j in range(pairs):
        q = q_ref[0, :, blk(j)] * (SB_HEAD_DIM ** -0.5)
        q_pairs.append(jnp.concatenate([masked(q, halves[0]), masked(q, halves[1])], axis=0))

    def sweep(kb0, count, accs, cs, diag_first):
        kbs = [kb0 - i for i in range(count)]
        starts = [pl.multiple_of(jnp.maximum(kb, 0) * tile, tile) for kb in kbs]
        voids = [jnp.where(kb >= 0, 0.0, SB_VOID) for kb in kbs]
        zs = [[_dot_nt(q_pairs[j], k_ref[0, pl.ds(st, tile), blk(j)])
               for j in range(pairs)] for st in starts]
        splits = []
        for i in range(count):
            for z in zs[i]:
                drop = jnp.maximum(z, 0.0) + jnp.log(1.0 + jnp.exp2(jnp.abs(z) * -LOG2_E))
                if diag_first and i == 0:
                    drop = jnp.where(causal2, drop, 0.0)
                hi = drop.astype(BF16)
                lo = (drop - hi.astype(F32)).astype(BF16)
                splits.append(jnp.concatenate([hi, lo], axis=1))
        totals = _dot(jnp.concatenate(splits, axis=0), from_key2)
        accs, cs = list(accs), list(cs)
        for i in range(count):
            for j in range(pairs):
                base = (i * pairs + j) * 2 * tile
                total = totals[base:base + 2 * tile]
                cs[j] = cs[j] + voids[i]
                w = jnp.exp(zs[i][j] - total - cs[j])
                if diag_first and i == 0:
                    w = jnp.where(causal2, w, 0.0)
                w = w.astype(BF16)
                v = v_ref[0, pl.ds(starts[i], tile), blk(j)]
                accs[j] = accs[j] + _dot(
                    jnp.concatenate([w[:tile], w[tile:]], axis=1),
                    jnp.concatenate([masked(v, halves[0]), masked(v, halves[1])], axis=0))
                cs[j] = cs[j] + total[:, 0:1]
        return tuple(accs), tuple(cs)

    def alive_flag(cs):
        low = functools.reduce(jnp.minimum, cs)
        return (jnp.min(low) < SB_DEAD_DROP).astype(jnp.int32)

    accs = tuple(jnp.zeros((tile, LANES), F32) for _ in range(pairs))
    cs = tuple(jnp.zeros((2 * tile, 1), F32) for _ in range(pairs))
    accs, cs = sweep(qi, SB_FIRST_SWEEP, accs, cs, True)

    def cond(state):
        kb, _, _, alive = state
        return jnp.logical_and(kb >= 0, alive > 0)

    def body(state):
        kb, accs, cs, _ = state
        accs, cs = sweep(kb, SB_LOOP_SWEEP, accs, cs, False)
        return kb - SB_LOOP_SWEEP, accs, cs, alive_flag(cs)

    _, accs, _, _ = lax.while_loop(
        cond, body, (qi - SB_FIRST_SWEEP, accs, cs, alive_flag(cs)))
    for j, acc in enumerate(accs):
        o_ref[0, :, blk(j)] = acc.astype(o_ref.dtype)


def _sb_attention(qkv, batch, seq):
    width = SB_HEADS_PER_STEP * SB_HEAD_DIM
    groups = D_MODEL // width
    t = SB_TILE
    return pl.pallas_call(
        functools.partial(_sb_attn_kernel, tile=t, heads=SB_HEADS_PER_STEP),
        grid=(batch, groups, seq // t),
        in_specs=[pl.BlockSpec((1, t, width), lambda b, p, i: (b, i, p)),
                  pl.BlockSpec((1, seq, width), lambda b, p, i: (b, 0, groups + p)),
                  pl.BlockSpec((1, seq, width), lambda b, p, i: (b, 0, 2 * groups + p))],
        out_specs=pl.BlockSpec((1, t, width), lambda b, p, i: (b, i, p)),
        out_shape=jax.ShapeDtypeStruct((batch, seq, D_MODEL), BF16),
        compiler_params=_params(),
        name="sb_attention",
    )(qkv, qkv, qkv)


def _proj_mlp_kernel(x_ref, a_ref, wo_ref, g_ref, w1_ref, w2_ref, *rest, ff_chunk, final):
    if final:
        gf_ref, o_ref = rest
    else:
        (o_ref,) = rest
    x1 = x_ref[...] + _dot(a_ref[...], wo_ref[...])
    h = _rms(x1, g_ref[...]).astype(BF16)
    acc = x1
    for c in range(D_FF // ff_chunk):
        sl = slice(c * ff_chunk, (c + 1) * ff_chunk)
        u = jnp.maximum(_dot(h, w1_ref[:, sl]), 0.0)
        acc = acc + _dot((u * u).astype(BF16), w2_ref[sl, :])
    if final:
        acc = _rms(acc, gf_ref[...])
    o_ref[...] = acc


def _proj_mlp(x, a, wo, g, w1, w2, final_g=None):
    n, d = x.shape
    final = final_g is not None
    row = lambda i: (i, 0)
    in_specs = [pl.BlockSpec((ROW_TILE, d), row),
                pl.BlockSpec((ROW_TILE, a.shape[1]), row),
                _resident(wo.shape), _resident((1, d)),
                _resident(w1.shape), _resident(w2.shape)]
    args = [x, a, wo, g, w1, w2]
    if final:
        in_specs.append(_resident((1, d)))
        args.append(final_g)
    return pl.pallas_call(
        functools.partial(_proj_mlp_kernel, ff_chunk=1024, final=final),
        grid=(n // ROW_TILE,),
        in_specs=in_specs,
        out_specs=pl.BlockSpec((ROW_TILE, d), row),
        out_shape=jax.ShapeDtypeStruct((n, d), F32),
        compiler_params=_params(),
        name="proj_mlp_final" if final else "proj_mlp",
    )(*args)


def _rope_tables(pos, inv_lane):
    lane = lax.broadcasted_iota(jnp.int32, (1, LANES), 1)
    rope = jnp.logical_and(lane >= MLA_NOPE_DIM, lane < MLA_NOPE_DIM + MLA_ROPE_DIM)
    ang = pos * inv_lane
    cos_t = jnp.where(lane < MLA_NOPE_DIM, 1.0, jnp.where(rope, jnp.cos(ang), 0.0))
    sin_t = jnp.where(rope, jnp.sin(ang), 0.0)
    return cos_t, sin_t


def _shared_kv_kernel(x_ref, pos_ref, inv_ref, g_ref, wd_ref, gl_ref, wk_ref, wv_ref,
                      k_ref, v_ref):
    h = _rms(x_ref[...], g_ref[...]).astype(BF16)
    down = _dot(h, wd_ref[...])
    c = _rms(down[:, :MLA_KV_RANK], gl_ref[...]).astype(BF16)
    cos_t, sin_t = _rope_tables(pos_ref[...], inv_ref[...])
    k_rope = (down[:, MLA_KV_RANK:MLA_KV_RANK + LANES] * cos_t
              + down[:, MLA_KV_RANK + LANES:] * sin_t)
    k_nope = _dot(c, wk_ref[...])
    for hd in range(MLA_HEADS):
        sl = slice(hd * LANES, (hd + 1) * LANES)
        k_ref[:, sl] = (k_nope[:, sl] + k_rope).astype(k_ref.dtype)
    v_ref[...] = _dot(c, wv_ref[...]).astype(v_ref.dtype)


def _shared_kv(x, pos, inv_lane, g, wd, gl, wk, wv):
    n, d = x.shape
    row = lambda i: (i, 0)
    return pl.pallas_call(
        _shared_kv_kernel,
        grid=(n // ROW_TILE,),
        in_specs=[pl.BlockSpec((ROW_TILE, d), row),
                  pl.BlockSpec((ROW_TILE, 1), row),
                  _resident((1, LANES)), _resident((1, d)), _resident(wd.shape),
                  _resident((1, MLA_KV_RANK)), _resident(wk.shape), _resident(wv.shape)],
        out_specs=[pl.BlockSpec((ROW_TILE, MLA_HEADS * LANES), row),
                   pl.BlockSpec((ROW_TILE, MLA_HEADS * MLA_V_DIM), row)],
        out_shape=[jax.ShapeDtypeStruct((n, MLA_HEADS * LANES), BF16),
                   jax.ShapeDtypeStruct((n, MLA_HEADS * MLA_V_DIM), BF16)],
        compiler_params=_params(),
        name="mla_shared_kv",
    )(x, pos, inv_lane, g, wd, gl, wk, wv)


def _mla_q_kernel(x_ref, pos_ref, inv_ref, g_ref, wdq_ref, gq_ref, wa_ref, q_ref,
                  *, heads_per_chunk):
    h = _rms(x_ref[...], g_ref[...]).astype(BF16)
    cq = _rms(_dot(h, wdq_ref[...]), gq_ref[...]).astype(BF16)
    cos_t, sin_t = _rope_tables(pos_ref[...], inv_ref[...])
    scale = (MLA_NOPE_DIM + MLA_ROPE_DIM) ** -0.5 * math.log2(math.e)
    cos_t = cos_t * scale
    lane = lax.broadcasted_iota(jnp.int32, (1, LANES), 1)
    low_half = lane < MLA_NOPE_DIM + MLA_ROPE_DIM // 2
    sin_t = jnp.where(low_half, -sin_t, sin_t) * scale
    half = MLA_ROPE_DIM // 2
    width = heads_per_chunk * LANES
    for c in range(MLA_HEADS // heads_per_chunk):
        qa = _dot(cq, wa_ref[:, c * width:(c + 1) * width])
        for j in range(heads_per_chunk):
            qh = qa[:, j * LANES:(j + 1) * LANES]
            partner = jnp.where(low_half, pltpu.roll(qh, LANES - half, 1), pltpu.roll(qh, half, 1))
            q_ref[:, c * width + j * LANES:c * width + (j + 1) * LANES] = (
                qh * cos_t + partner * sin_t).astype(q_ref.dtype)


def _mla_q(x, pos, inv_lane, g, wdq, gq, wa):
    n, d = x.shape
    row = lambda i: (i, 0)
    return pl.pallas_call(
        functools.partial(_mla_q_kernel, heads_per_chunk=4),
        grid=(n // ROW_TILE,),
        in_specs=[pl.BlockSpec((ROW_TILE, d), row),
                  pl.BlockSpec((ROW_TILE, 1), row),
                  _resident((1, LANES)), _resident((1, d)), _resident(wdq.shape),
                  _resident((1, MLA_Q_RANK)), _resident(wa.shape)],
        out_specs=pl.BlockSpec((ROW_TILE, MLA_HEADS * LANES), row),
        out_shape=jax.ShapeDtypeStruct((n, MLA_HEADS * LANES), BF16),
        compiler_params=_params(),
        name="mla_q",
    )(x, pos, inv_lane, g, wdq, gq, wa)


def _mla_attn_kernel(q_ref, k_ref, v_ref, o_ref, *, tile, heads):
    qi = pl.program_id(2)
    pairs = heads // 2
    lane = lax.broadcasted_iota(jnp.int32, (1, LANES), 1)
    halves = (lane < MLA_V_DIM, lane >= MLA_V_DIM)
    row = lax.broadcasted_iota(jnp.int32, (tile, tile), 0)
    col = lax.broadcasted_iota(jnp.int32, (tile, tile), 1)
    allowed = (col // CHUNK) <= (row // CHUNK)

    def masked(x, keep):
        return jnp.where(keep, x, jnp.zeros_like(x))

    def blk(j):
        return slice(j * LANES, (j + 1) * LANES)

    key_row = lax.broadcasted_iota(jnp.int32, (2 * tile, 1), 0)
    ones_cols = jnp.where((key_row < tile) == halves[0], 1.0, 0.0).astype(BF16)

    def sweep(kbs, diag_last, carry):
        ms, accs = carry
        starts = [pl.multiple_of(kb * tile, tile) for kb in kbs]
        ss = [[_dot_nt(q_ref[0, :, blk(h)], k_ref[0, pl.ds(st, tile), blk(h)])
               for st in starts] for h in range(heads)]
        if diag_last:
            for s_head in ss:
                s_head[-1] = jnp.where(allowed, s_head[-1], MASK_VALUE)
        count = len(kbs)
        ps, alphas, new_ms = [], [], []
        for h in range(heads):
            top = functools.reduce(jnp.maximum, ss[h])
            m_new = jnp.maximum(ms[h], jnp.max(top, axis=-1, keepdims=True))
            alphas.append(jnp.exp2(ms[h] - m_new))
            new_ms.append(m_new)
            ps.append([jnp.exp2(s - m_new).astype(BF16) for s in ss[h]])
        new_accs = []
        for j in range(pairs):
            lhs, rhs = [], []
            for n, st in enumerate(starts):
                v = v_ref[0, pl.ds(st, tile), blk(j)]
                vals = jnp.concatenate([masked(v, halves[0]), masked(v, halves[1])], axis=0)
                lhs += [ps[2 * j][n], ps[2 * j + 1][n]]
                rhs.append(jnp.concatenate([vals, ones_cols], axis=1))
            pv = _dot(jnp.concatenate(lhs, axis=1),
                      jnp.concatenate(rhs, axis=0))
            alpha = jnp.where(halves[0], alphas[2 * j], alphas[2 * j + 1])
            new_accs.append(jnp.concatenate([alpha, alpha], axis=1) * accs[j] + pv)
        return tuple(new_ms), tuple(new_accs)

    carry = (tuple(jnp.full((tile, 1), MASK_VALUE, F32) for _ in range(heads)),
             tuple(jnp.zeros((tile, 2 * LANES), F32) for _ in range(pairs)))
    nblk = MLA_BLOCK_TILES
    carry = lax.fori_loop(
        0, qi // nblk, lambda i, cr: sweep([nblk * i + t for t in range(nblk)], False, cr), carry)

    def last_block(rest):
        return lambda cr: sweep([qi - rest + t for t in range(rest)] + [qi], True, cr)

    _, accs = lax.switch(qi % nblk, [last_block(rest) for rest in range(nblk)], carry)
    for j in range(pairs):
        o_ref[0, :, blk(j)] = (accs[j][:, :LANES] / accs[j][:, LANES:]).astype(o_ref.dtype)


def _mla_attention(q, k, v, batch, seq):
    groups = MLA_HEADS // MLA_HEADS_PER_STEP
    t = MLA_TILE
    qk_width = MLA_HEADS_PER_STEP * LANES
    v_width = MLA_HEADS_PER_STEP * MLA_V_DIM
    return pl.pallas_call(
        functools.partial(_mla_attn_kernel, tile=t, heads=MLA_HEADS_PER_STEP),
        grid=(batch, groups, seq // t),
        in_specs=[pl.BlockSpec((1, t, qk_width), lambda b, p, i: (b, i, p)),
                  pl.BlockSpec((1, seq, qk_width), lambda b, p, i: (b, 0, p)),
                  pl.BlockSpec((1, seq, v_width), lambda b, p, i: (b, 0, p))],
        out_specs=pl.BlockSpec((1, t, v_width), lambda b, p, i: (b, i, p)),
        out_shape=jax.ShapeDtypeStruct((batch, seq, MLA_HEADS * MLA_V_DIM), BF16),
        compiler_params=_params(),
        name="mla_attention",
    )(q, k, v)


def _rope_lane_freqs():
    inv_freq = ROPE_THETA ** (-jnp.arange(0, MLA_ROPE_DIM, 2, dtype=F32) / MLA_ROPE_DIM)
    half = MLA_ROPE_DIM // 2
    out = jnp.zeros((1, LANES), F32)
    out = out.at[0, MLA_NOPE_DIM:MLA_NOPE_DIM + half].set(inv_freq)
    out = out.at[0, MLA_NOPE_DIM + half:MLA_NOPE_DIM + 2 * half].set(inv_freq)
    return out


def _layout_w_dkv(w_dkv):
    half = MLA_ROPE_DIM // 2
    d = w_dkv.shape[0]
    t1 = w_dkv[:, MLA_KV_RANK:MLA_KV_RANK + half]
    t2 = w_dkv[:, MLA_KV_RANK + half:]
    pad_lo = jnp.zeros((d, MLA_NOPE_DIM), w_dkv.dtype)
    pad_hi = jnp.zeros((d, LANES - MLA_NOPE_DIM - MLA_ROPE_DIM), w_dkv.dtype)
    blk_a = jnp.concatenate([pad_lo, t1, t2, pad_hi], axis=1)
    blk_b = jnp.concatenate([pad_lo, -t2, t1, pad_hi], axis=1)
    return jnp.concatenate([w_dkv[:, :MLA_KV_RANK], blk_a, blk_b], axis=1).astype(BF16)


def _layout_w_ukv(w_ukv):
    r = w_ukv.shape[0]
    w = w_ukv.reshape(r, MLA_HEADS, MLA_NOPE_DIM + MLA_V_DIM)
    wk = jnp.concatenate(
        [w[:, :, :MLA_NOPE_DIM], jnp.zeros((r, MLA_HEADS, LANES - MLA_NOPE_DIM), w.dtype)],
        axis=2).reshape(r, MLA_HEADS * LANES)
    wv = w[:, :, MLA_NOPE_DIM:].reshape(r, MLA_HEADS * MLA_V_DIM)
    return wk.astype(BF16), wv.astype(BF16)


def _layout_w_uq(w_uq):
    r = w_uq.shape[0]
    w = w_uq.reshape(r, MLA_HEADS, MLA_NOPE_DIM + MLA_ROPE_DIM)
    pad_hi = jnp.zeros((r, MLA_HEADS, LANES - MLA_NOPE_DIM - MLA_ROPE_DIM), w.dtype)
    return jnp.concatenate([w, pad_hi], axis=2).reshape(r, MLA_HEADS * LANES).astype(BF16)


def kernel(x, positions, attn_norm, mlp_norm, sb_w_qkv, sb_w_o, kv_norm, mla_w_dkv,
           mla_kv_lat_norm, mla_w_ukv, mla_w_dq, mla_q_lat_norm, mla_w_uq, mla_w_o,
           mlp_w1, mlp_w2, final_norm):
    batch, seq, d = x.shape
    n = batch * seq
    xs = x.reshape(n, d)
    pos = positions.astype(F32).reshape(n, 1)
    inv_lane = _rope_lane_freqs()

    for layer in range(N_A_LAYERS):
        qkv = _norm_matmul(xs, attn_norm[layer].reshape(1, d), sb_w_qkv[layer].astype(BF16))
        attn = _sb_attention(qkv.reshape(batch, seq, 3 * d), batch, seq)
        xs = _proj_mlp(xs, attn.reshape(n, d), sb_w_o[layer].astype(BF16),
                       mlp_norm[layer].reshape(1, d),
                       mlp_w1[layer].astype(BF16), mlp_w2[layer].astype(BF16))

    wk, wv = _layout_w_ukv(mla_w_ukv)
    k_all, v_all = _shared_kv(xs, pos, inv_lane, kv_norm.reshape(1, d),
                              _layout_w_dkv(mla_w_dkv),
                              mla_kv_lat_norm.reshape(1, MLA_KV_RANK), wk, wv)
    k_all = k_all.reshape(batch, seq, MLA_HEADS * LANES)
    v_all = v_all.reshape(batch, seq, MLA_HEADS * MLA_V_DIM)

    for layer in range(N_A_LAYERS, DEPTH):
        j = layer - N_A_LAYERS
        q = _mla_q(xs, pos, inv_lane, attn_norm[layer].reshape(1, d),
                   mla_w_dq[j].astype(BF16), mla_q_lat_norm[j].reshape(1, MLA_Q_RANK),
                   _layout_w_uq(mla_w_uq[j]))
        attn = _mla_attention(q.reshape(batch, seq, MLA_HEADS * LANES), k_all, v_all, batch, seq)
        xs = _proj_mlp(xs, attn.reshape(n, MLA_HEADS * MLA_V_DIM), mla_w_o[j].astype(BF16),
                       mlp_norm[layer].reshape(1, d),
                       mlp_w1[layer].astype(BF16), mlp_w2[layer].astype(BF16),
                       final_g=final_norm.reshape(1, d) if layer == DEPTH - 1 else None)
    return xs.reshape(batch, seq, d)
```

```python
import functools
import math

import jax
import jax.numpy as jnp
from jax import lax
from jax.experimental import pallas as pl
from jax.experimental.pallas import tpu as pltpu

D_MODEL = 1024
DEPTH = 4
CHUNK = 64
N_A_LAYERS = DEPTH // 2
SB_HEADS = 16
SB_HEAD_DIM = D_MODEL // SB_HEADS
MLA_HEADS = 16
MLA_NOPE_DIM = 64
MLA_ROPE_DIM = 32
MLA_V_DIM = 64
MLA_Q_RANK = 384
MLA_KV_RANK = 256
D_FF = 4 * D_MODEL
ROPE_THETA = 10000.0
NORM_EPS = 1e-6

LANES = 128
ROW_TILE = 512
SB_TILE = 128
SB_HEADS_PER_STEP = 8
SB_DEAD_DROP = 104.0
SB_VOID = 1e30
SB_FIRST_SWEEP = 3
SB_LOOP_SWEEP = 2
MLA_TILE = 512
MLA_HEADS_PER_STEP = 4
MLA_BLOCK_TILES = 4
VMEM_LIMIT = 56 * 1024 * 1024
MASK_VALUE = -1e30
LOG2_E = math.log2(math.e)

F32 = jnp.float32
BF16 = jnp.bfloat16


def _rms(x, g):
    return x * lax.rsqrt(jnp.mean(x * x, axis=-1, keepdims=True) + NORM_EPS) * g


def _dot(a, b):
    return jnp.dot(a, b, preferred_element_type=F32)


def _dot_nt(a, b):
    return lax.dot_general(a, b, (((1,), (1,)), ((), ())), preferred_element_type=F32)


def _params():
    return pltpu.CompilerParams(
        dimension_semantics=None, vmem_limit_bytes=VMEM_LIMIT)


def _resident(shape):
    return pl.BlockSpec(shape, lambda *_: (0,) * len(shape),
                        pipeline_mode=pl.Buffered(1))


def _norm_matmul_kernel(x_ref, g_ref, w_ref, o_ref, *, col_chunk):
    h = _rms(x_ref[...], g_ref[...]).astype(BF16)
    n_out = o_ref.shape[1]
    for c in range(n_out // col_chunk):
        sl = slice(c * col_chunk, (c + 1) * col_chunk)
        o_ref[:, sl] = _dot(h, w_ref[:, sl]).astype(o_ref.dtype)


def _norm_matmul(x, g, w):
    n, d = x.shape
    n_out = w.shape[1]
    return pl.pallas_call(
        functools.partial(_norm_matmul_kernel, col_chunk=1024),
        grid=(n // ROW_TILE,),
        in_specs=[pl.BlockSpec((ROW_TILE, d), lambda i: (i, 0)),
                  _resident((1, d)),
                  _resident((d, n_out))],
        out_specs=pl.BlockSpec((ROW_TILE, n_out), lambda i: (i, 0)),
        out_shape=jax.ShapeDtypeStruct((n, n_out), BF16),
        compiler_params=_params(),
        name="norm_qkv",
    )(x, g, w)


def _sb_attn_kernel(q_ref, k_ref, v_ref, o_ref, *, tile, heads):
    qi = pl.program_id(2)
    pairs = heads // 2
    lane = lax.broadcasted_iota(jnp.int32, (1, LANES), 1)
    halves = (lane < SB_HEAD_DIM, lane >= SB_HEAD_DIM)
    row = lax.broadcasted_iota(jnp.int32, (tile, tile), 0)
    col = lax.broadcasted_iota(jnp.int32, (tile, tile), 1)
    from_key = (row >= col).astype(BF16)
    from_key2 = jnp.concatenate([from_key, from_key], axis=0)
    causal = col < row
    causal2 = jnp.concatenate([causal, causal], axis=0)

    def masked(x, keep):
        return jnp.where(keep, x, jnp.zeros_like(x))

    def blk(j):
        return slice(j * LANES, (j + 1) * LANES)

    q_pairs = []
    for j in range(pairs):
        q = q_ref[0, :, blk(j)] * (SB_HEAD_DIM ** -0.5)
        q_pairs.append(jnp.concatenate([masked(q, halves[0]), masked(q, halves[1])], axis=0))

    def sweep(kb0, count, accs, cs, diag_first):
        kbs = [kb0 - i for i in range(count)]
        starts = [pl.multiple_of(jnp.maximum(kb, 0) * tile, tile) for kb in kbs]
        voids = [jnp.where(kb >= 0, 0.0, SB_VOID) for kb in kbs]
        zs = [[_dot_nt(q_pairs[j], k_ref[0, pl.ds(st, tile), blk(j)])
               for j in range(pairs)] for st in starts]
        splits = []
        for i in range(count):
            for z in zs[i]:
                drop = jnp.maximum(z, 0.0) + jnp.log(1.0 + jnp.exp2(jnp.abs(z) * -LOG2_E))
                if diag_first and i == 0:
                    drop = jnp.where(causal2, drop, 0.0)
                hi = drop.astype(BF16)
                lo = (drop - hi.astype(F32)).astype(BF16)
                splits.append(jnp.concatenate([hi, lo], axis=1))
        totals = _dot(jnp.concatenate(splits, axis=0), from_key2)
        accs, cs = list(accs), list(cs)
        for i in range(count):
            for j in range(pairs):
                base = (i * pairs + j) * 2 * tile
                total = totals[base:base + 2 * tile]
                cs[j] = cs[j] + voids[i]
                w = jnp.exp(zs[i][j] - total - cs[j])
                if diag_first and i == 0:
                    w = jnp.where(causal2, w, 0.0)
                w = w.astype(BF16)
                v = v_ref[0, pl.ds(starts[i], tile), blk(j)]
                accs[j] = accs[j] + _dot(
                    jnp.concatenate([w[:tile], w[tile:]], axis=1),
                    jnp.concatenate([masked(v, halves[0]), masked(v, halves[1])], axis=0))
                cs[j] = cs[j] + total[:, 0:1]
        return tuple(accs), tuple(cs)

    def alive_flag(cs):
        low = functools.reduce(jnp.minimum, cs)
        return (jnp.min(low) < SB_DEAD_DROP).astype(jnp.int32)

    accs = tuple(jnp.zeros((tile, LANES), F32) for _ in range(pairs))
    cs = tuple(jnp.zeros((2 * tile, 1), F32) for _ in range(pairs))
    accs, cs = sweep(qi, SB_FIRST_SWEEP, accs, cs, True)

    def cond(state):
        kb, _, _, alive = state
        return jnp.logical_and(kb >= 0, alive > 0)

    def body(state):
        kb, accs, cs, _ = state
        accs, cs = sweep(kb, SB_LOOP_SWEEP, accs, cs, False)
        return kb - SB_LOOP_SWEEP, accs, cs, alive_flag(cs)

    _, accs, _, _ = lax.while_loop(
        cond, body, (qi - SB_FIRST_SWEEP, accs, cs, alive_flag(cs)))
    for j, acc in enumerate(accs):
        o_ref[0, :, blk(j)] = acc.astype(o_ref.dtype)


def _sb_attention(qkv, batch, seq):
    width = SB_HEADS_PER_STEP * SB_HEAD_DIM
    groups = D_MODEL // width
    t = SB_TILE
    return pl.pallas_call(
        functools.partial(_sb_attn_kernel, tile=t, heads=SB_HEADS_PER_STEP),
        grid=(batch, groups, seq // t),
        in_specs=[pl.BlockSpec((1, t, width), lambda b, p, i: (b, i, p)),
                  pl.BlockSpec((1, seq, width), lambda b, p, i: (b, 0, groups + p)),
                  pl.BlockSpec((1, seq, width), lambda b, p, i: (b, 0, 2 * groups + p))],
        out_specs=pl.BlockSpec((1, t, width), lambda b, p, i: (b, i, p)),
        out_shape=jax.ShapeDtypeStruct((batch, seq, D_MODEL), BF16),
        compiler_params=_params(),
        name="sb_attention",
    )(qkv, qkv, qkv)


def _proj_mlp_kernel(x_ref, a_ref, wo_ref, g_ref, w1_ref, w2_ref, *rest, ff_chunk, final):
    if final:
        gf_ref, o_ref = rest
    else:
        (o_ref,) = rest
    x1 = x_ref[...] + _dot(a_ref[...], wo_ref[...])
    h = _rms(x1, g_ref[...]).astype(BF16)
    acc = x1
    for c in range(D_FF // ff_chunk):
        sl = slice(c * ff_chunk, (c + 1) * ff_chunk)
        u = jnp.maximum(_dot(h, w1_ref[:, sl]), 0.0)
        acc = acc + _dot((u * u).astype(BF16), w2_ref[sl, :])
    if final:
        acc = _rms(acc, gf_ref[...])
    o_ref[...] = acc


def _proj_mlp(x, a, wo, g, w1, w2, final_g=None):
    n, d = x.shape
    final = final_g is not None
    row = lambda i: (i, 0)
    in_specs = [pl.BlockSpec((ROW_TILE, d), row),
                pl.BlockSpec((ROW_TILE, a.shape[1]), row),
                _resident(wo.shape), _resident((1, d)),
                _resident(w1.shape), _resident(w2.shape)]
    args = [x, a, wo, g, w1, w2]
    if final:
        in_specs.append(_resident((1, d)))
        args.append(final_g)
    return pl.pallas_call(
        functools.partial(_proj_mlp_kernel, ff_chunk=1024, final=final),
        grid=(n // ROW_TILE,),
        in_specs=in_specs,
        out_specs=pl.BlockSpec((ROW_TILE, d), row),
        out_shape=jax.ShapeDtypeStruct((n, d), F32),
        compiler_params=_params(),
        name="proj_mlp_final" if final else "proj_mlp",
    )(*args)


def _rope_tables(pos, inv_lane):
    lane = lax.broadcasted_iota(jnp.int32, (1, LANES), 1)
    rope = jnp.logical_and(lane >= MLA_NOPE_DIM, lane < MLA_NOPE_DIM + MLA_ROPE_DIM)
    ang = pos * inv_lane
    cos_t = jnp.where(lane < MLA_NOPE_DIM, 1.0, jnp.where(rope, jnp.cos(ang), 0.0))
    sin_t = jnp.where(rope, jnp.sin(ang), 0.0)
    return cos_t, sin_t


def _shared_kv_kernel(x_ref, pos_ref, inv_ref, g_ref, wd_ref, gl_ref, wk_ref, wv_ref,
                      k_ref, v_ref, cos_ref, sin_ref):
    h = _rms(x_ref[...], g_ref[...]).astype(BF16)
    down = _dot(h, wd_ref[...])
    c = _rms(down[:, :MLA_KV_RANK], gl_ref[...]).astype(BF16)
    cos_t, sin_t = _rope_tables(pos_ref[...], inv_ref[...])
    cos_ref[...] = cos_t
    sin_ref[...] = sin_t
    k_rope = (down[:, MLA_KV_RANK:MLA_KV_RANK + LANES] * cos_t
              + down[:, MLA_KV_RANK + LANES:] * sin_t)
    k_nope = _dot(c, wk_ref[...])
    for hd in range(MLA_HEADS):
        sl = slice(hd * LANES, (hd + 1) * LANES)
        k_ref[:, sl] = (k_nope[:, sl] + k_rope).astype(k_ref.dtype)
    v_ref[...] = _dot(c, wv_ref[...]).astype(v_ref.dtype)


def _shared_kv(x, pos, inv_lane, g, wd, gl, wk, wv):
    n, d = x.shape
    row = lambda i: (i, 0)
    return pl.pallas_call(
        _shared_kv_kernel,
        grid=(n // ROW_TILE,),
        in_specs=[pl.BlockSpec((ROW_TILE, d), row),
                  pl.BlockSpec((ROW_TILE, 1), row),
                  _resident((1, LANES)), _resident((1, d)), _resident(wd.shape),
                  _resident((1, MLA_KV_RANK)), _resident(wk.shape), _resident(wv.shape)],
        out_specs=[pl.BlockSpec((ROW_TILE, MLA_HEADS * LANES), row),
                   pl.BlockSpec((ROW_TILE, MLA_HEADS * MLA_V_DIM), row),
                   pl.BlockSpec((ROW_TILE, LANES), row),
                   pl.BlockSpec((ROW_TILE, LANES), row)],
        out_shape=[jax.ShapeDtypeStruct((n, MLA_HEADS * LANES), BF16),
                   jax.ShapeDtypeStruct((n, MLA_HEADS * MLA_V_DIM), BF16),
                   jax.ShapeDtypeStruct((n, LANES), F32),
                   jax.ShapeDtypeStruct((n, LANES), F32)],
        compiler_params=_params(),
        name="mla_shared_kv",
    )(x, pos, inv_lane, g, wd, gl, wk, wv)


def _mla_q_kernel(x_ref, cos_ref, sin_ref, g_ref, wdq_ref, gq_ref, wa_ref, q_ref,
                  *, heads_per_chunk):
    h = _rms(x_ref[...], g_ref[...]).astype(BF16)
    cq = _rms(_dot(h, wdq_ref[...]), gq_ref[...]).astype(BF16)
    cos_t, sin_t = cos_ref[...], sin_ref[...]
    scale = (MLA_NOPE_DIM + MLA_ROPE_DIM) ** -0.5 * math.log2(math.e)
    cos_t = cos_t * scale
    lane = lax.broadcasted_iota(jnp.int32, (1, LANES), 1)
    low_half = lane < MLA_NOPE_DIM + MLA_ROPE_DIM // 2
    sin_t = jnp.where(low_half, -sin_t, sin_t) * scale
    half = MLA_ROPE_DIM // 2
    width = heads_per_chunk * LANES
    for c in range(MLA_HEADS // heads_per_chunk):
        qa = _dot(cq, wa_ref[:, c * width:(c + 1) * width])
        for j in range(heads_per_chunk):
            qh = qa[:, j * LANES:(j + 1) * LANES]
            partner = pltpu.roll(qh, LANES - half, 1)
            q_ref[:, c * width + j * LANES:c * width + (j + 1) * LANES] = (
                qh * cos_t + partner * sin_t).astype(q_ref.dtype)


def _mla_q(x, cos_t, sin_t, g, wdq, gq, wa):
    n, d = x.shape
    row = lambda i: (i, 0)
    return pl.pallas_call(
        functools.partial(_mla_q_kernel, heads_per_chunk=4),
        grid=(n // ROW_TILE,),
        in_specs=[pl.BlockSpec((ROW_TILE, d), row),
                  pl.BlockSpec((ROW_TILE, LANES), row),
                  pl.BlockSpec((ROW_TILE, LANES), row), _resident((1, d)), _resident(wdq.shape),
                  _resident((1, MLA_Q_RANK)), _resident(wa.shape)],
        out_specs=pl.BlockSpec((ROW_TILE, MLA_HEADS * LANES), row),
        out_shape=jax.ShapeDtypeStruct((n, MLA_HEADS * LANES), BF16),
        compiler_params=_params(),
        name="mla_q",
    )(x, cos_t, sin_t, g, wdq, gq, wa)


def _mla_attn_kernel(q_ref, k_ref, v_ref, o_ref, *, tile, heads):
    qi = pl.program_id(2)
    pairs = heads // 2
    lane = lax.broadcasted_iota(jnp.int32, (1, LANES), 1)
    halves = (lane < MLA_V_DIM, lane >= MLA_V_DIM)
    row = lax.broadcasted_iota(jnp.int32, (tile, tile), 0)
    col = lax.broadcasted_iota(jnp.int32, (tile, tile), 1)
    allowed = (col // CHUNK) <= (row // CHUNK)

    def masked(x, keep):
        return jnp.where(keep, x, jnp.zeros_like(x))

    def blk(j):
        return slice(j * LANES, (j + 1) * LANES)

    key_row = lax.broadcasted_iota(jnp.int32, (2 * tile, 1), 0)
    ones_cols = jnp.where((key_row < tile) == halves[0], 1.0, 0.0).astype(BF16)

    def sweep(kbs, diag_last, carry):
        ms, accs = carry
        starts = [pl.multiple_of(kb * tile, tile) for kb in kbs]
        ss = [[_dot_nt(q_ref[0, :, blk(h)], k_ref[0, pl.ds(st, tile), blk(h)])
               for st in starts] for h in range(heads)]
        if diag_last:
            for s_head in ss:
                s_head[-1] = jnp.where(allowed, s_head[-1], MASK_VALUE)
        count = len(kbs)
        ps, alphas, new_ms = [], [], []
        for h in range(heads):
            top = functools.reduce(jnp.maximum, ss[h])
            m_new = jnp.maximum(ms[h], jnp.max(top, axis=-1, keepdims=True))
            alphas.append(jnp.exp2(ms[h] - m_new))
            new_ms.append(m_new)
            ps.append([jnp.exp2(s - m_new).astype(BF16) for s in ss[h]])
        new_accs = []
        for j in range(pairs):
            lhs, rhs = [], []
            for n, st in enumerate(starts):
                v = v_ref[0, pl.ds(st, tile), blk(j)]
                vals = jnp.concatenate([masked(v, halves[0]), masked(v, halves[1])], axis=0)
                lhs += [ps[2 * j][n], ps[2 * j + 1][n]]
                rhs.append(jnp.concatenate([vals, ones_cols], axis=1))
            pv = _dot(jnp.concatenate(lhs, axis=1),
                      jnp.concatenate(rhs, axis=0))
            alpha = jnp.where(halves[0], alphas[2 * j], alphas[2 * j + 1])
            new_accs.append(jnp.concatenate([alpha, alpha], axis=1) * accs[j] + pv)
        return tuple(new_ms), tuple(new_accs)

    carry = (tuple(jnp.full((tile, 1), MASK_VALUE, F32) for _ in range(heads)),
             tuple(jnp.zeros((tile, 2 * LANES), F32) for _ in range(pairs)))
    nblk = MLA_BLOCK_TILES
    carry = lax.fori_loop(
        0, qi // nblk, lambda i, cr: sweep([nblk * i + t for t in range(nblk)], False, cr), carry)

    def last_block(rest):
        return lambda cr: sweep([qi - rest + t for t in range(rest)] + [qi], True, cr)

    _, accs = lax.switch(qi % nblk, [last_block(rest) for rest in range(nblk)], carry)
    for j in range(pairs):
        o_ref[0, :, blk(j)] = (accs[j][:, :LANES] / accs[j][:, LANES:]).astype(o_ref.dtype)


def _mla_attention(q, k, v, batch, seq):
    groups = MLA_HEADS // MLA_HEADS_PER_STEP
    t = MLA_TILE
    qk_width = MLA_HEADS_PER_STEP * LANES
    v_width = MLA_HEADS_PER_STEP * MLA_V_DIM
    return pl.pallas_call(
        functools.partial(_mla_attn_kernel, tile=t, heads=MLA_HEADS_PER_STEP),
        grid=(batch, groups, seq // t),
        in_specs=[pl.BlockSpec((1, t, qk_width), lambda b, p, i: (b, i, p)),
                  pl.BlockSpec((1, seq, qk_width), lambda b, p, i: (b, 0, p)),
                  pl.BlockSpec((1, seq, v_width), lambda b, p, i: (b, 0, p))],
        out_specs=pl.BlockSpec((1, t, v_width), lambda b, p, i: (b, i, p)),
        out_shape=jax.ShapeDtypeStruct((batch, seq, MLA_HEADS * MLA_V_DIM), BF16),
        compiler_params=_params(),
        name="mla_attention",
    )(q, k, v)


def _rope_lane_freqs():
    inv_freq = ROPE_THETA ** (-jnp.arange(0, MLA_ROPE_DIM, 2, dtype=F32) / MLA_ROPE_DIM)
    half = MLA_ROPE_DIM // 2
    out = jnp.zeros((1, LANES), F32)
    out = out.at[0, MLA_NOPE_DIM:MLA_NOPE_DIM + half].set(inv_freq)
    out = out.at[0, MLA_NOPE_DIM + half:MLA_NOPE_DIM + 2 * half].set(inv_freq)
    return out


def _layout_w_dkv(w_dkv):
    half = MLA_ROPE_DIM // 2
    d = w_dkv.shape[0]
    t1 = w_dkv[:, MLA_KV_RANK:MLA_KV_RANK + half]
    t2 = w_dkv[:, MLA_KV_RANK + half:]
    pad_lo = jnp.zeros((d, MLA_NOPE_DIM), w_dkv.dtype)
    pad_hi = jnp.zeros((d, LANES - MLA_NOPE_DIM - MLA_ROPE_DIM), w_dkv.dtype)
    blk_a = jnp.concatenate([pad_lo, t1, t2, pad_hi], axis=1)
    blk_b = jnp.concatenate([pad_lo, -t2, t1, pad_hi], axis=1)
    return jnp.concatenate([w_dkv[:, :MLA_KV_RANK], blk_a, blk_b], axis=1).astype(BF16)


def _layout_w_ukv(w_ukv):
    r = w_ukv.shape[0]
    w = w_ukv.reshape(r, MLA_HEADS, MLA_NOPE_DIM + MLA_V_DIM)
    wk = jnp.concatenate(
        [w[:, :, :MLA_NOPE_DIM], jnp.zeros((r, MLA_HEADS, LANES - MLA_NOPE_DIM), w.dtype)],
        axis=2).reshape(r, MLA_HEADS * LANES)
    wv = w[:, :, MLA_NOPE_DIM:].reshape(r, MLA_HEADS * MLA_V_DIM)
    return wk.astype(BF16), wv.astype(BF16)


def _layout_w_uq(w_uq):
    r = w_uq.shape[0]
    half = MLA_ROPE_DIM // 2
    w = w_uq.reshape(r, MLA_HEADS, MLA_NOPE_DIM + MLA_ROPE_DIM)
    t1 = w[:, :, MLA_NOPE_DIM:MLA_NOPE_DIM + half]
    pad_hi = jnp.zeros((r, MLA_HEADS, LANES - MLA_NOPE_DIM - MLA_ROPE_DIM - half), w.dtype)
    return jnp.concatenate([w, t1, pad_hi], axis=2).reshape(r, MLA_HEADS * LANES).astype(BF16)


def kernel(x, positions, attn_norm, mlp_norm, sb_w_qkv, sb_w_o, kv_norm, mla_w_dkv,
           mla_kv_lat_norm, mla_w_ukv, mla_w_dq, mla_q_lat_norm, mla_w_uq, mla_w_o,
           mlp_w1, mlp_w2, final_norm):
    batch, seq, d = x.shape
    n = batch * seq
    xs = x.reshape(n, d)
    pos = positions.astype(F32).reshape(n, 1)
    inv_lane = _rope_lane_freqs()

    for layer in range(N_A_LAYERS):
        qkv = _norm_matmul(xs, attn_norm[layer].reshape(1, d), sb_w_qkv[layer].astype(BF16))
        attn = _sb_attention(qkv.reshape(batch, seq, 3 * d), batch, seq)
        xs = _proj_mlp(xs, attn.reshape(n, d), sb_w_o[layer].astype(BF16),
                       mlp_norm[layer].reshape(1, d),
                       mlp_w1[layer].astype(BF16), mlp_w2[layer].astype(BF16))

    wk, wv = _layout_w_ukv(mla_w_ukv)
    k_all, v_all, cos_t, sin_t = _shared_kv(
        xs, pos, inv_lane, kv_norm.reshape(1, d), _layout_w_dkv(mla_w_dkv),
        mla_kv_lat_norm.reshape(1, MLA_KV_RANK), wk, wv)
    k_all = k_all.reshape(batch, seq, MLA_HEADS * LANES)
    v_all = v_all.reshape(batch, seq, MLA_HEADS * MLA_V_DIM)

    for layer in range(N_A_LAYERS, DEPTH):
        j = layer - N_A_LAYERS
        q = _mla_q(xs, cos_t, sin_t, attn_norm[layer].reshape(1, d),
                   mla_w_dq[j].astype(BF16), mla_q_lat_norm[j].reshape(1, MLA_Q_RANK),
                   _layout_w_uq(mla_w_uq[j]))
        attn = _mla_attention(q.reshape(batch, seq, MLA_HEADS * LANES), k_all, v_all, batch, seq)
        xs = _proj_mlp(xs, attn.reshape(n, MLA_HEADS * MLA_V_DIM), mla_w_o[j].astype(BF16),
                       mlp_norm[layer].reshape(1, d),
                       mlp_w1[layer].astype(BF16), mlp_w2[layer].astype(BF16),
                       final_g=final_norm.reshape(1, d) if layer == DEPTH - 1 else None)
    return xs.reshape(batch, seq, d)
```

```python
import functools
import math

import jax
import jax.numpy as jnp
from jax import lax
from jax.experimental import pallas as pl
from jax.experimental.pallas import tpu as pltpu

D_MODEL = 1024
DEPTH = 4
CHUNK = 64
N_A_LAYERS = DEPTH // 2
SB_HEADS = 16
SB_HEAD_DIM = D_MODEL // SB_HEADS
MLA_HEADS = 16
MLA_NOPE_DIM = 64
MLA_ROPE_DIM = 32
MLA_V_DIM = 64
MLA_Q_RANK = 384
MLA_KV_RANK = 256
D_FF = 4 * D_MODEL
ROPE_THETA = 10000.0
NORM_EPS = 1e-6

LANES = 128
ROW_TILE = 1024
SB_TILE = 128
SB_HEADS_PER_STEP = 16
SB_DEAD_DROP = 104.0
SB_VOID = 1e30
SB_FIRST_SWEEP = 3
SB_LOOP_SWEEP = 2
MLA_TILE = 512
MLA_HEADS_PER_STEP = 4
MLA_BLOCK_TILES = 4
VMEM_LIMIT = 56 * 1024 * 1024
MASK_VALUE = -1e30
LOG2_E = math.log2(math.e)

F32 = jnp.float32
BF16 = jnp.bfloat16


def _rms(x, g):
    return x * lax.rsqrt(jnp.mean(x * x, axis=-1, keepdims=True) + NORM_EPS) * g


def _dot(a, b):
    return jnp.dot(a, b, preferred_element_type=F32)


def _dot_nt(a, b):
    return lax.dot_general(a, b, (((1,), (1,)), ((), ())), preferred_element_type=F32)


def _params():
    return pltpu.CompilerParams(
        dimension_semantics=None, vmem_limit_bytes=VMEM_LIMIT)


def _resident(shape):
    return pl.BlockSpec(shape, lambda *_: (0,) * len(shape),
                        pipeline_mode=pl.Buffered(1))


def _norm_matmul_kernel(x_ref, g_ref, w_ref, o_ref, *, col_chunk):
    h = _rms(x_ref[...], g_ref[...]).astype(BF16)
    n_out = o_ref.shape[1]
    for c in range(n_out // col_chunk):
        sl = slice(c * col_chunk, (c + 1) * col_chunk)
        o_ref[:, sl] = _dot(h, w_ref[:, sl]).astype(o_ref.dtype)


def _norm_matmul(x, g, w):
    n, d = x.shape
    n_out = w.shape[1]
    return pl.pallas_call(
        functools.partial(_norm_matmul_kernel, col_chunk=1024),
        grid=(n // ROW_TILE,),
        in_specs=[pl.BlockSpec((ROW_TILE, d), lambda i: (i, 0)),
                  _resident((1, d)),
                  _resident((d, n_out))],
        out_specs=pl.BlockSpec((ROW_TILE, n_out), lambda i: (i, 0)),
        out_shape=jax.ShapeDtypeStruct((n, n_out), BF16),
        compiler_params=_params(),
        name="norm_qkv",
    )(x, g, w)


def _sb_attn_kernel(q_ref, k_ref, v_ref, o_ref, *, tile, heads):
    qi = pl.program_id(2)
    pairs = heads // 2
    lane = lax.broadcasted_iota(jnp.int32, (1, LANES), 1)
    halves = (lane < SB_HEAD_DIM, lane >= SB_HEAD_DIM)
    row = lax.broadcasted_iota(jnp.int32, (tile, tile), 0)
    col = lax.broadcasted_iota(jnp.int32, (tile, tile), 1)
    from_key = (row >= col).astype(BF16)
    from_key2 = jnp.concatenate([from_key, from_key], axis=0)
    causal = col < row
    causal2 = jnp.concatenate([causal, causal], axis=0)

    def masked(x, keep):
        return jnp.where(keep, x, jnp.zeros_like(x))

    def blk(j):
        return slice(j * LANES, (j + 1) * LANES)

    q_pairs = []
    for j in range(pairs):
        q = q_ref[0, :, blk(j)] * (SB_HEAD_DIM ** -0.5)
        q_pairs.append(jnp.concatenate([masked(q, halves[0]), masked(q, halves[1])], axis=0))

    def sweep(kb0, count, accs, cs, diag_first):
        kbs = [kb0 - i for i in range(count)]
        starts = [pl.multiple_of(jnp.maximum(kb, 0) * tile, tile) for kb in kbs]
        voids = [jnp.where(kb >= 0, 0.0, SB_VOID) for kb in kbs]
        zs = [[_dot_nt(q_pairs[j], k_ref[0, pl.ds(st, tile), blk(j)])
               for j in range(pairs)] for st in starts]
        splits = []
        for i in range(count):
            for z in zs[i]:
                drop = jnp.maximum(z, 0.0) + jnp.log(1.0 + jnp.exp2(jnp.abs(z) * -LOG2_E))
                if diag_first and i == 0:
                    drop = jnp.where(causal2, drop, 0.0)
                hi = drop.astype(BF16)
                lo = (drop - hi.astype(F32)).astype(BF16)
                splits.append(jnp.concatenate([hi, lo], axis=1))
        totals = _dot(jnp.concatenate(splits, axis=0), from_key2)
        accs, cs = list(accs), list(cs)
        for i in range(count):
            for j in range(pairs):
                base = (i * pairs + j) * 2 * tile
                total = totals[base:base + 2 * tile]
                cs[j] = cs[j] + voids[i]
                w = jnp.exp(zs[i][j] - total - cs[j])
                if diag_first and i == 0:
                    w = jnp.where(causal2, w, 0.0)
                w = w.astype(BF16)
                v = v_ref[0, pl.ds(starts[i], tile), blk(j)]
                accs[j] = accs[j] + _dot(
                    jnp.concatenate([w[:tile], w[tile:]], axis=1),
                    jnp.concatenate([masked(v, halves[0]), masked(v, halves[1])], axis=0))
                cs[j] = cs[j] + total[:, 0:1]
        return tuple(accs), tuple(cs)

    def alive_flag(cs):
        low = functools.reduce(jnp.minimum, cs)
        return (jnp.min(low) < SB_DEAD_DROP).astype(jnp.int32)

    accs = tuple(jnp.zeros((tile, LANES), F32) for _ in range(pairs))
    cs = tuple(jnp.zeros((2 * tile, 1), F32) for _ in range(pairs))
    accs, cs = sweep(qi, SB_FIRST_SWEEP, accs, cs, True)

    def cond(state):
        kb, _, _, alive = state
        return jnp.logical_and(kb >= 0, alive > 0)

    def body(state):
        kb, accs, cs, _ = state
        accs, cs = sweep(kb, SB_LOOP_SWEEP, accs, cs, False)
        return kb - SB_LOOP_SWEEP, accs, cs, alive_flag(cs)

    _, accs, _, _ = lax.while_loop(
        cond, body, (qi - SB_FIRST_SWEEP, accs, cs, alive_flag(cs)))
    for j, acc in enumerate(accs):
        o_ref[0, :, blk(j)] = acc.astype(o_ref.dtype)


def _sb_attention(qkv, batch, seq):
    width = SB_HEADS_PER_STEP * SB_HEAD_DIM
    groups = D_MODEL // width
    t = SB_TILE
    return pl.pallas_call(
        functools.partial(_sb_attn_kernel, tile=t, heads=SB_HEADS_PER_STEP),
        grid=(batch, groups, seq // t),
        in_specs=[pl.BlockSpec((1, t, width), lambda b, p, i: (b, i, p)),
                  pl.BlockSpec((1, seq, width), lambda b, p, i: (b, 0, groups + p),
                               pipeline_mode=pl.Buffered(1)),
                  pl.BlockSpec((1, seq, width), lambda b, p, i: (b, 0, 2 * groups + p),
                               pipeline_mode=pl.Buffered(1))],
        out_specs=pl.BlockSpec((1, t, width), lambda b, p, i: (b, i, p)),
        out_shape=jax.ShapeDtypeStruct((batch, seq, D_MODEL), BF16),
        compiler_params=_params(),
        name="sb_attention",
    )(qkv, qkv, qkv)


def _proj_mlp_kernel(x_ref, a_ref, wo_ref, g_ref, w1_ref, w2_ref, *rest, ff_chunk, final):
    if final:
        gf_ref, o_ref = rest
    else:
        (o_ref,) = rest
    x1 = x_ref[...] + _dot(a_ref[...], wo_ref[...])
    h = _rms(x1, g_ref[...]).astype(BF16)
    acc = x1
    for c in range(D_FF // ff_chunk):
        sl = slice(c * ff_chunk, (c + 1) * ff_chunk)
        u = jnp.maximum(_dot(h, w1_ref[:, sl]), 0.0)
        acc = acc + _dot((u * u).astype(BF16), w2_ref[sl, :])
    if final:
        acc = _rms(acc, gf_ref[...])
    o_ref[...] = acc


def _proj_mlp(x, a, wo, g, w1, w2, final_g=None):
    n, d = x.shape
    final = final_g is not None
    row = lambda i: (i, 0)
    in_specs = [pl.BlockSpec((ROW_TILE, d), row),
                pl.BlockSpec((ROW_TILE, a.shape[1]), row),
                _resident(wo.shape), _resident((1, d)),
                _resident(w1.shape), _resident(w2.shape)]
    args = [x, a, wo, g, w1, w2]
    if final:
        in_specs.append(_resident((1, d)))
        args.append(final_g)
    return pl.pallas_call(
        functools.partial(_proj_mlp_kernel, ff_chunk=1024, final=final),
        grid=(n // ROW_TILE,),
        in_specs=in_specs,
        out_specs=pl.BlockSpec((ROW_TILE, d), row),
        out_shape=jax.ShapeDtypeStruct((n, d), F32),
        compiler_params=_params(),
        name="proj_mlp_final" if final else "proj_mlp",
    )(*args)


def _rope_tables(pos, inv_lane):
    lane = lax.broadcasted_iota(jnp.int32, (1, LANES), 1)
    rope = jnp.logical_and(lane >= MLA_NOPE_DIM, lane < MLA_NOPE_DIM + MLA_ROPE_DIM)
    ang = pos * inv_lane
    cos_t = jnp.where(lane < MLA_NOPE_DIM, 1.0, jnp.where(rope, jnp.cos(ang), 0.0))
    sin_t = jnp.where(rope, jnp.sin(ang), 0.0)
    return cos_t, sin_t


def _shared_kv_kernel(x_ref, pos_ref, inv_ref, g_ref, wd_ref, gl_ref, wk_ref, wv_ref,
                      k_ref, v_ref, cos_ref, sin_ref):
    h = _rms(x_ref[...], g_ref[...]).astype(BF16)
    down = _dot(h, wd_ref[...])
    c = _rms(down[:, :MLA_KV_RANK], gl_ref[...]).astype(BF16)
    cos_t, sin_t = _rope_tables(pos_ref[...], inv_ref[...])
    cos_ref[...] = cos_t
    sin_ref[...] = sin_t
    k_rope = (down[:, MLA_KV_RANK:MLA_KV_RANK + LANES] * cos_t
              + down[:, MLA_KV_RANK + LANES:] * sin_t)
    k_nope = _dot(c, wk_ref[...])
    for hd in range(MLA_HEADS):
        sl = slice(hd * LANES, (hd + 1) * LANES)
        k_ref[:, sl] = (k_nope[:, sl] + k_rope).astype(k_ref.dtype)
    v_ref[...] = _dot(c, wv_ref[...]).astype(v_ref.dtype)


def _shared_kv(x, pos, inv_lane, g, wd, gl, wk, wv):
    n, d = x.shape
    row = lambda i: (i, 0)
    return pl.pallas_call(
        _shared_kv_kernel,
        grid=(n // ROW_TILE,),
        in_specs=[pl.BlockSpec((ROW_TILE, d), row),
                  pl.BlockSpec((ROW_TILE, 1), row),
                  _resident((1, LANES)), _resident((1, d)), _resident(wd.shape),
                  _resident((1, MLA_KV_RANK)), _resident(wk.shape), _resident(wv.shape)],
        out_specs=[pl.BlockSpec((ROW_TILE, MLA_HEADS * LANES), row),
                   pl.BlockSpec((ROW_TILE, MLA_HEADS * MLA_V_DIM), row),
                   pl.BlockSpec((ROW_TILE, LANES), row),
                   pl.BlockSpec((ROW_TILE, LANES), row)],
        out_shape=[jax.ShapeDtypeStruct((n, MLA_HEADS * LANES), BF16),
                   jax.ShapeDtypeStruct((n, MLA_HEADS * MLA_V_DIM), BF16),
                   jax.ShapeDtypeStruct((n, LANES), F32),
                   jax.ShapeDtypeStruct((n, LANES), F32)],
        compiler_params=_params(),
        name="mla_shared_kv",
    )(x, pos, inv_lane, g, wd, gl, wk, wv)


def _mla_q_kernel(x_ref, cos_ref, sin_ref, g_ref, wdq_ref, gq_ref, wa_ref, q_ref,
                  *, heads_per_chunk):
    h = _rms(x_ref[...], g_ref[...]).astype(BF16)
    cq = _rms(_dot(h, wdq_ref[...]), gq_ref[...]).astype(BF16)
    cos_t, sin_t = cos_ref[...], sin_ref[...]
    scale = (MLA_NOPE_DIM + MLA_ROPE_DIM) ** -0.5 * math.log2(math.e)
    cos_t = cos_t * scale
    lane = lax.broadcasted_iota(jnp.int32, (1, LANES), 1)
    low_half = lane < MLA_NOPE_DIM + MLA_ROPE_DIM // 2
    sin_t = jnp.where(low_half, -sin_t, sin_t) * scale
    half = MLA_ROPE_DIM // 2
    width = heads_per_chunk * LANES
    for c in range(MLA_HEADS // heads_per_chunk):
        qa = _dot(cq, wa_ref[:, c * width:(c + 1) * width])
        for j in range(heads_per_chunk):
            qh = qa[:, j * LANES:(j + 1) * LANES]
            partner = pltpu.roll(qh, LANES - half, 1)
            q_ref[:, c * width + j * LANES:c * width + (j + 1) * LANES] = (
                qh * cos_t + partner * sin_t).astype(q_ref.dtype)


def _mla_q(x, cos_t, sin_t, g, wdq, gq, wa):
    n, d = x.shape
    row = lambda i: (i, 0)
    return pl.pallas_call(
        functools.partial(_mla_q_kernel, heads_per_chunk=4),
        grid=(n // ROW_TILE,),
        in_specs=[pl.BlockSpec((ROW_TILE, d), row),
                  pl.BlockSpec((ROW_TILE, LANES), row),
                  pl.BlockSpec((ROW_TILE, LANES), row), _resident((1, d)), _resident(wdq.shape),
                  _resident((1, MLA_Q_RANK)), _resident(wa.shape)],
        out_specs=pl.BlockSpec((ROW_TILE, MLA_HEADS * LANES), row),
        out_shape=jax.ShapeDtypeStruct((n, MLA_HEADS * LANES), BF16),
        compiler_params=_params(),
        name="mla_q",
    )(x, cos_t, sin_t, g, wdq, gq, wa)


def _mla_attn_kernel(q_ref, k_ref, v_ref, o_ref, *, tile, heads):
    qi = pl.program_id(2)
    pairs = heads // 2
    lane = lax.broadcasted_iota(jnp.int32, (1, LANES), 1)
    halves = (lane < MLA_V_DIM, lane >= MLA_V_DIM)
    row = lax.broadcasted_iota(jnp.int32, (tile, tile), 0)
    col = lax.broadcasted_iota(jnp.int32, (tile, tile), 1)
    allowed = (col // CHUNK) <= (row // CHUNK)

    def masked(x, keep):
        return jnp.where(keep, x, jnp.zeros_like(x))

    def blk(j):
        return slice(j * LANES, (j + 1) * LANES)

    key_row = lax.broadcasted_iota(jnp.int32, (2 * tile, 1), 0)
    ones_cols = jnp.where((key_row < tile) == halves[0], 1.0, 0.0).astype(BF16)

    def sweep(kbs, diag_last, carry):
        ms, accs = carry
        starts = [pl.multiple_of(kb * tile, tile) for kb in kbs]
        ss = [[_dot_nt(q_ref[0, :, blk(h)], k_ref[0, pl.ds(st, tile), blk(h)])
               for st in starts] for h in range(heads)]
        if diag_last:
            for s_head in ss:
                s_head[-1] = jnp.where(allowed, s_head[-1], MASK_VALUE)
        count = len(kbs)
        ps, alphas, new_ms = [], [], []
        for h in range(heads):
            top = functools.reduce(jnp.maximum, ss[h])
            m_new = jnp.maximum(ms[h], jnp.max(top, axis=-1, keepdims=True))
            alphas.append(jnp.exp2(ms[h] - m_new))
            new_ms.append(m_new)
            ps.append([jnp.exp2(s - m_new).astype(BF16) for s in ss[h]])
        new_accs = []
        for j in range(pairs):
            lhs, rhs = [], []
            for n, st in enumerate(starts):
                v = v_ref[0, pl.ds(st, tile), blk(j)]
                vals = jnp.concatenate([masked(v, halves[0]), masked(v, halves[1])], axis=0)
                lhs += [ps[2 * j][n], ps[2 * j + 1][n]]
                rhs.append(jnp.concatenate([vals, ones_cols], axis=1))
            pv = _dot(jnp.concatenate(lhs, axis=1),
                      jnp.concatenate(rhs, axis=0))
            alpha = jnp.where(halves[0], alphas[2 * j], alphas[2 * j + 1])
            new_accs.append(jnp.concatenate([alpha, alpha], axis=1) * accs[j] + pv)
        return tuple(new_ms), tuple(new_accs)

    carry = (tuple(jnp.full((tile, 1), MASK_VALUE, F32) for _ in range(heads)),
             tuple(jnp.zeros((tile, 2 * LANES), F32) for _ in range(pairs)))
    nblk = MLA_BLOCK_TILES
    carry = lax.fori_loop(
        0, qi // nblk, lambda i, cr: sweep([nblk * i + t for t in range(nblk)], False, cr), carry)

    def last_block(rest):
        return lambda cr: sweep([qi - rest + t for t in range(rest)] + [qi], True, cr)

    _, accs = lax.switch(qi % nblk, [last_block(rest) for rest in range(nblk)], carry)
    for j in range(pairs):
        o_ref[0, :, blk(j)] = (accs[j][:, :LANES] / accs[j][:, LANES:]).astype(o_ref.dtype)


def _mla_attention(q, k, v, batch, seq):
    groups = MLA_HEADS // MLA_HEADS_PER_STEP
    t = MLA_TILE
    qk_width = MLA_HEADS_PER_STEP * LANES
    v_width = MLA_HEADS_PER_STEP * MLA_V_DIM
    return pl.pallas_call(
        functools.partial(_mla_attn_kernel, tile=t, heads=MLA_HEADS_PER_STEP),
        grid=(batch, groups, seq // t),
        in_specs=[pl.BlockSpec((1, t, qk_width), lambda b, p, i: (b, i, p)),
                  pl.BlockSpec((1, seq, qk_width), lambda b, p, i: (b, 0, p)),
                  pl.BlockSpec((1, seq, v_width), lambda b, p, i: (b, 0, p))],
        out_specs=pl.BlockSpec((1, t, v_width), lambda b, p, i: (b, i, p)),
        out_shape=jax.ShapeDtypeStruct((batch, seq, MLA_HEADS * MLA_V_DIM), BF16),
        compiler_params=_params(),
        name="mla_attention",
    )(q, k, v)


def _rope_lane_freqs():
    inv_freq = ROPE_THETA ** (-jnp.arange(0, MLA_ROPE_DIM, 2, dtype=F32) / MLA_ROPE_DIM)
    half = MLA_ROPE_DIM // 2
    out = jnp.zeros((1, LANES), F32)
    out = out.at[0, MLA_NOPE_DIM:MLA_NOPE_DIM + half].set(inv_freq)
    out = out.at[0, MLA_NOPE_DIM + half:MLA_NOPE_DIM + 2 * half].set(inv_freq)
    return out


def _layout_w_dkv(w_dkv):
    half = MLA_ROPE_DIM // 2
    d = w_dkv.shape[0]
    t1 = w_dkv[:, MLA_KV_RANK:MLA_KV_RANK + half]
    t2 = w_dkv[:, MLA_KV_RANK + half:]
    pad_lo = jnp.zeros((d, MLA_NOPE_DIM), w_dkv.dtype)
    pad_hi = jnp.zeros((d, LANES - MLA_NOPE_DIM - MLA_ROPE_DIM), w_dkv.dtype)
    blk_a = jnp.concatenate([pad_lo, t1, t2, pad_hi], axis=1)
    blk_b = jnp.concatenate([pad_lo, -t2, t1, pad_hi], axis=1)
    return jnp.concatenate([w_dkv[:, :MLA_KV_RANK], blk_a, blk_b], axis=1).astype(BF16)


def _layout_w_ukv(w_ukv):
    r = w_ukv.shape[0]
    w = w_ukv.reshape(r, MLA_HEADS, MLA_NOPE_DIM + MLA_V_DIM)
    wk = jnp.concatenate(
        [w[:, :, :MLA_NOPE_DIM], jnp.zeros((r, MLA_HEADS, LANES - MLA_NOPE_DIM), w.dtype)],
        axis=2).reshape(r, MLA_HEADS * LANES)
    wv = w[:, :, MLA_NOPE_DIM:].reshape(r, MLA_HEADS * MLA_V_DIM)
    return wk.astype(BF16), wv.astype(BF16)


def _layout_w_uq(w_uq):
    r = w_uq.shape[0]
    half = MLA_ROPE_DIM // 2
    w = w_uq.reshape(r, MLA_HEADS, MLA_NOPE_DIM + MLA_ROPE_DIM)
    t1 = w[:, :, MLA_NOPE_DIM:MLA_NOPE_DIM + half]
    pad_hi = jnp.zeros((r, MLA_HEADS, LANES - MLA_NOPE_DIM - MLA_ROPE_DIM - half), w.dtype)
    return jnp.concatenate([w, t1, pad_hi], axis=2).reshape(r, MLA_HEADS * LANES).astype(BF16)


def kernel(x, positions, attn_norm, mlp_norm, sb_w_qkv, sb_w_o, kv_norm, mla_w_dkv,
           mla_kv_lat_norm, mla_w_ukv, mla_w_dq, mla_q_lat_norm, mla_w_uq, mla_w_o,
           mlp_w1, mlp_w2, final_norm):
    batch, seq, d = x.shape
    n = batch * seq
    xs = x.reshape(n, d)
    pos = positions.astype(F32).reshape(n, 1)
    inv_lane = _rope_lane_freqs()

    for layer in range(N_A_LAYERS):
        qkv = _norm_matmul(xs, attn_norm[layer].reshape(1, d), sb_w_qkv[layer].astype(BF16))
        attn = _sb_attention(qkv.reshape(batch, seq, 3 * d), batch, seq)
        xs = _proj_mlp(xs, attn.reshape(n, d), sb_w_o[layer].astype(BF16),
                       mlp_norm[layer].reshape(1, d),
                       mlp_w1[layer].astype(BF16), mlp_w2[layer].astype(BF16))

    wk, wv = _layout_w_ukv(mla_w_ukv)
    k_all, v_all, cos_t, sin_t = _shared_kv(
        xs, pos, inv_lane, kv_norm.reshape(1, d), _layout_w_dkv(mla_w_dkv),
        mla_kv_lat_norm.reshape(1, MLA_KV_RANK), wk, wv)
    k_all = k_all.reshape(batch, seq, MLA_HEADS * LANES)
    v_all = v_all.reshape(batch, seq, MLA_HEADS * MLA_V_DIM)

    for layer in range(N_A_LAYERS, DEPTH):
        j = layer - N_A_LAYERS
        q = _mla_q(xs, cos_t, sin_t, attn_norm[layer].reshape(1, d),
                   mla_w_dq[j].astype(BF16), mla_q_lat_norm[j].reshape(1, MLA_Q_RANK),
                   _layout_w_uq(mla_w_uq[j]))
        attn = _mla_attention(q.reshape(batch, seq, MLA_HEADS * LANES), k_all, v_all, batch, seq)
        xs = _proj_mlp(xs, attn.reshape(n, MLA_HEADS * MLA_V_DIM), mla_w_o[j].astype(BF16),
                       mlp_norm[layer].reshape(1, d),
                       mlp_w1[layer].astype(BF16), mlp_w2[layer].astype(BF16),
                       final_g=final_norm.reshape(1, d) if layer == DEPTH - 1 else None)
    return xs.reshape(batch, seq, d)
```

```python
import functools
import math

import jax
import jax.numpy as jnp
from jax import lax
from jax.experimental import pallas as pl
from jax.experimental.pallas import tpu as pltpu

D_MODEL = 1024
DEPTH = 4
CHUNK = 64
N_A_LAYERS = DEPTH // 2
SB_HEADS = 16
SB_HEAD_DIM = D_MODEL // SB_HEADS
MLA_HEADS = 16
MLA_NOPE_DIM = 64
MLA_ROPE_DIM = 32
MLA_V_DIM = 64
MLA_Q_RANK = 384
MLA_KV_RANK = 256
D_FF = 4 * D_MODEL
ROPE_THETA = 10000.0
NORM_EPS = 1e-6

LANES = 128
ROW_TILE = 1024
SB_TILE = 128
SB_HEADS_PER_STEP = 16
SB_DEAD_DROP = 104.0
SB_VOID = 1e30
SB_FIRST_SWEEP = 3
SB_LOOP_SWEEP = 2
MLA_TILE = 512
MLA_HEADS_PER_STEP = 4
MLA_SUM_ROWS = 16
MLA_BLOCK_TILES = 4
VMEM_LIMIT = 56 * 1024 * 1024
MASK_VALUE = -1e30
LOG2_E = math.log2(math.e)

F32 = jnp.float32
BF16 = jnp.bfloat16


def _rms(x, g):
    return x * lax.rsqrt(jnp.mean(x * x, axis=-1, keepdims=True) + NORM_EPS) * g


def _dot(a, b):
    return jnp.dot(a, b, preferred_element_type=F32)


def _dot_nt(a, b):
    return lax.dot_general(a, b, (((1,), (1,)), ((), ())), preferred_element_type=F32)


def _params():
    return pltpu.CompilerParams(
        dimension_semantics=None, vmem_limit_bytes=VMEM_LIMIT)


def _resident(shape):
    return pl.BlockSpec(shape, lambda *_: (0,) * len(shape),
                        pipeline_mode=pl.Buffered(1))


def _norm_matmul_kernel(x_ref, g_ref, w_ref, o_ref, *, col_chunk):
    h = _rms(x_ref[...], g_ref[...]).astype(BF16)
    n_out = o_ref.shape[1]
    for c in range(n_out // col_chunk):
        sl = slice(c * col_chunk, (c + 1) * col_chunk)
        o_ref[:, sl] = _dot(h, w_ref[:, sl]).astype(o_ref.dtype)


def _norm_matmul(x, g, w):
    n, d = x.shape
    n_out = w.shape[1]
    return pl.pallas_call(
        functools.partial(_norm_matmul_kernel, col_chunk=1024),
        grid=(n // ROW_TILE,),
        in_specs=[pl.BlockSpec((ROW_TILE, d), lambda i: (i, 0)),
                  _resident((1, d)),
                  _resident((d, n_out))],
        out_specs=pl.BlockSpec((ROW_TILE, n_out), lambda i: (i, 0)),
        out_shape=jax.ShapeDtypeStruct((n, n_out), BF16),
        compiler_params=_params(),
        name="norm_qkv",
    )(x, g, w)


def _sb_attn_kernel(q_ref, k_ref, v_ref, o_ref, *, tile, heads):
    qi = pl.program_id(2)
    pairs = heads // 2
    lane = lax.broadcasted_iota(jnp.int32, (1, LANES), 1)
    halves = (lane < SB_HEAD_DIM, lane >= SB_HEAD_DIM)
    row = lax.broadcasted_iota(jnp.int32, (tile, tile), 0)
    col = lax.broadcasted_iota(jnp.int32, (tile, tile), 1)
    from_key = (row >= col).astype(BF16)
    from_key2 = jnp.concatenate([from_key, from_key], axis=0)
    causal = col < row
    causal2 = jnp.concatenate([causal, causal], axis=0)

    def masked(x, keep):
        return jnp.where(keep, x, jnp.zeros_like(x))

    def blk(j):
        return slice(j * LANES, (j + 1) * LANES)

    q_pairs = []
    for j in range(pairs):
        q = q_ref[0, :, blk(j)] * (SB_HEAD_DIM ** -0.5)
        q_pairs.append(jnp.concatenate([masked(q, halves[0]), masked(q, halves[1])], axis=0))

    def sweep(kb0, count, accs, cs, diag_first):
        kbs = [kb0 - i for i in range(count)]
        starts = [pl.multiple_of(jnp.maximum(kb, 0) * tile, tile) for kb in kbs]
        voids = [jnp.where(kb >= 0, 0.0, SB_VOID) for kb in kbs]
        zs = [[_dot_nt(q_pairs[j], k_ref[0, pl.ds(st, tile), blk(j)])
               for j in range(pairs)] for st in starts]
        splits = []
        for i in range(count):
            for z in zs[i]:
                drop = jnp.maximum(z, 0.0) + jnp.log(1.0 + jnp.exp2(jnp.abs(z) * -LOG2_E))
                if diag_first and i == 0:
                    drop = jnp.where(causal2, drop, 0.0)
                hi = drop.astype(BF16)
                lo = (drop - hi.astype(F32)).astype(BF16)
                splits.append(jnp.concatenate([hi, lo], axis=1))
        totals = _dot(jnp.concatenate(splits, axis=0), from_key2)
        accs, cs = list(accs), list(cs)
        for i in range(count):
            for j in range(pairs):
                base = (i * pairs + j) * 2 * tile
                total = totals[base:base + 2 * tile]
                cs[j] = cs[j] + voids[i]
                w = jnp.exp(zs[i][j] - total - cs[j])
                if diag_first and i == 0:
                    w = jnp.where(causal2, w, 0.0)
                w = w.astype(BF16)
                v = v_ref[0, pl.ds(starts[i], tile), blk(j)]
                accs[j] = accs[j] + _dot(
                    jnp.concatenate([w[:tile], w[tile:]], axis=1),
                    jnp.concatenate([masked(v, halves[0]), masked(v, halves[1])], axis=0))
                cs[j] = cs[j] + total[:, 0:1]
        return tuple(accs), tuple(cs)

    def alive_flag(cs):
        low = functools.reduce(jnp.minimum, cs)
        return (jnp.min(low) < SB_DEAD_DROP).astype(jnp.int32)

    accs = tuple(jnp.zeros((tile, LANES), F32) for _ in range(pairs))
    cs = tuple(jnp.zeros((2 * tile, 1), F32) for _ in range(pairs))
    accs, cs = sweep(qi, SB_FIRST_SWEEP, accs, cs, True)

    def cond(state):
        kb, _, _, alive = state
        return jnp.logical_and(kb >= 0, alive > 0)

    def body(state):
        kb, accs, cs, _ = state
        accs, cs = sweep(kb, SB_LOOP_SWEEP, accs, cs, False)
        return kb - SB_LOOP_SWEEP, accs, cs, alive_flag(cs)

    _, accs, _, _ = lax.while_loop(
        cond, body, (qi - SB_FIRST_SWEEP, accs, cs, alive_flag(cs)))
    for j, acc in enumerate(accs):
        o_ref[0, :, blk(j)] = acc.astype(o_ref.dtype)


def _sb_attention(qkv, batch, seq):
    width = SB_HEADS_PER_STEP * SB_HEAD_DIM
    groups = D_MODEL // width
    t = SB_TILE
    return pl.pallas_call(
        functools.partial(_sb_attn_kernel, tile=t, heads=SB_HEADS_PER_STEP),
        grid=(batch, groups, seq // t),
        in_specs=[pl.BlockSpec((1, t, width), lambda b, p, i: (b, i, p)),
                  pl.BlockSpec((1, seq, width), lambda b, p, i: (b, 0, groups + p),
                               pipeline_mode=pl.Buffered(1)),
                  pl.BlockSpec((1, seq, width), lambda b, p, i: (b, 0, 2 * groups + p),
                               pipeline_mode=pl.Buffered(1))],
        out_specs=pl.BlockSpec((1, t, width), lambda b, p, i: (b, i, p)),
        out_shape=jax.ShapeDtypeStruct((batch, seq, D_MODEL), BF16),
        compiler_params=_params(),
        name="sb_attention",
    )(qkv, qkv, qkv)


def _proj_mlp_kernel(x_ref, a_ref, wo_ref, g_ref, w1_ref, w2_ref, *rest, ff_chunk, final):
    if final:
        gf_ref, o_ref = rest
    else:
        (o_ref,) = rest
    x1 = x_ref[...] + _dot(a_ref[...], wo_ref[...])
    h = _rms(x1, g_ref[...]).astype(BF16)
    acc = x1
    for c in range(D_FF // ff_chunk):
        sl = slice(c * ff_chunk, (c + 1) * ff_chunk)
        u = jnp.maximum(_dot(h, w1_ref[:, sl]), 0.0)
        acc = acc + _dot((u * u).astype(BF16), w2_ref[sl, :])
    if final:
        acc = _rms(acc, gf_ref[...])
    o_ref[...] = acc


def _proj_mlp(x, a, wo, g, w1, w2, final_g=None):
    n, d = x.shape
    final = final_g is not None
    row = lambda i: (i, 0)
    in_specs = [pl.BlockSpec((ROW_TILE, d), row),
                pl.BlockSpec((ROW_TILE, a.shape[1]), row),
                _resident(wo.shape), _resident((1, d)),
                _resident(w1.shape), _resident(w2.shape)]
    args = [x, a, wo, g, w1, w2]
    if final:
        in_specs.append(_resident((1, d)))
        args.append(final_g)
    return pl.pallas_call(
        functools.partial(_proj_mlp_kernel, ff_chunk=1024, final=final),
        grid=(n // ROW_TILE,),
        in_specs=in_specs,
        out_specs=pl.BlockSpec((ROW_TILE, d), row),
        out_shape=jax.ShapeDtypeStruct((n, d), F32),
        compiler_params=_params(),
        name="proj_mlp_final" if final else "proj_mlp",
    )(*args)


def _rope_tables(pos, inv_lane):
    lane = lax.broadcasted_iota(jnp.int32, (1, LANES), 1)
    rope = jnp.logical_and(lane >= MLA_NOPE_DIM, lane < MLA_NOPE_DIM + MLA_ROPE_DIM)
    ang = pos * inv_lane
    cos_t = jnp.where(lane < MLA_NOPE_DIM, 1.0, jnp.where(rope, jnp.cos(ang), 0.0))
    sin_t = jnp.where(rope, jnp.sin(ang), 0.0)
    return cos_t, sin_t


def _shared_kv_kernel(x_ref, pos_ref, inv_ref, g_ref, wd_ref, gl_ref, wk_ref, wv_ref,
                      k_ref, v_ref, cos_ref, sin_ref):
    h = _rms(x_ref[...], g_ref[...]).astype(BF16)
    down = _dot(h, wd_ref[...])
    c = _rms(down[:, :MLA_KV_RANK], gl_ref[...]).astype(BF16)
    cos_t, sin_t = _rope_tables(pos_ref[...], inv_ref[...])
    cos_ref[...] = cos_t
    sin_ref[...] = sin_t
    k_rope = (down[:, MLA_KV_RANK:MLA_KV_RANK + LANES] * cos_t
              + down[:, MLA_KV_RANK + LANES:] * sin_t)
    k_nope = _dot(c, wk_ref[...])
    for hd in range(MLA_HEADS):
        sl = slice(hd * LANES, (hd + 1) * LANES)
        k_ref[:, sl] = (k_nope[:, sl] + k_rope).astype(k_ref.dtype)
    v_ref[...] = _dot(c, wv_ref[...]).astype(v_ref.dtype)


def _shared_kv(x, pos, inv_lane, g, wd, gl, wk, wv):
    n, d = x.shape
    row = lambda i: (i, 0)
    return pl.pallas_call(
        _shared_kv_kernel,
        grid=(n // ROW_TILE,),
        in_specs=[pl.BlockSpec((ROW_TILE, d), row),
                  pl.BlockSpec((ROW_TILE, 1), row),
                  _resident((1, LANES)), _resident((1, d)), _resident(wd.shape),
                  _resident((1, MLA_KV_RANK)), _resident(wk.shape), _resident(wv.shape)],
        out_specs=[pl.BlockSpec((ROW_TILE, MLA_HEADS * LANES), row),
                   pl.BlockSpec((ROW_TILE, MLA_HEADS * MLA_V_DIM), row),
                   pl.BlockSpec((ROW_TILE, LANES), row),
                   pl.BlockSpec((ROW_TILE, LANES), row)],
        out_shape=[jax.ShapeDtypeStruct((n, MLA_HEADS * LANES), BF16),
                   jax.ShapeDtypeStruct((n, MLA_HEADS * MLA_V_DIM), BF16),
                   jax.ShapeDtypeStruct((n, LANES), F32),
                   jax.ShapeDtypeStruct((n, LANES), F32)],
        compiler_params=_params(),
        name="mla_shared_kv",
    )(x, pos, inv_lane, g, wd, gl, wk, wv)


def _mla_q_kernel(x_ref, cos_ref, sin_ref, g_ref, wdq_ref, gq_ref, wa_ref, q_ref,
                  *, heads_per_chunk):
    h = _rms(x_ref[...], g_ref[...]).astype(BF16)
    cq = _rms(_dot(h, wdq_ref[...]), gq_ref[...]).astype(BF16)
    cos_t, sin_t = cos_ref[...], sin_ref[...]
    scale = (MLA_NOPE_DIM + MLA_ROPE_DIM) ** -0.5 * math.log2(math.e)
    cos_t = cos_t * scale
    lane = lax.broadcasted_iota(jnp.int32, (1, LANES), 1)
    low_half = lane < MLA_NOPE_DIM + MLA_ROPE_DIM // 2
    sin_t = jnp.where(low_half, -sin_t, sin_t) * scale
    half = MLA_ROPE_DIM // 2
    width = heads_per_chunk * LANES
    for c in range(MLA_HEADS // heads_per_chunk):
        qa = _dot(cq, wa_ref[:, c * width:(c + 1) * width])
        for j in range(heads_per_chunk):
            qh = qa[:, j * LANES:(j + 1) * LANES]
            partner = pltpu.roll(qh, LANES - half, 1)
            q_ref[:, c * width + j * LANES:c * width + (j + 1) * LANES] = (
                qh * cos_t + partner * sin_t).astype(q_ref.dtype)


def _mla_q(x, cos_t, sin_t, g, wdq, gq, wa):
    n, d = x.shape
    row = lambda i: (i, 0)
    return pl.pallas_call(
        functools.partial(_mla_q_kernel, heads_per_chunk=4),
        grid=(n // ROW_TILE,),
        in_specs=[pl.BlockSpec((ROW_TILE, d), row),
                  pl.BlockSpec((ROW_TILE, LANES), row),
                  pl.BlockSpec((ROW_TILE, LANES), row), _resident((1, d)), _resident(wdq.shape),
                  _resident((1, MLA_Q_RANK)), _resident(wa.shape)],
        out_specs=pl.BlockSpec((ROW_TILE, MLA_HEADS * LANES), row),
        out_shape=jax.ShapeDtypeStruct((n, MLA_HEADS * LANES), BF16),
        compiler_params=_params(),
        name="mla_q",
    )(x, cos_t, sin_t, g, wdq, gq, wa)


def _mla_attn_kernel(q_ref, k_ref, vt_ref, o_ref, *, tile, heads):
    qi = pl.program_id(2)
    key_id = lax.broadcasted_iota(jnp.int32, (tile, tile), 0)
    qry_id = lax.broadcasted_iota(jnp.int32, (tile, tile), 1)
    allowed = (key_id // CHUNK) <= (qry_id // CHUNK)

    def blk(j):
        return slice(j * LANES, (j + 1) * LANES)

    def sweep(kbs, diag_last, carry):
        ms, accs = carry
        count = len(kbs)
        starts = [pl.multiple_of(kb * tile, tile) for kb in kbs]
        ss = [[_dot_nt(k_ref[0, pl.ds(st, tile), blk(h)], q_ref[0, :, blk(h)])
               for st in starts] for h in range(heads)]
        if diag_last:
            for s_head in ss:
                s_head[-1] = jnp.where(allowed, s_head[-1], MASK_VALUE)
        ones_rows = jnp.ones((MLA_SUM_ROWS, count * tile), BF16)
        new_ms, new_accs = [], []
        for h in range(heads):
            top = functools.reduce(jnp.maximum, ss[h])
            m_new = jnp.maximum(ms[h], jnp.max(top, axis=0, keepdims=True))
            alpha = jnp.exp2(ms[h] - m_new)
            p_t = jnp.concatenate([jnp.exp2(s - m_new).astype(BF16) for s in ss[h]], axis=0)
            v_t = jnp.concatenate(
                [vt_ref[0, h * MLA_V_DIM:(h + 1) * MLA_V_DIM, pl.ds(st, tile)] for st in starts],
                axis=1)
            pv_t = _dot(jnp.concatenate([v_t, ones_rows], axis=0), p_t)
            new_ms.append(m_new)
            new_accs.append(alpha * accs[h] + pv_t)
        return tuple(new_ms), tuple(new_accs)

    carry = (tuple(jnp.full((1, tile), MASK_VALUE, F32) for _ in range(heads)),
             tuple(jnp.zeros((MLA_V_DIM + MLA_SUM_ROWS, tile), F32) for _ in range(heads)))
    nblk = MLA_BLOCK_TILES
    carry = lax.fori_loop(
        0, qi // nblk, lambda i, cr: sweep([nblk * i + t for t in range(nblk)], False, cr), carry)

    def last_block(rest):
        return lambda cr: sweep([qi - rest + t for t in range(rest)] + [qi], True, cr)

    _, accs = lax.switch(qi % nblk, [last_block(rest) for rest in range(nblk)], carry)
    for j in range(heads // 2):
        out_t = jnp.concatenate(
            [accs[h][:MLA_V_DIM] / accs[h][MLA_V_DIM:MLA_V_DIM + 1] for h in (2 * j, 2 * j + 1)],
            axis=0)
        o_ref[0, :, blk(j)] = out_t.T.astype(o_ref.dtype)


def _mla_attention(q, k, v_t, batch, seq):
    groups = MLA_HEADS // MLA_HEADS_PER_STEP
    t = MLA_TILE
    qk_width = MLA_HEADS_PER_STEP * LANES
    v_width = MLA_HEADS_PER_STEP * MLA_V_DIM
    return pl.pallas_call(
        functools.partial(_mla_attn_kernel, tile=t, heads=MLA_HEADS_PER_STEP),
        grid=(batch, groups, seq // t),
        in_specs=[pl.BlockSpec((1, t, qk_width), lambda b, p, i: (b, i, p)),
                  pl.BlockSpec((1, seq, qk_width), lambda b, p, i: (b, 0, p)),
                  pl.BlockSpec((1, v_width, seq), lambda b, p, i: (b, p, 0))],
        out_specs=pl.BlockSpec((1, t, v_width), lambda b, p, i: (b, i, p)),
        out_shape=jax.ShapeDtypeStruct((batch, seq, MLA_HEADS * MLA_V_DIM), BF16),
        compiler_params=_params(),
        name="mla_attention",
    )(q, k, v_t)


def _rope_lane_freqs():
    inv_freq = ROPE_THETA ** (-jnp.arange(0, MLA_ROPE_DIM, 2, dtype=F32) / MLA_ROPE_DIM)
    half = MLA_ROPE_DIM // 2
    out = jnp.zeros((1, LANES), F32)
    out = out.at[0, MLA_NOPE_DIM:MLA_NOPE_DIM + half].set(inv_freq)
    out = out.at[0, MLA_NOPE_DIM + half:MLA_NOPE_DIM + 2 * half].set(inv_freq)
    return out


def _layout_w_dkv(w_dkv):
    half = MLA_ROPE_DIM // 2
    d = w_dkv.shape[0]
    t1 = w_dkv[:, MLA_KV_RANK:MLA_KV_RANK + half]
    t2 = w_dkv[:, MLA_KV_RANK + half:]
    pad_lo = jnp.zeros((d, MLA_NOPE_DIM), w_dkv.dtype)
    pad_hi = jnp.zeros((d, LANES - MLA_NOPE_DIM - MLA_ROPE_DIM), w_dkv.dtype)
    blk_a = jnp.concatenate([pad_lo, t1, t2, pad_hi], axis=1)
    blk_b = jnp.concatenate([pad_lo, -t2, t1, pad_hi], axis=1)
    return jnp.concatenate([w_dkv[:, :MLA_KV_RANK], blk_a, blk_b], axis=1).astype(BF16)


def _layout_w_ukv(w_ukv):
    r = w_ukv.shape[0]
    w = w_ukv.reshape(r, MLA_HEADS, MLA_NOPE_DIM + MLA_V_DIM)
    wk = jnp.concatenate(
        [w[:, :, :MLA_NOPE_DIM], jnp.zeros((r, MLA_HEADS, LANES - MLA_NOPE_DIM), w.dtype)],
        axis=2).reshape(r, MLA_HEADS * LANES)
    wv = w[:, :, MLA_NOPE_DIM:].reshape(r, MLA_HEADS * MLA_V_DIM)
    return wk.astype(BF16), wv.astype(BF16)


def _layout_w_uq(w_uq):
    r = w_uq.shape[0]
    half = MLA_ROPE_DIM // 2
    w = w_uq.reshape(r, MLA_HEADS, MLA_NOPE_DIM + MLA_ROPE_DIM)
    t1 = w[:, :, MLA_NOPE_DIM:MLA_NOPE_DIM + half]
    pad_hi = jnp.zeros((r, MLA_HEADS, LANES - MLA_NOPE_DIM - MLA_ROPE_DIM - half), w.dtype)
    return jnp.concatenate([w, t1, pad_hi], axis=2).reshape(r, MLA_HEADS * LANES).astype(BF16)


def kernel(x, positions, attn_norm, mlp_norm, sb_w_qkv, sb_w_o, kv_norm, mla_w_dkv,
           mla_kv_lat_norm, mla_w_ukv, mla_w_dq, mla_q_lat_norm, mla_w_uq, mla_w_o,
           mlp_w1, mlp_w2, final_norm):
    batch, seq, d = x.shape
    n = batch * seq
    xs = x.reshape(n, d)
    pos = positions.astype(F32).reshape(n, 1)
    inv_lane = _rope_lane_freqs()

    for layer in range(N_A_LAYERS):
        qkv = _norm_matmul(xs, attn_norm[layer].reshape(1, d), sb_w_qkv[layer].astype(BF16))
        attn = _sb_attention(qkv.reshape(batch, seq, 3 * d), batch, seq)
        xs = _proj_mlp(xs, attn.reshape(n, d), sb_w_o[layer].astype(BF16),
                       mlp_norm[layer].reshape(1, d),
                       mlp_w1[layer].astype(BF16), mlp_w2[layer].astype(BF16))

    wk, wv = _layout_w_ukv(mla_w_ukv)
    k_all, v_all, cos_t, sin_t = _shared_kv(
        xs, pos, inv_lane, kv_norm.reshape(1, d), _layout_w_dkv(mla_w_dkv),
        mla_kv_lat_norm.reshape(1, MLA_KV_RANK), wk, wv)
    k_all = k_all.reshape(batch, seq, MLA_HEADS * LANES)
    v_all = v_all.reshape(batch, seq, MLA_HEADS * MLA_V_DIM).transpose(0, 2, 1)

    for layer in range(N_A_LAYERS, DEPTH):
        j = layer - N_A_LAYERS
        q = _mla_q(xs, cos_t, sin_t, attn_norm[layer].reshape(1, d),
                   mla_w_dq[j].astype(BF16), mla_q_lat_norm[j].reshape(1, MLA_Q_RANK),
                   _layout_w_uq(mla_w_uq[j]))
        attn = _mla_attention(q.reshape(batch, seq, MLA_HEADS * LANES), k_all, v_all, batch, seq)
        xs = _proj_mlp(xs, attn.reshape(n, MLA_HEADS * MLA_V_DIM), mla_w_o[j].astype(BF16),
                       mlp_norm[layer].reshape(1, d),
                       mlp_w1[layer].astype(BF16), mlp_w2[layer].astype(BF16),
                       final_g=final_norm.reshape(1, d) if layer == DEPTH - 1 else None)
    return xs.reshape(batch, seq, d)
```

```python
import functools
import math

import jax
import jax.numpy as jnp
from jax import lax
from jax.experimental import pallas as pl
from jax.experimental.pallas import tpu as pltpu

D_MODEL = 1024
DEPTH = 4
CHUNK = 64
N_A_LAYERS = DEPTH // 2
SB_HEADS = 16
SB_HEAD_DIM = D_MODEL // SB_HEADS
MLA_HEADS = 16
MLA_NOPE_DIM = 64
MLA_ROPE_DIM = 32
MLA_V_DIM = 64
MLA_Q_RANK = 384
MLA_KV_RANK = 256
D_FF = 4 * D_MODEL
ROPE_THETA = 10000.0
NORM_EPS = 1e-6

LANES = 128
ROW_TILE = 1024
SB_TILE = 128
SB_HEADS_PER_STEP = 16
SB_DEAD_DROP = 104.0
SB_VOID = 1e30
SB_FIRST_SWEEP = 3
SB_LOOP_SWEEP = 2
MLA_Q_TILE = 1024
MLA_K_TILE = 512
MLA_HEADS_PER_STEP = 4
VMEM_LIMIT = 56 * 1024 * 1024
MASK_VALUE = -1e30
LOG2_E = math.log2(math.e)

F32 = jnp.float32
BF16 = jnp.bfloat16


def _rms(x, g):
    return x * lax.rsqrt(jnp.mean(x * x, axis=-1, keepdims=True) + NORM_EPS) * g


def _dot(a, b):
    return jnp.dot(a, b, preferred_element_type=F32)


def _dot_nt(a, b):
    return lax.dot_general(a, b, (((1,), (1,)), ((), ())), preferred_element_type=F32)


def _params():
    return pltpu.CompilerParams(
        dimension_semantics=None, vmem_limit_bytes=VMEM_LIMIT)


def _resident(shape):
    return pl.BlockSpec(shape, lambda *_: (0,) * len(shape),
                        pipeline_mode=pl.Buffered(1))


def _norm_matmul_kernel(x_ref, g_ref, w_ref, o_ref, *, col_chunk):
    h = _rms(x_ref[...], g_ref[...]).astype(BF16)
    n_out = o_ref.shape[1]
    for c in range(n_out // col_chunk):
        sl = slice(c * col_chunk, (c + 1) * col_chunk)
        o_ref[:, sl] = _dot(h, w_ref[:, sl]).astype(o_ref.dtype)


def _norm_matmul(x, g, w):
    n, d = x.shape
    n_out = w.shape[1]
    return pl.pallas_call(
        functools.partial(_norm_matmul_kernel, col_chunk=1024),
        grid=(n // ROW_TILE,),
        in_specs=[pl.BlockSpec((ROW_TILE, d), lambda i: (i, 0)),
                  _resident((1, d)),
                  _resident((d, n_out))],
        out_specs=pl.BlockSpec((ROW_TILE, n_out), lambda i: (i, 0)),
        out_shape=jax.ShapeDtypeStruct((n, n_out), BF16),
        compiler_params=_params(),
        name="norm_qkv",
    )(x, g, w)


def _sb_attn_kernel(q_ref, k_ref, v_ref, o_ref, *, tile, heads):
    qi = pl.program_id(2)
    pairs = heads // 2
    lane = lax.broadcasted_iota(jnp.int32, (1, LANES), 1)
    halves = (lane < SB_HEAD_DIM, lane >= SB_HEAD_DIM)
    row = lax.broadcasted_iota(jnp.int32, (tile, tile), 0)
    col = lax.broadcasted_iota(jnp.int32, (tile, tile), 1)
    from_key = (row >= col).astype(BF16)
    from_key2 = jnp.concatenate([from_key, from_key], axis=0)
    causal = col < row
    causal2 = jnp.concatenate([causal, causal], axis=0)

    def masked(x, keep):
        return jnp.where(keep, x, jnp.zeros_like(x))

    def blk(j):
        return slice(j * LANES, (j + 1) * LANES)

    q_pairs = []
    for j in range(pairs):
        q = q_ref[0, :, blk(j)] * (SB_HEAD_DIM ** -0.5)
        q_pairs.append(jnp.concatenate([masked(q, halves[0]), masked(q, halves[1])], axis=0))

    def sweep(kb0, count, accs, cs, diag_first):
        kbs = [kb0 - i for i in range(count)]
        starts = [pl.multiple_of(jnp.maximum(kb, 0) * tile, tile) for kb in kbs]
        voids = [jnp.where(kb >= 0, 0.0, SB_VOID) for kb in kbs]
        zs = [[_dot_nt(q_pairs[j], k_ref[0, pl.ds(st, tile), blk(j)])
               for j in range(pairs)] for st in starts]
        splits = []
        for i in range(count):
            for z in zs[i]:
                drop = jnp.maximum(z, 0.0) + jnp.log(1.0 + jnp.exp2(jnp.abs(z) * -LOG2_E))
                if diag_first and i == 0:
                    drop = jnp.where(causal2, drop, 0.0)
                hi = drop.astype(BF16)
                lo = (drop - hi.astype(F32)).astype(BF16)
                splits.append(jnp.concatenate([hi, lo], axis=1))
        totals = _dot(jnp.concatenate(splits, axis=0), from_key2)
        accs, cs = list(accs), list(cs)
        for i in range(count):
            for j in range(pairs):
                base = (i * pairs + j) * 2 * tile
                total = totals[base:base + 2 * tile]
                cs[j] = cs[j] + voids[i]
                w = jnp.exp(zs[i][j] - total - cs[j])
                if diag_first and i == 0:
                    w = jnp.where(causal2, w, 0.0)
                w = w.astype(BF16)
                v = v_ref[0, pl.ds(starts[i], tile), blk(j)]
                accs[j] = accs[j] + _dot(
                    jnp.concatenate([w[:tile], w[tile:]], axis=1),
                    jnp.concatenate([masked(v, halves[0]), masked(v, halves[1])], axis=0))
                cs[j] = cs[j] + total[:, 0:1]
        return tuple(accs), tuple(cs)

    def alive_flag(cs):
        low = functools.reduce(jnp.minimum, cs)
        return (jnp.min(low) < SB_DEAD_DROP).astype(jnp.int32)

    accs = tuple(jnp.zeros((tile, LANES), F32) for _ in range(pairs))
    cs = tuple(jnp.zeros((2 * tile, 1), F32) for _ in range(pairs))
    accs, cs = sweep(qi, SB_FIRST_SWEEP, accs, cs, True)

    def cond(state):
        kb, _, _, alive = state
        return jnp.logical_and(kb >= 0, alive > 0)

    def body(state):
        kb, accs, cs, _ = state
        accs, cs = sweep(kb, SB_LOOP_SWEEP, accs, cs, False)
        return kb - SB_LOOP_SWEEP, accs, cs, alive_flag(cs)

    _, accs, _, _ = lax.while_loop(
        cond, body, (qi - SB_FIRST_SWEEP, accs, cs, alive_flag(cs)))
    for j, acc in enumerate(accs):
        o_ref[0, :, blk(j)] = acc.astype(o_ref.dtype)


def _sb_attention(qkv, batch, seq):
    width = SB_HEADS_PER_STEP * SB_HEAD_DIM
    groups = D_MODEL // width
    t = SB_TILE
    return pl.pallas_call(
        functools.partial(_sb_attn_kernel, tile=t, heads=SB_HEADS_PER_STEP),
        grid=(batch, groups, seq // t),
        in_specs=[pl.BlockSpec((1, t, width), lambda b, p, i: (b, i, p)),
                  pl.BlockSpec((1, seq, width), lambda b, p, i: (b, 0, groups + p),
                               pipeline_mode=pl.Buffered(1)),
                  pl.BlockSpec((1, seq, width), lambda b, p, i: (b, 0, 2 * groups + p),
                               pipeline_mode=pl.Buffered(1))],
        out_specs=pl.BlockSpec((1, t, width), lambda b, p, i: (b, i, p)),
        out_shape=jax.ShapeDtypeStruct((batch, seq, D_MODEL), BF16),
        compiler_params=_params(),
        name="sb_attention",
    )(qkv, qkv, qkv)


def _proj_mlp_kernel(x_ref, a_ref, wo_ref, g_ref, w1_ref, w2_ref, *rest, ff_chunk, final):
    if final:
        gf_ref, o_ref = rest
    else:
        (o_ref,) = rest
    x1 = x_ref[...] + _dot(a_ref[...], wo_ref[...])
    h = _rms(x1, g_ref[...]).astype(BF16)
    acc = x1
    for c in range(D_FF // ff_chunk):
        sl = slice(c * ff_chunk, (c + 1) * ff_chunk)
        u = jnp.maximum(_dot(h, w1_ref[:, sl]), 0.0)
        acc = acc + _dot((u * u).astype(BF16), w2_ref[sl, :])
    if final:
        acc = _rms(acc, gf_ref[...])
    o_ref[...] = acc


def _proj_mlp(x, a, wo, g, w1, w2, final_g=None):
    n, d = x.shape
    final = final_g is not None
    row = lambda i: (i, 0)
    in_specs = [pl.BlockSpec((ROW_TILE, d), row),
                pl.BlockSpec((ROW_TILE, a.shape[1]), row),
                _resident(wo.shape), _resident((1, d)),
                _resident(w1.shape), _resident(w2.shape)]
    args = [x, a, wo, g, w1, w2]
    if final:
        in_specs.append(_resident((1, d)))
        args.append(final_g)
    return pl.pallas_call(
        functools.partial(_proj_mlp_kernel, ff_chunk=1024, final=final),
        grid=(n // ROW_TILE,),
        in_specs=in_specs,
        out_specs=pl.BlockSpec((ROW_TILE, d), row),
        out_shape=jax.ShapeDtypeStruct((n, d), F32),
        compiler_params=_params(),
        name="proj_mlp_final" if final else "proj_mlp",
    )(*args)


def _rope_tables(pos, inv_lane):
    lane = lax.broadcasted_iota(jnp.int32, (1, LANES), 1)
    rope = jnp.logical_and(lane >= MLA_NOPE_DIM, lane < MLA_NOPE_DIM + MLA_ROPE_DIM)
    ang = pos * inv_lane
    cos_t = jnp.where(lane < MLA_NOPE_DIM, 1.0, jnp.where(rope, jnp.cos(ang), 0.0))
    sin_t = jnp.where(rope, jnp.sin(ang), 0.0)
    return cos_t, sin_t


def _shared_kv_kernel(x_ref, pos_ref, inv_ref, g_ref, wd_ref, gl_ref, wk_ref, wv_ref,
                      k_ref, v_ref, cos_ref, sin_ref):
    h = _rms(x_ref[...], g_ref[...]).astype(BF16)
    down = _dot(h, wd_ref[...])
    c = _rms(down[:, :MLA_KV_RANK], gl_ref[...]).astype(BF16)
    cos_t, sin_t = _rope_tables(pos_ref[...], inv_ref[...])
    cos_ref[...] = cos_t
    sin_ref[...] = sin_t
    k_rope = (down[:, MLA_KV_RANK:MLA_KV_RANK + LANES] * cos_t
              + down[:, MLA_KV_RANK + LANES:] * sin_t)
    k_nope = _dot(c, wk_ref[...])
    for hd in range(MLA_HEADS):
        sl = slice(hd * LANES, (hd + 1) * LANES)
        k_ref[:, sl] = (k_nope[:, sl] + k_rope).astype(k_ref.dtype)
    v_ref[...] = _dot(c, wv_ref[...]).astype(v_ref.dtype)


def _shared_kv(x, pos, inv_lane, g, wd, gl, wk, wv):
    n, d = x.shape
    row = lambda i: (i, 0)
    return pl.pallas_call(
        _shared_kv_kernel,
        grid=(n // ROW_TILE,),
        in_specs=[pl.BlockSpec((ROW_TILE, d), row),
                  pl.BlockSpec((ROW_TILE, 1), row),
                  _resident((1, LANES)), _resident((1, d)), _resident(wd.shape),
                  _resident((1, MLA_KV_RANK)), _resident(wk.shape), _resident(wv.shape)],
        out_specs=[pl.BlockSpec((ROW_TILE, MLA_HEADS * LANES), row),
                   pl.BlockSpec((ROW_TILE, MLA_HEADS * MLA_V_DIM), row),
                   pl.BlockSpec((ROW_TILE, LANES), row),
                   pl.BlockSpec((ROW_TILE, LANES), row)],
        out_shape=[jax.ShapeDtypeStruct((n, MLA_HEADS * LANES), BF16),
                   jax.ShapeDtypeStruct((n, MLA_HEADS * MLA_V_DIM), BF16),
                   jax.ShapeDtypeStruct((n, LANES), F32),
                   jax.ShapeDtypeStruct((n, LANES), F32)],
        compiler_params=_params(),
        name="mla_shared_kv",
    )(x, pos, inv_lane, g, wd, gl, wk, wv)


def _mla_q_kernel(x_ref, cos_ref, sin_ref, g_ref, wdq_ref, gq_ref, wa_ref, q_ref,
                  *, heads_per_chunk):
    h = _rms(x_ref[...], g_ref[...]).astype(BF16)
    cq = _rms(_dot(h, wdq_ref[...]), gq_ref[...]).astype(BF16)
    cos_t, sin_t = cos_ref[...], sin_ref[...]
    scale = (MLA_NOPE_DIM + MLA_ROPE_DIM) ** -0.5 * math.log2(math.e)
    cos_t = cos_t * scale
    lane = lax.broadcasted_iota(jnp.int32, (1, LANES), 1)
    low_half = lane < MLA_NOPE_DIM + MLA_ROPE_DIM // 2
    sin_t = jnp.where(low_half, -sin_t, sin_t) * scale
    half = MLA_ROPE_DIM // 2
    width = heads_per_chunk * LANES
    for c in range(MLA_HEADS // heads_per_chunk):
        qa = _dot(cq, wa_ref[:, c * width:(c + 1) * width])
        for j in range(heads_per_chunk):
            qh = qa[:, j * LANES:(j + 1) * LANES]
            partner = pltpu.roll(qh, LANES - half, 1)
            q_ref[:, c * width + j * LANES:c * width + (j + 1) * LANES] = (
                qh * cos_t + partner * sin_t).astype(q_ref.dtype)


def _mla_q(x, cos_t, sin_t, g, wdq, gq, wa):
    n, d = x.shape
    row = lambda i: (i, 0)
    return pl.pallas_call(
        functools.partial(_mla_q_kernel, heads_per_chunk=4),
        grid=(n // ROW_TILE,),
        in_specs=[pl.BlockSpec((ROW_TILE, d), row),
                  pl.BlockSpec((ROW_TILE, LANES), row),
                  pl.BlockSpec((ROW_TILE, LANES), row), _resident((1, d)), _resident(wdq.shape),
                  _resident((1, MLA_Q_RANK)), _resident(wa.shape)],
        out_specs=pl.BlockSpec((ROW_TILE, MLA_HEADS * LANES), row),
        out_shape=jax.ShapeDtypeStruct((n, MLA_HEADS * LANES), BF16),
        compiler_params=_params(),
        name="mla_q",
    )(x, cos_t, sin_t, g, wdq, gq, wa)


def _mla_attn_kernel(q_ref, k_ref, v_ref, o_ref, *, tq, tk, heads):
    qi = pl.program_id(2)
    per_q = tq // tk
    pairs = heads // 2
    lane = lax.broadcasted_iota(jnp.int32, (1, LANES), 1)
    halves = (lane < MLA_V_DIM, lane >= MLA_V_DIM)
    q_chunk = lax.broadcasted_iota(jnp.int32, (tq, tk), 0) // CHUNK
    k_chunk = lax.broadcasted_iota(jnp.int32, (tq, tk), 1) // CHUNK

    def masked(x, keep):
        return jnp.where(keep, x, jnp.zeros_like(x))

    def blk(j):
        return slice(j * LANES, (j + 1) * LANES)

    key_row = lax.broadcasted_iota(jnp.int32, (2 * tk, 1), 0)
    ones_cols = jnp.where((key_row < tk) == halves[0], 1.0, 0.0).astype(BF16)

    def sweep(kbs, diagonal, carry):
        ms, accs = carry
        starts = [pl.multiple_of(kb * tk, tk) for kb in kbs]
        ss = [[_dot_nt(q_ref[0, :, blk(h)], k_ref[0, pl.ds(st, tk), blk(h)])
               for st in starts] for h in range(heads)]
        if diagonal:
            for s_head in ss:
                for n in range(len(kbs)):
                    allowed = k_chunk + n * (tk // CHUNK) <= q_chunk
                    s_head[n] = jnp.where(allowed, s_head[n], MASK_VALUE)
        ps, alphas, new_ms = [], [], []
        for h in range(heads):
            top = functools.reduce(jnp.maximum, ss[h])
            m_new = jnp.maximum(ms[h], jnp.max(top, axis=-1, keepdims=True))
            alphas.append(jnp.exp2(ms[h] - m_new))
            new_ms.append(m_new)
            ps.append([jnp.exp2(s - m_new).astype(BF16) for s in ss[h]])
        new_accs = []
        for j in range(pairs):
            lhs, rhs = [], []
            for n, st in enumerate(starts):
                v = v_ref[0, pl.ds(st, tk), blk(j)]
                vals = jnp.concatenate([masked(v, halves[0]), masked(v, halves[1])], axis=0)
                lhs += [ps[2 * j][n], ps[2 * j + 1][n]]
                rhs.append(jnp.concatenate([vals, ones_cols], axis=1))
            pv = _dot(jnp.concatenate(lhs, axis=1),
                      jnp.concatenate(rhs, axis=0))
            alpha = jnp.where(halves[0], alphas[2 * j], alphas[2 * j + 1])
            new_accs.append(jnp.concatenate([alpha, alpha], axis=1) * accs[j] + pv)
        return tuple(new_ms), tuple(new_accs)

    carry = (tuple(jnp.full((tq, 1), MASK_VALUE, F32) for _ in range(heads)),
             tuple(jnp.zeros((tq, 2 * LANES), F32) for _ in range(pairs)))
    carry = lax.fori_loop(
        0, qi, lambda i, cr: sweep([per_q * i + t for t in range(per_q)], False, cr), carry)
    _, accs = sweep([per_q * qi + t for t in range(per_q)], True, carry)
    for j in range(pairs):
        o_ref[0, :, blk(j)] = (accs[j][:, :LANES] / accs[j][:, LANES:]).astype(o_ref.dtype)


def _mla_attention(q, k, v, batch, seq):
    groups = MLA_HEADS // MLA_HEADS_PER_STEP
    qk_width = MLA_HEADS_PER_STEP * LANES
    v_width = MLA_HEADS_PER_STEP * MLA_V_DIM
    return pl.pallas_call(
        functools.partial(_mla_attn_kernel, tq=MLA_Q_TILE, tk=MLA_K_TILE,
                          heads=MLA_HEADS_PER_STEP),
        grid=(batch, groups, seq // MLA_Q_TILE),
        in_specs=[pl.BlockSpec((1, MLA_Q_TILE, qk_width), lambda b, p, i: (b, i, p)),
                  pl.BlockSpec((1, seq, qk_width), lambda b, p, i: (b, 0, p),
                               pipeline_mode=pl.Buffered(1)),
                  pl.BlockSpec((1, seq, v_width), lambda b, p, i: (b, 0, p),
                               pipeline_mode=pl.Buffered(1))],
        out_specs=pl.BlockSpec((1, MLA_Q_TILE, v_width), lambda b, p, i: (b, i, p)),
        out_shape=jax.ShapeDtypeStruct((batch, seq, MLA_HEADS * MLA_V_DIM), BF16),
        compiler_params=_params(),
        name="mla_attention",
    )(q, k, v)


def _rope_lane_freqs():
    inv_freq = ROPE_THETA ** (-jnp.arange(0, MLA_ROPE_DIM, 2, dtype=F32) / MLA_ROPE_DIM)
    half = MLA_ROPE_DIM // 2
    out = jnp.zeros((1, LANES), F32)
    out = out.at[0, MLA_NOPE_DIM:MLA_NOPE_DIM + half].set(inv_freq)
    out = out.at[0, MLA_NOPE_DIM + half:MLA_NOPE_DIM + 2 * half].set(inv_freq)
    return out


def _layout_w_dkv(w_dkv):
    half = MLA_ROPE_DIM // 2
    d = w_dkv.shape[0]
    t1 = w_dkv[:, MLA_KV_RANK:MLA_KV_RANK + half]
    t2 = w_dkv[:, MLA_KV_RANK + half:]
    pad_lo = jnp.zeros((d, MLA_NOPE_DIM), w_dkv.dtype)
    pad_hi = jnp.zeros((d, LANES - MLA_NOPE_DIM - MLA_ROPE_DIM), w_dkv.dtype)
    blk_a = jnp.concatenate([pad_lo, t1, t2, pad_hi], axis=1)
    blk_b = jnp.concatenate([pad_lo, -t2, t1, pad_hi], axis=1)
    return jnp.concatenate([w_dkv[:, :MLA_KV_RANK], blk_a, blk_b], axis=1).astype(BF16)


def _layout_w_ukv(w_ukv):
    r = w_ukv.shape[0]
    w = w_ukv.reshape(r, MLA_HEADS, MLA_NOPE_DIM + MLA_V_DIM)
    wk = jnp.concatenate(
        [w[:, :, :MLA_NOPE_DIM], jnp.zeros((r, MLA_HEADS, LANES - MLA_NOPE_DIM), w.dtype)],
        axis=2).reshape(r, MLA_HEADS * LANES)
    wv = w[:, :, MLA_NOPE_DIM:].reshape(r, MLA_HEADS * MLA_V_DIM)
    return wk.astype(BF16), wv.astype(BF16)


def _layout_w_uq(w_uq):
    r = w_uq.shape[0]
    half = MLA_ROPE_DIM // 2
    w = w_uq.reshape(r, MLA_HEADS, MLA_NOPE_DIM + MLA_ROPE_DIM)
    t1 = w[:, :, MLA_NOPE_DIM:MLA_NOPE_DIM + half]
    pad_hi = jnp.zeros((r, MLA_HEADS, LANES - MLA_NOPE_DIM - MLA_ROPE_DIM - half), w.dtype)
    return jnp.concatenate([w, t1, pad_hi], axis=2).reshape(r, MLA_HEADS * LANES).astype(BF16)


def kernel(x, positions, attn_norm, mlp_norm, sb_w_qkv, sb_w_o, kv_norm, mla_w_dkv,
           mla_kv_lat_norm, mla_w_ukv, mla_w_dq, mla_q_lat_norm, mla_w_uq, mla_w_o,
           mlp_w1, mlp_w2, final_norm):
    batch, seq, d = x.shape
    n = batch * seq
    xs = x.reshape(n, d)
    pos = positions.astype(F32).reshape(n, 1)
    inv_lane = _rope_lane_freqs()

    for layer in range(N_A_LAYERS):
        qkv = _norm_matmul(xs, attn_norm[layer].reshape(1, d), sb_w_qkv[layer].astype(BF16))
        attn = _sb_attention(qkv.reshape(batch, seq, 3 * d), batch, seq)
        xs = _proj_mlp(xs, attn.reshape(n, d), sb_w_o[layer].astype(BF16),
                       mlp_norm[layer].reshape(1, d),
                       mlp_w1[layer].astype(BF16), mlp_w2[layer].astype(BF16))

    wk, wv = _layout_w_ukv(mla_w_ukv)
    k_all, v_all, cos_t, sin_t = _shared_kv(
        xs, pos, inv_lane, kv_norm.reshape(1, d), _layout_w_dkv(mla_w_dkv),
        mla_kv_lat_norm.reshape(1, MLA_KV_RANK), wk, wv)
    k_all = k_all.reshape(batch, seq, MLA_HEADS * LANES)
    v_all = v_all.reshape(batch, seq, MLA_HEADS * MLA_V_DIM)

    for layer in range(N_A_LAYERS, DEPTH):
        j = layer - N_A_LAYERS
        q = _mla_q(xs, cos_t, sin_t, attn_norm[layer].reshape(1, d),
                   mla_w_dq[j].astype(BF16), mla_q_lat_norm[j].reshape(1, MLA_Q_RANK),
                   _layout_w_uq(mla_w_uq[j]))
        attn = _mla_attention(q.reshape(batch, seq, MLA_HEADS * LANES), k_all, v_all, batch, seq)
        xs = _proj_mlp(xs, attn.reshape(n, MLA_HEADS * MLA_V_DIM), mla_w_o[j].astype(BF16),
                       mlp_norm[layer].reshape(1, d),
                       mlp_w1[layer].astype(BF16), mlp_w2[layer].astype(BF16),
                       final_g=final_norm.reshape(1, d) if layer == DEPTH - 1 else None)
    return xs.reshape(batch, seq, d)
```

```python
import functools
import math

import jax
import jax.numpy as jnp
from jax import lax
from jax.experimental import pallas as pl
from jax.experimental.pallas import tpu as pltpu

D_MODEL = 1024
DEPTH = 4
CHUNK = 64
N_A_LAYERS = DEPTH // 2
SB_HEADS = 16
SB_HEAD_DIM = D_MODEL // SB_HEADS
MLA_HEADS = 16
MLA_NOPE_DIM = 64
MLA_ROPE_DIM = 32
MLA_V_DIM = 64
MLA_Q_RANK = 384
MLA_KV_RANK = 256
D_FF = 4 * D_MODEL
ROPE_THETA = 10000.0
NORM_EPS = 1e-6

LANES = 128
ROW_TILE = 1024
SB_TILE = 128
SB_HEADS_PER_STEP = 16
SB_DEAD_DROP = 104.0
SB_LOOP_SWEEP = 2
MLA_TILE = 512
MLA_HEADS_PER_STEP = 4
MLA_BLOCK_TILES = 4
VMEM_LIMIT = 56 * 1024 * 1024
MASK_VALUE = -1e30
LOG2_E = math.log2(math.e)

F32 = jnp.float32
BF16 = jnp.bfloat16


def _rms(x, g):
    return x * lax.rsqrt(jnp.mean(x * x, axis=-1, keepdims=True) + NORM_EPS) * g


def _dot(a, b):
    return jnp.dot(a, b, preferred_element_type=F32)


def _dot_nt(a, b):
    return lax.dot_general(a, b, (((1,), (1,)), ((), ())), preferred_element_type=F32)


def _params():
    return pltpu.CompilerParams(
        dimension_semantics=None, vmem_limit_bytes=VMEM_LIMIT)


def _resident(shape):
    return pl.BlockSpec(shape, lambda *_: (0,) * len(shape),
                        pipeline_mode=pl.Buffered(1))


def _norm_matmul_kernel(x_ref, g_ref, w_ref, o_ref, *, col_chunk):
    h = _rms(x_ref[...], g_ref[...]).astype(BF16)
    n_out = o_ref.shape[1]
    for c in range(n_out // col_chunk):
        sl = slice(c * col_chunk, (c + 1) * col_chunk)
        o_ref[:, sl] = _dot(h, w_ref[:, sl]).astype(o_ref.dtype)


def _norm_matmul(x, g, w):
    n, d = x.shape
    n_out = w.shape[1]
    return pl.pallas_call(
        functools.partial(_norm_matmul_kernel, col_chunk=1024),
        grid=(n // ROW_TILE,),
        in_specs=[pl.BlockSpec((ROW_TILE, d), lambda i: (i, 0)),
                  _resident((1, d)),
                  _resident((d, n_out))],
        out_specs=pl.BlockSpec((ROW_TILE, n_out), lambda i: (i, 0)),
        out_shape=jax.ShapeDtypeStruct((n, n_out), BF16),
        compiler_params=_params(),
        name="norm_qkv",
    )(x, g, w)


def _sb_attn_kernel(q_ref, k_ref, v_ref, o_ref, *, tile, heads):
    qi = pl.program_id(2)
    pairs = heads // 2
    lane = lax.broadcasted_iota(jnp.int32, (1, LANES), 1)
    halves = (lane < SB_HEAD_DIM, lane >= SB_HEAD_DIM)
    row = lax.broadcasted_iota(jnp.int32, (tile, tile), 0)
    col = lax.broadcasted_iota(jnp.int32, (tile, tile), 1)
    from_key = (row >= col).astype(BF16)
    from_key2 = jnp.concatenate([from_key, from_key], axis=0)
    causal = col < row
    causal2 = jnp.concatenate([causal, causal], axis=0)

    def masked(x, keep):
        return jnp.where(keep, x, jnp.zeros_like(x))

    def blk(j):
        return slice(j * LANES, (j + 1) * LANES)

    q_pairs = []
    for j in range(pairs):
        q = q_ref[0, :, blk(j)] * (SB_HEAD_DIM ** -0.5)
        q_pairs.append(jnp.concatenate([masked(q, halves[0]), masked(q, halves[1])], axis=0))

    half = tile // 2

    def take_rows(x, part):
        if part is None:
            return x
        lo = part * half
        return jnp.concatenate([x[lo:lo + half], x[tile + lo:tile + lo + half]], axis=0)

    def put_rows(x, part, new):
        if part is None:
            return new
        lo = part * half
        pieces = [x[:lo], new[:half], x[lo + half:tile + lo], new[half:], x[tile + lo + half:]]
        return jnp.concatenate([p for p in pieces if p.shape[0]], axis=0)

    def sweep(specs, accs, cs, first):
        starts = [pl.multiple_of(jnp.maximum(kb, 0) * tile, tile) for kb, _ in specs]
        zs = [[_dot_nt(take_rows(q_pairs[j], part), k_ref[0, pl.ds(st, tile), blk(j)])
               for j in range(pairs)] for (_, part), st in zip(specs, starts)]
        splits = []
        for i in range(len(specs)):
            for z in zs[i]:
                drop = jnp.maximum(z, 0.0) + jnp.log(1.0 + jnp.exp2(jnp.abs(z) * -LOG2_E))
                if first and i == 0:
                    drop = jnp.where(causal2, drop, 0.0)
                hi = drop.astype(BF16)
                lo = (drop - hi.astype(F32)).astype(BF16)
                splits.append(jnp.concatenate([hi, lo], axis=1))
        totals = _dot(jnp.concatenate(splits, axis=0), from_key2)
        accs, cs = list(accs), list(cs)
        base = 0
        for i, (kb, part) in enumerate(specs):
            for j in range(pairs):
                rows = zs[i][j].shape[0]
                total = totals[base:base + rows]
                base += rows
                arg = zs[i][j] - total
                if first and i == 0:
                    carried = total[:, 0:1]
                else:
                    c = take_rows(cs[j], part)
                    arg = arg - c
                    carried = c + total[:, 0:1]
                w = jnp.exp(arg)
                if first and i == 0:
                    w = jnp.where(causal2, w, 0.0)
                w = w.astype(BF16)
                v = v_ref[0, pl.ds(starts[i], tile), blk(j)]
                v = jnp.where(kb >= 0, v, jnp.zeros_like(v))
                pv = _dot(jnp.concatenate([w[:rows // 2], w[rows // 2:]], axis=1),
                          jnp.concatenate([masked(v, halves[0]), masked(v, halves[1])], axis=0))
                if part is None:
                    accs[j] = accs[j] + pv
                else:
                    lo = part * half
                    pieces = [accs[j][:lo], accs[j][lo:lo + half] + pv, accs[j][lo + half:]]
                    accs[j] = jnp.concatenate([p for p in pieces if p.shape[0]], axis=0)
                cs[j] = put_rows(cs[j], part, carried)
        return tuple(accs), tuple(cs)

    def alive_flag(cs):
        low = functools.reduce(jnp.minimum, cs)
        return (jnp.min(low) < SB_DEAD_DROP).astype(jnp.int32)

    accs = tuple(jnp.zeros((tile, LANES), F32) for _ in range(pairs))
    cs = tuple(jnp.zeros((2 * tile, 1), F32) for _ in range(pairs))
    accs, cs = sweep([(qi, None), (qi - 1, None), (qi - 2, 0)], accs, cs, True)
    accs, cs = lax.cond(
        jnp.logical_and(qi >= 2, alive_flag(cs) > 0),
        lambda a, c: sweep([(qi - 2, 1)], a, c, False), lambda a, c: (a, c), accs, cs)

    def cond(state):
        kb, _, _, alive = state
        return jnp.logical_and(kb >= 0, alive > 0)

    def body(state):
        kb, accs, cs, _ = state
        accs, cs = sweep([(kb - t, None) for t in range(SB_LOOP_SWEEP)], accs, cs, False)
        return kb - SB_LOOP_SWEEP, accs, cs, alive_flag(cs)

    _, accs, _, _ = lax.while_loop(cond, body, (qi - 3, accs, cs, alive_flag(cs)))
    for j, acc in enumerate(accs):
        o_ref[0, :, blk(j)] = acc.astype(o_ref.dtype)


def _sb_attention(qkv, batch, seq):
    width = SB_HEADS_PER_STEP * SB_HEAD_DIM
    groups = D_MODEL // width
    t = SB_TILE
    return pl.pallas_call(
        functools.partial(_sb_attn_kernel, tile=t, heads=SB_HEADS_PER_STEP),
        grid=(batch, groups, seq // t),
        in_specs=[pl.BlockSpec((1, t, width), lambda b, p, i: (b, i, p)),
                  pl.BlockSpec((1, seq, width), lambda b, p, i: (b, 0, groups + p),
                               pipeline_mode=pl.Buffered(1)),
                  pl.BlockSpec((1, seq, width), lambda b, p, i: (b, 0, 2 * groups + p),
                               pipeline_mode=pl.Buffered(1))],
        out_specs=pl.BlockSpec((1, t, width), lambda b, p, i: (b, i, p)),
        out_shape=jax.ShapeDtypeStruct((batch, seq, D_MODEL), BF16),
        compiler_params=_params(),
        name="sb_attention",
    )(qkv, qkv, qkv)


def _proj_mlp_kernel(x_ref, a_ref, wo_ref, g_ref, w1_ref, w2_ref, *rest, ff_chunk, final):
    if final:
        gf_ref, o_ref = rest
    else:
        (o_ref,) = rest
    x1 = x_ref[...] + _dot(a_ref[...], wo_ref[...])
    h = _rms(x1, g_ref[...]).astype(BF16)
    acc = x1
    for c in range(D_FF // ff_chunk):
        sl = slice(c * ff_chunk, (c + 1) * ff_chunk)
        u = jnp.maximum(_dot(h, w1_ref[:, sl]), 0.0)
        acc = acc + _dot((u * u).astype(BF16), w2_ref[sl, :])
    if final:
        acc = _rms(acc, gf_ref[...])
    o_ref[...] = acc


def _proj_mlp(x, a, wo, g, w1, w2, final_g=None):
    n, d = x.shape
    final = final_g is not None
    row = lambda i: (i, 0)
    in_specs = [pl.BlockSpec((ROW_TILE, d), row),
                pl.BlockSpec((ROW_TILE, a.shape[1]), row),
                _resident(wo.shape), _resident((1, d)),
                _resident(w1.shape), _resident(w2.shape)]
    args = [x, a, wo, g, w1, w2]
    if final:
        in_specs.append(_resident((1, d)))
        args.append(final_g)
    return pl.pallas_call(
        functools.partial(_proj_mlp_kernel, ff_chunk=1024, final=final),
        grid=(n // ROW_TILE,),
        in_specs=in_specs,
        out_specs=pl.BlockSpec((ROW_TILE, d), row),
        out_shape=jax.ShapeDtypeStruct((n, d), F32),
        compiler_params=_params(),
        name="proj_mlp_final" if final else "proj_mlp",
    )(*args)


def _rope_tables(pos, inv_lane):
    lane = lax.broadcasted_iota(jnp.int32, (1, LANES), 1)
    rope = jnp.logical_and(lane >= MLA_NOPE_DIM, lane < MLA_NOPE_DIM + MLA_ROPE_DIM)
    ang = pos * inv_lane
    cos_t = jnp.where(lane < MLA_NOPE_DIM, 1.0, jnp.where(rope, jnp.cos(ang), 0.0))
    sin_t = jnp.where(rope, jnp.sin(ang), 0.0)
    return cos_t, sin_t


def _shared_kv_kernel(x_ref, pos_ref, inv_ref, g_ref, wd_ref, gl_ref, wk_ref, wv_ref,
                      k_ref, v_ref, cos_ref, sin_ref):
    h = _rms(x_ref[...], g_ref[...]).astype(BF16)
    down = _dot(h, wd_ref[...])
    c = _rms(down[:, :MLA_KV_RANK], gl_ref[...]).astype(BF16)
    cos_t, sin_t = _rope_tables(pos_ref[...], inv_ref[...])
    cos_ref[...] = cos_t
    sin_ref[...] = sin_t
    k_rope = (down[:, MLA_KV_RANK:MLA_KV_RANK + LANES] * cos_t
              + down[:, MLA_KV_RANK + LANES:] * sin_t)
    k_nope = _dot(c, wk_ref[...])
    for hd in range(MLA_HEADS):
        sl = slice(hd * LANES, (hd + 1) * LANES)
        k_ref[:, sl] = (k_nope[:, sl] + k_rope).astype(k_ref.dtype)
    v_ref[...] = _dot(c, wv_ref[...]).astype(v_ref.dtype)


def _shared_kv(x, pos, inv_lane, g, wd, gl, wk, wv):
    n, d = x.shape
    row = lambda i: (i, 0)
    return pl.pallas_call(
        _shared_kv_kernel,
        grid=(n // ROW_TILE,),
        in_specs=[pl.BlockSpec((ROW_TILE, d), row),
                  pl.BlockSpec((ROW_TILE, 1), row),
                  _resident((1, LANES)), _resident((1, d)), _resident(wd.shape),
                  _resident((1, MLA_KV_RANK)), _resident(wk.shape), _resident(wv.shape)],
        out_specs=[pl.BlockSpec((ROW_TILE, MLA_HEADS * LANES), row),
                   pl.BlockSpec((ROW_TILE, MLA_HEADS * MLA_V_DIM), row),
                   pl.BlockSpec((ROW_TILE, LANES), row),
                   pl.BlockSpec((ROW_TILE, LANES), row)],
        out_shape=[jax.ShapeDtypeStruct((n, MLA_HEADS * LANES), BF16),
                   jax.ShapeDtypeStruct((n, MLA_HEADS * MLA_V_DIM), BF16),
                   jax.ShapeDtypeStruct((n, LANES), F32),
                   jax.ShapeDtypeStruct((n, LANES), F32)],
        compiler_params=_params(),
        name="mla_shared_kv",
    )(x, pos, inv_lane, g, wd, gl, wk, wv)


def _mla_q_kernel(x_ref, cos_ref, sin_ref, g_ref, wdq_ref, gq_ref, wa_ref, q_ref,
                  *, heads_per_chunk):
    h = _rms(x_ref[...], g_ref[...]).astype(BF16)
    cq = _rms(_dot(h, wdq_ref[...]), gq_ref[...]).astype(BF16)
    cos_t, sin_t = cos_ref[...], sin_ref[...]
    scale = (MLA_NOPE_DIM + MLA_ROPE_DIM) ** -0.5 * math.log2(math.e)
    cos_t = cos_t * scale
    lane = lax.broadcasted_iota(jnp.int32, (1, LANES), 1)
    low_half = lane < MLA_NOPE_DIM + MLA_ROPE_DIM // 2
    sin_t = jnp.where(low_half, -sin_t, sin_t) * scale
    half = MLA_ROPE_DIM // 2
    width = heads_per_chunk * LANES
    for c in range(MLA_HEADS // heads_per_chunk):
        qa = _dot(cq, wa_ref[:, c * width:(c + 1) * width])
        for j in range(heads_per_chunk):
            qh = qa[:, j * LANES:(j + 1) * LANES]
            partner = pltpu.roll(qh, LANES - half, 1)
            q_ref[:, c * width + j * LANES:c * width + (j + 1) * LANES] = (
                qh * cos_t + partner * sin_t).astype(q_ref.dtype)


def _mla_q(x, cos_t, sin_t, g, wdq, gq, wa):
    n, d = x.shape
    row = lambda i: (i, 0)
    return pl.pallas_call(
        functools.partial(_mla_q_kernel, heads_per_chunk=4),
        grid=(n // ROW_TILE,),
        in_specs=[pl.BlockSpec((ROW_TILE, d), row),
                  pl.BlockSpec((ROW_TILE, LANES), row),
                  pl.BlockSpec((ROW_TILE, LANES), row), _resident((1, d)), _resident(wdq.shape),
                  _resident((1, MLA_Q_RANK)), _resident(wa.shape)],
        out_specs=pl.BlockSpec((ROW_TILE, MLA_HEADS * LANES), row),
        out_shape=jax.ShapeDtypeStruct((n, MLA_HEADS * LANES), BF16),
        compiler_params=_params(),
        name="mla_q",
    )(x, cos_t, sin_t, g, wdq, gq, wa)


def _mla_attn_kernel(q_ref, k_ref, v_ref, o_ref, *, tile, heads):
    qi = pl.program_id(2)
    pairs = heads // 2
    lane = lax.broadcasted_iota(jnp.int32, (1, LANES), 1)
    halves = (lane < MLA_V_DIM, lane >= MLA_V_DIM)
    row = lax.broadcasted_iota(jnp.int32, (tile, tile), 0)
    col = lax.broadcasted_iota(jnp.int32, (tile, tile), 1)
    allowed = (col // CHUNK) <= (row // CHUNK)

    def masked(x, keep):
        return jnp.where(keep, x, jnp.zeros_like(x))

    def blk(j):
        return slice(j * LANES, (j + 1) * LANES)

    key_row = lax.broadcasted_iota(jnp.int32, (2 * tile, 1), 0)
    ones_cols = jnp.where((key_row < tile) == halves[0], 1.0, 0.0).astype(BF16)

    def sweep(kbs, diag_last, carry):
        ms, accs = carry
        starts = [pl.multiple_of(kb * tile, tile) for kb in kbs]
        ss = [[_dot_nt(q_ref[0, :, blk(h)], k_ref[0, pl.ds(st, tile), blk(h)])
               for st in starts] for h in range(heads)]
        if diag_last:
            for s_head in ss:
                s_head[-1] = jnp.where(allowed, s_head[-1], MASK_VALUE)
        count = len(kbs)
        ps, alphas, new_ms = [], [], []
        for h in range(heads):
            top = functools.reduce(jnp.maximum, ss[h])
            m_new = jnp.maximum(ms[h], jnp.max(top, axis=-1, keepdims=True))
            alphas.append(jnp.exp2(ms[h] - m_new))
            new_ms.append(m_new)
            ps.append([jnp.exp2(s - m_new).astype(BF16) for s in ss[h]])
        new_accs = []
        for j in range(pairs):
            lhs, rhs = [], []
            for n, st in enumerate(starts):
                v = v_ref[0, pl.ds(st, tile), blk(j)]
                vals = jnp.concatenate([masked(v, halves[0]), masked(v, halves[1])], axis=0)
                lhs += [ps[2 * j][n], ps[2 * j + 1][n]]
                rhs.append(jnp.concatenate([vals, ones_cols], axis=1))
            pv = _dot(jnp.concatenate(lhs, axis=1),
                      jnp.concatenate(rhs, axis=0))
            alpha = jnp.where(halves[0], alphas[2 * j], alphas[2 * j + 1])
            new_accs.append(jnp.concatenate([alpha, alpha], axis=1) * accs[j] + pv)
        return tuple(new_ms), tuple(new_accs)

    carry = (tuple(jnp.full((tile, 1), MASK_VALUE, F32) for _ in range(heads)),
             tuple(jnp.zeros((tile, 2 * LANES), F32) for _ in range(pairs)))
    nblk = MLA_BLOCK_TILES
    carry = lax.fori_loop(
        0, qi // nblk, lambda i, cr: sweep([nblk * i + t for t in range(nblk)], False, cr), carry)

    def last_block(rest):
        return lambda cr: sweep([qi - rest + t for t in range(rest)] + [qi], True, cr)

    _, accs = lax.switch(qi % nblk, [last_block(rest) for rest in range(nblk)], carry)
    for j in range(pairs):
        o_ref[0, :, blk(j)] = (accs[j][:, :LANES] / accs[j][:, LANES:]).astype(o_ref.dtype)


def _mla_attention(q, k, v, batch, seq):
    groups = MLA_HEADS // MLA_HEADS_PER_STEP
    t = MLA_TILE
    qk_width = MLA_HEADS_PER_STEP * LANES
    v_width = MLA_HEADS_PER_STEP * MLA_V_DIM
    return pl.pallas_call(
        functools.partial(_mla_attn_kernel, tile=t, heads=MLA_HEADS_PER_STEP),
        grid=(batch, groups, seq // t),
        in_specs=[pl.BlockSpec((1, t, qk_width), lambda b, p, i: (b, i, p)),
                  pl.BlockSpec((1, seq, qk_width), lambda b, p, i: (b, 0, p)),
                  pl.BlockSpec((1, seq, v_width), lambda b, p, i: (b, 0, p))],
        out_specs=pl.BlockSpec((1, t, v_width), lambda b, p, i: (b, i, p)),
        out_shape=jax.ShapeDtypeStruct((batch, seq, MLA_HEADS * MLA_V_DIM), BF16),
        compiler_params=_params(),
        name="mla_attention",
    )(q, k, v)


def _rope_lane_freqs():
    inv_freq = ROPE_THETA ** (-jnp.arange(0, MLA_ROPE_DIM, 2, dtype=F32) / MLA_ROPE_DIM)
    half = MLA_ROPE_DIM // 2
    out = jnp.zeros((1, LANES), F32)
    out = out.at[0, MLA_NOPE_DIM:MLA_NOPE_DIM + half].set(inv_freq)
    out = out.at[0, MLA_NOPE_DIM + half:MLA_NOPE_DIM + 2 * half].set(inv_freq)
    return out


def _layout_w_dkv(w_dkv):
    half = MLA_ROPE_DIM // 2
    d = w_dkv.shape[0]
    t1 = w_dkv[:, MLA_KV_RANK:MLA_KV_RANK + half]
    t2 = w_dkv[:, MLA_KV_RANK + half:]
    pad_lo = jnp.zeros((d, MLA_NOPE_DIM), w_dkv.dtype)
    pad_hi = jnp.zeros((d, LANES - MLA_NOPE_DIM - MLA_ROPE_DIM), w_dkv.dtype)
    blk_a = jnp.concatenate([pad_lo, t1, t2, pad_hi], axis=1)
    blk_b = jnp.concatenate([pad_lo, -t2, t1, pad_hi], axis=1)
    return jnp.concatenate([w_dkv[:, :MLA_KV_RANK], blk_a, blk_b], axis=1).astype(BF16)


def _layout_w_ukv(w_ukv):
    r = w_ukv.shape[0]
    w = w_ukv.reshape(r, MLA_HEADS, MLA_NOPE_DIM + MLA_V_DIM)
    wk = jnp.concatenate(
        [w[:, :, :MLA_NOPE_DIM], jnp.zeros((r, MLA_HEADS, LANES - MLA_NOPE_DIM), w.dtype)],
        axis=2).reshape(r, MLA_HEADS * LANES)
    wv = w[:, :, MLA_NOPE_DIM:].reshape(r, MLA_HEADS * MLA_V_DIM)
    return wk.astype(BF16), wv.astype(BF16)


def _layout_w_uq(w_uq):
    r = w_uq.shape[0]
    half = MLA_ROPE_DIM // 2
    w = w_uq.reshape(r, MLA_HEADS, MLA_NOPE_DIM + MLA_ROPE_DIM)
    t1 = w[:, :, MLA_NOPE_DIM:MLA_NOPE_DIM + half]
    pad_hi = jnp.zeros((r, MLA_HEADS, LANES - MLA_NOPE_DIM - MLA_ROPE_DIM - half), w.dtype)
    return jnp.concatenate([w, t1, pad_hi], axis=2).reshape(r, MLA_HEADS * LANES).astype(BF16)


def kernel(x, positions, attn_norm, mlp_norm, sb_w_qkv, sb_w_o, kv_norm, mla_w_dkv,
           mla_kv_lat_norm, mla_w_ukv, mla_w_dq, mla_q_lat_norm, mla_w_uq, mla_w_o,
           mlp_w1, mlp_w2, final_norm):
    batch, seq, d = x.shape
    n = batch * seq
    xs = x.reshape(n, d)
    pos = positions.astype(F32).reshape(n, 1)
    inv_lane = _rope_lane_freqs()

    for layer in range(N_A_LAYERS):
        qkv = _norm_matmul(xs, attn_norm[layer].reshape(1, d), sb_w_qkv[layer].astype(BF16))
        attn = _sb_attention(qkv.reshape(batch, seq, 3 * d), batch, seq)
        xs = _proj_mlp(xs, attn.reshape(n, d), sb_w_o[layer].astype(BF16),
                       mlp_norm[layer].reshape(1, d),
                       mlp_w1[layer].astype(BF16), mlp_w2[layer].astype(BF16))

    wk, wv = _layout_w_ukv(mla_w_ukv)
    k_all, v_all, cos_t, sin_t = _shared_kv(
        xs, pos, inv_lane, kv_norm.reshape(1, d), _layout_w_dkv(mla_w_dkv),
        mla_kv_lat_norm.reshape(1, MLA_KV_RANK), wk, wv)
    k_all = k_all.reshape(batch, seq, MLA_HEADS * LANES)
    v_all = v_all.reshape(batch, seq, MLA_HEADS * MLA_V_DIM)

    for layer in range(N_A_LAYERS, DEPTH):
        j = layer - N_A_LAYERS
        q = _mla_q(xs, cos_t, sin_t, attn_norm[layer].reshape(1, d),
                   mla_w_dq[j].astype(BF16), mla_q_lat_norm[j].reshape(1, MLA_Q_RANK),
                   _layout_w_uq(mla_w_uq[j]))
        attn = _mla_attention(q.reshape(batch, seq, MLA_HEADS * LANES), k_all, v_all, batch, seq)
        xs = _proj_mlp(xs, attn.reshape(n, MLA_HEADS * MLA_V_DIM), mla_w_o[j].astype(BF16),
                       mlp_norm[layer].reshape(1, d),
                       mlp_w1[layer].astype(BF16), mlp_w2[layer].astype(BF16),
                       final_g=final_norm.reshape(1, d) if layer == DEPTH - 1 else None)
    return xs.reshape(batch, seq, d)
```

```python
import functools
import math

import jax
import jax.numpy as jnp
from jax import lax
from jax.experimental import pallas as pl
from jax.experimental.pallas import tpu as pltpu

D_MODEL = 1024
DEPTH = 4
CHUNK = 64
N_A_LAYERS = DEPTH // 2
SB_HEADS = 16
SB_HEAD_DIM = D_MODEL // SB_HEADS
MLA_HEADS = 16
MLA_NOPE_DIM = 64
MLA_ROPE_DIM = 32
MLA_V_DIM = 64
MLA_Q_RANK = 384
MLA_KV_RANK = 256
D_FF = 4 * D_MODEL
ROPE_THETA = 10000.0
NORM_EPS = 1e-6

LANES = 128
ROW_TILE = 1024
SB_TILE = 128
SB_HEADS_PER_STEP = 16
SB_DEAD_DROP = 104.0
SB_VOID = 1e30
SB_FIRST_SWEEP = 3
SB_LOOP_SWEEP = 2
MLA_TILE = 512
MLA_HEADS_PER_STEP = 4
MLA_BLOCK_TILES = 4
VMEM_LIMIT = 56 * 1024 * 1024
MASK_VALUE = -1e30
LOG2_E = math.log2(math.e)

F32 = jnp.float32
BF16 = jnp.bfloat16


def _rms(x, g):
    return x * lax.rsqrt(jnp.mean(x * x, axis=-1, keepdims=True) + NORM_EPS) * g


def _dot(a, b):
    return jnp.dot(a, b, preferred_element_type=F32)


def _dot_nt(a, b):
    return lax.dot_general(a, b, (((1,), (1,)), ((), ())), preferred_element_type=F32)


def _params():
    return pltpu.CompilerParams(
        dimension_semantics=None, vmem_limit_bytes=VMEM_LIMIT)


def _resident(shape):
    return pl.BlockSpec(shape, lambda *_: (0,) * len(shape),
                        pipeline_mode=pl.Buffered(1))


def _norm_matmul_kernel(x_ref, g_ref, w_ref, o_ref, *, col_chunk):
    h = _rms(x_ref[...], g_ref[...]).astype(BF16)
    n_out = o_ref.shape[1]
    for c in range(n_out // col_chunk):
        sl = slice(c * col_chunk, (c + 1) * col_chunk)
        o_ref[:, sl] = _dot(h, w_ref[:, sl]).astype(o_ref.dtype)


def _norm_matmul(x, g, w):
    n, d = x.shape
    n_out = w.shape[1]
    return pl.pallas_call(
        functools.partial(_norm_matmul_kernel, col_chunk=1024),
        grid=(n // ROW_TILE,),
        in_specs=[pl.BlockSpec((ROW_TILE, d), lambda i: (i, 0)),
                  _resident((1, d)),
                  _resident((d, n_out))],
        out_specs=pl.BlockSpec((ROW_TILE, n_out), lambda i: (i, 0)),
        out_shape=jax.ShapeDtypeStruct((n, n_out), BF16),
        compiler_params=_params(),
        name="norm_qkv",
    )(x, g, w)


def _sb_attn_kernel(q_ref, k_ref, v_ref, o_ref, *, tile, heads):
    qi = pl.program_id(2)
    pairs = heads // 2
    lane = lax.broadcasted_iota(jnp.int32, (1, LANES), 1)
    halves = (lane < SB_HEAD_DIM, lane >= SB_HEAD_DIM)
    row = lax.broadcasted_iota(jnp.int32, (tile, tile), 0)
    col = lax.broadcasted_iota(jnp.int32, (tile, tile), 1)
    from_key = (row >= col).astype(BF16)
    causal = col < row
    causal2 = jnp.concatenate([causal, causal], axis=0)

    def masked(x, keep):
        return jnp.where(keep, x, jnp.zeros_like(x))

    def blk(j):
        return slice(j * LANES, (j + 1) * LANES)

    q_pairs = []
    for j in range(pairs):
        q = q_ref[0, :, blk(j)] * (SB_HEAD_DIM ** -0.5)
        q_pairs.append(jnp.concatenate([masked(q, halves[0]), masked(q, halves[1])], axis=0))

    def sweep(kb0, count, accs, cs, diag_first):
        kbs = [kb0 - i for i in range(count)]
        starts = [pl.multiple_of(jnp.maximum(kb, 0) * tile, tile) for kb in kbs]
        voids = [jnp.where(kb >= 0, 0.0, SB_VOID) for kb in kbs]
        zs = [[_dot_nt(q_pairs[j], k_ref[0, pl.ds(st, tile), blk(j)])
               for j in range(pairs)] for st in starts]
        splits = []
        for i in range(count):
            for z in zs[i]:
                drop = jnp.maximum(z, 0.0) + jnp.log(1.0 + jnp.exp2(jnp.abs(z) * -LOG2_E))
                if diag_first and i == 0:
                    drop = jnp.where(causal2, drop, 0.0)
                splits.append(drop.astype(BF16))
        totals = _dot(jnp.concatenate(splits, axis=0), from_key)
        accs, cs = list(accs), list(cs)
        for i in range(count):
            for j in range(pairs):
                base = (i * pairs + j) * 2 * tile
                total = totals[base:base + 2 * tile]
                cs[j] = cs[j] + voids[i]
                w = jnp.exp(zs[i][j] - total - cs[j])
                if diag_first and i == 0:
                    w = jnp.where(causal2, w, 0.0)
                w = w.astype(BF16)
                v = v_ref[0, pl.ds(starts[i], tile), blk(j)]
                accs[j] = accs[j] + _dot(
                    jnp.concatenate([w[:tile], w[tile:]], axis=1),
                    jnp.concatenate([masked(v, halves[0]), masked(v, halves[1])], axis=0))
                cs[j] = cs[j] + total[:, 0:1]
        return tuple(accs), tuple(cs)

    def alive_flag(cs):
        low = functools.reduce(jnp.minimum, cs)
        return (jnp.min(low) < SB_DEAD_DROP).astype(jnp.int32)

    accs = tuple(jnp.zeros((tile, LANES), F32) for _ in range(pairs))
    cs = tuple(jnp.zeros((2 * tile, 1), F32) for _ in range(pairs))
    accs, cs = sweep(qi, SB_FIRST_SWEEP, accs, cs, True)

    def cond(state):
        kb, _, _, alive = state
        return jnp.logical_and(kb >= 0, alive > 0)

    def body(state):
        kb, accs, cs, _ = state
        accs, cs = sweep(kb, SB_LOOP_SWEEP, accs, cs, False)
        return kb - SB_LOOP_SWEEP, accs, cs, alive_flag(cs)

    _, accs, _, _ = lax.while_loop(
        cond, body, (qi - SB_FIRST_SWEEP, accs, cs, alive_flag(cs)))
    for j, acc in enumerate(accs):
        o_ref[0, :, blk(j)] = acc.astype(o_ref.dtype)


def _sb_attention(qkv, batch, seq):
    width = SB_HEADS_PER_STEP * SB_HEAD_DIM
    groups = D_MODEL // width
    t = SB_TILE
    return pl.pallas_call(
        functools.partial(_sb_attn_kernel, tile=t, heads=SB_HEADS_PER_STEP),
        grid=(batch, groups, seq // t),
        in_specs=[pl.BlockSpec((1, t, width), lambda b, p, i: (b, i, p)),
                  pl.BlockSpec((1, seq, width), lambda b, p, i: (b, 0, groups + p),
                               pipeline_mode=pl.Buffered(1)),
                  pl.BlockSpec((1, seq, width), lambda b, p, i: (b, 0, 2 * groups + p),
                               pipeline_mode=pl.Buffered(1))],
        out_specs=pl.BlockSpec((1, t, width), lambda b, p, i: (b, i, p)),
        out_shape=jax.ShapeDtypeStruct((batch, seq, D_MODEL), BF16),
        compiler_params=_params(),
        name="sb_attention",
    )(qkv, qkv, qkv)


def _proj_mlp_kernel(x_ref, a_ref, wo_ref, g_ref, w1_ref, w2_ref, *rest, ff_chunk, final):
    if final:
        gf_ref, o_ref = rest
    else:
        (o_ref,) = rest
    x1 = x_ref[...] + _dot(a_ref[...], wo_ref[...])
    h = _rms(x1, g_ref[...]).astype(BF16)
    acc = x1
    for c in range(D_FF // ff_chunk):
        sl = slice(c * ff_chunk, (c + 1) * ff_chunk)
        u = jnp.maximum(_dot(h, w1_ref[:, sl]), 0.0)
        acc = acc + _dot((u * u).astype(BF16), w2_ref[sl, :])
    if final:
        acc = _rms(acc, gf_ref[...])
    o_ref[...] = acc


def _proj_mlp(x, a, wo, g, w1, w2, final_g=None):
    n, d = x.shape
    final = final_g is not None
    row = lambda i: (i, 0)
    in_specs = [pl.BlockSpec((ROW_TILE, d), row),
                pl.BlockSpec((ROW_TILE, a.shape[1]), row),
                _resident(wo.shape), _resident((1, d)),
                _resident(w1.shape), _resident(w2.shape)]
    args = [x, a, wo, g, w1, w2]
    if final:
        in_specs.append(_resident((1, d)))
        args.append(final_g)
    return pl.pallas_call(
        functools.partial(_proj_mlp_kernel, ff_chunk=1024, final=final),
        grid=(n // ROW_TILE,),
        in_specs=in_specs,
        out_specs=pl.BlockSpec((ROW_TILE, d), row),
        out_shape=jax.ShapeDtypeStruct((n, d), F32),
        compiler_params=_params(),
        name="proj_mlp_final" if final else "proj_mlp",
    )(*args)


def _rope_tables(pos, inv_lane):
    lane = lax.broadcasted_iota(jnp.int32, (1, LANES), 1)
    rope = jnp.logical_and(lane >= MLA_NOPE_DIM, lane < MLA_NOPE_DIM + MLA_ROPE_DIM)
    ang = pos * inv_lane
    cos_t = jnp.where(lane < MLA_NOPE_DIM, 1.0, jnp.where(rope, jnp.cos(ang), 0.0))
    sin_t = jnp.where(rope, jnp.sin(ang), 0.0)
    return cos_t, sin_t


def _shared_kv_kernel(x_ref, pos_ref, inv_ref, g_ref, wd_ref, gl_ref, wk_ref, wv_ref,
                      k_ref, v_ref, cos_ref, sin_ref):
    h = _rms(x_ref[...], g_ref[...]).astype(BF16)
    down = _dot(h, wd_ref[...])
    c = _rms(down[:, :MLA_KV_RANK], gl_ref[...]).astype(BF16)
    cos_t, sin_t = _rope_tables(pos_ref[...], inv_ref[...])
    cos_ref[...] = cos_t
    sin_ref[...] = sin_t
    k_rope = (down[:, MLA_KV_RANK:MLA_KV_RANK + LANES] * cos_t
              + down[:, MLA_KV_RANK + LANES:] * sin_t)
    k_nope = _dot(c, wk_ref[...])
    for hd in range(MLA_HEADS):
        sl = slice(hd * LANES, (hd + 1) * LANES)
        k_ref[:, sl] = (k_nope[:, sl] + k_rope).astype(k_ref.dtype)
    v_ref[...] = _dot(c, wv_ref[...]).astype(v_ref.dtype)


def _shared_kv(x, pos, inv_lane, g, wd, gl, wk, wv):
    n, d = x.shape
    row = lambda i: (i, 0)
    return pl.pallas_call(
        _shared_kv_kernel,
        grid=(n // ROW_TILE,),
        in_specs=[pl.BlockSpec((ROW_TILE, d), row),
                  pl.BlockSpec((ROW_TILE, 1), row),
                  _resident((1, LANES)), _resident((1, d)), _resident(wd.shape),
                  _resident((1, MLA_KV_RANK)), _resident(wk.shape), _resident(wv.shape)],
        out_specs=[pl.BlockSpec((ROW_TILE, MLA_HEADS * LANES), row),
                   pl.BlockSpec((ROW_TILE, MLA_HEADS * MLA_V_DIM), row),
                   pl.BlockSpec((ROW_TILE, LANES), row),
                   pl.BlockSpec((ROW_TILE, LANES), row)],
        out_shape=[jax.ShapeDtypeStruct((n, MLA_HEADS * LANES), BF16),
                   jax.ShapeDtypeStruct((n, MLA_HEADS * MLA_V_DIM), BF16),
                   jax.ShapeDtypeStruct((n, LANES), F32),
                   jax.ShapeDtypeStruct((n, LANES), F32)],
        compiler_params=_params(),
        name="mla_shared_kv",
    )(x, pos, inv_lane, g, wd, gl, wk, wv)


def _mla_q_kernel(x_ref, cos_ref, sin_ref, g_ref, wdq_ref, gq_ref, wa_ref, q_ref,
                  *, heads_per_chunk):
    h = _rms(x_ref[...], g_ref[...]).astype(BF16)
    cq = _rms(_dot(h, wdq_ref[...]), gq_ref[...]).astype(BF16)
    cos_t, sin_t = cos_ref[...], sin_ref[...]
    scale = (MLA_NOPE_DIM + MLA_ROPE_DIM) ** -0.5 * math.log2(math.e)
    cos_t = cos_t * scale
    lane = lax.broadcasted_iota(jnp.int32, (1, LANES), 1)
    low_half = lane < MLA_NOPE_DIM + MLA_ROPE_DIM // 2
    sin_t = jnp.where(low_half, -sin_t, sin_t) * scale
    half = MLA_ROPE_DIM // 2
    width = heads_per_chunk * LANES
    for c in range(MLA_HEADS // heads_per_chunk):
        qa = _dot(cq, wa_ref[:, c * width:(c + 1) * width])
        for j in range(heads_per_chunk):
            qh = qa[:, j * LANES:(j + 1) * LANES]
            partner = pltpu.roll(qh, LANES - half, 1)
            q_ref[:, c * width + j * LANES:c * width + (j + 1) * LANES] = (
                qh * cos_t + partner * sin_t).astype(q_ref.dtype)


def _mla_q(x, cos_t, sin_t, g, wdq, gq, wa):
    n, d = x.shape
    row = lambda i: (i, 0)
    return pl.pallas_call(
        functools.partial(_mla_q_kernel, heads_per_chunk=4),
        grid=(n // ROW_TILE,),
        in_specs=[pl.BlockSpec((ROW_TILE, d), row),
                  pl.BlockSpec((ROW_TILE, LANES), row),
                  pl.BlockSpec((ROW_TILE, LANES), row), _resident((1, d)), _resident(wdq.shape),
                  _resident((1, MLA_Q_RANK)), _resident(wa.shape)],
        out_specs=pl.BlockSpec((ROW_TILE, MLA_HEADS * LANES), row),
        out_shape=jax.ShapeDtypeStruct((n, MLA_HEADS * LANES), BF16),
        compiler_params=_params(),
        name="mla_q",
    )(x, cos_t, sin_t, g, wdq, gq, wa)


def _mla_attn_kernel(q_ref, k_ref, v_ref, o_ref, *, tile, heads):
    qi = pl.program_id(2)
    pairs = heads // 2
    lane = lax.broadcasted_iota(jnp.int32, (1, LANES), 1)
    halves = (lane < MLA_V_DIM, lane >= MLA_V_DIM)
    row = lax.broadcasted_iota(jnp.int32, (tile, tile), 0)
    col = lax.broadcasted_iota(jnp.int32, (tile, tile), 1)
    allowed = (col // CHUNK) <= (row // CHUNK)

    def masked(x, keep):
        return jnp.where(keep, x, jnp.zeros_like(x))

    def blk(j):
        return slice(j * LANES, (j + 1) * LANES)

    key_row = lax.broadcasted_iota(jnp.int32, (2 * tile, 1), 0)
    ones_cols = jnp.where((key_row < tile) == halves[0], 1.0, 0.0).astype(BF16)

    def sweep(kbs, diag_last, carry):
        ms, accs = carry
        starts = [pl.multiple_of(kb * tile, tile) for kb in kbs]
        ss = [[_dot_nt(q_ref[0, :, blk(h)], k_ref[0, pl.ds(st, tile), blk(h)])
               for st in starts] for h in range(heads)]
        if diag_last:
            for s_head in ss:
                s_head[-1] = jnp.where(allowed, s_head[-1], MASK_VALUE)
        count = len(kbs)
        ps, alphas, new_ms = [], [], []
        for h in range(heads):
            top = functools.reduce(jnp.maximum, ss[h])
            m_new = jnp.maximum(ms[h], jnp.max(top, axis=-1, keepdims=True))
            alphas.append(jnp.exp2(ms[h] - m_new))
            new_ms.append(m_new)
            ps.append([jnp.exp2(s - m_new).astype(BF16) for s in ss[h]])
        new_accs = []
        for j in range(pairs):
            lhs, rhs = [], []
            for n, st in enumerate(starts):
                v = v_ref[0, pl.ds(st, tile), blk(j)]
                vals = jnp.concatenate([masked(v, halves[0]), masked(v, halves[1])], axis=0)
                lhs += [ps[2 * j][n], ps[2 * j + 1][n]]
                rhs.append(jnp.concatenate([vals, ones_cols], axis=1))
            pv = _dot(jnp.concatenate(lhs, axis=1),
                      jnp.concatenate(rhs, axis=0))
            alpha = jnp.where(halves[0], alphas[2 * j], alphas[2 * j + 1])
            new_accs.append(jnp.concatenate([alpha, alpha], axis=1) * accs[j] + pv)
        return tuple(new_ms), tuple(new_accs)

    carry = (tuple(jnp.full((tile, 1), MASK_VALUE, F32) for _ in range(heads)),
             tuple(jnp.zeros((tile, 2 * LANES), F32) for _ in range(pairs)))
    nblk = MLA_BLOCK_TILES
    carry = lax.fori_loop(
        0, qi // nblk, lambda i, cr: sweep([nblk * i + t for t in range(nblk)], False, cr), carry)

    def last_block(rest):
        return lambda cr: sweep([qi - rest + t for t in range(rest)] + [qi], True, cr)

    _, accs = lax.switch(qi % nblk, [last_block(rest) for rest in range(nblk)], carry)
    for j in range(pairs):
        o_ref[0, :, blk(j)] = (accs[j][:, :LANES] / accs[j][:, LANES:]).astype(o_ref.dtype)


def _mla_attention(q, k, v, batch, seq):
    groups = MLA_HEADS // MLA_HEADS_PER_STEP
    t = MLA_TILE
    qk_width = MLA_HEADS_PER_STEP * LANES
    v_width = MLA_HEADS_PER_STEP * MLA_V_DIM
    return pl.pallas_call(
        functools.partial(_mla_attn_kernel, tile=t, heads=MLA_HEADS_PER_STEP),
        grid=(batch, groups, seq // t),
        in_specs=[pl.BlockSpec((1, t, qk_width), lambda b, p, i: (b, i, p)),
                  pl.BlockSpec((1, seq, qk_width), lambda b, p, i: (b, 0, p)),
                  pl.BlockSpec((1, seq, v_width), lambda b, p, i: (b, 0, p))],
        out_specs=pl.BlockSpec((1, t, v_width), lambda b, p, i: (b, i, p)),
        out_shape=jax.ShapeDtypeStruct((batch, seq, MLA_HEADS * MLA_V_DIM), BF16),
        compiler_params=_params(),
        name="mla_attention",
    )(q, k, v)


def _rope_lane_freqs():
    inv_freq = ROPE_THETA ** (-jnp.arange(0, MLA_ROPE_DIM, 2, dtype=F32) / MLA_ROPE_DIM)
    half = MLA_ROPE_DIM // 2
    out = jnp.zeros((1, LANES), F32)
    out = out.at[0, MLA_NOPE_DIM:MLA_NOPE_DIM + half].set(inv_freq)
    out = out.at[0, MLA_NOPE_DIM + half:MLA_NOPE_DIM + 2 * half].set(inv_freq)
    return out


def _layout_w_dkv(w_dkv):
    half = MLA_ROPE_DIM // 2
    d = w_dkv.shape[0]
    t1 = w_dkv[:, MLA_KV_RANK:MLA_KV_RANK + half]
    t2 = w_dkv[:, MLA_KV_RANK + half:]
    pad_lo = jnp.zeros((d, MLA_NOPE_DIM), w_dkv.dtype)
    pad_hi = jnp.zeros((d, LANES - MLA_NOPE_DIM - MLA_ROPE_DIM), w_dkv.dtype)
    blk_a = jnp.concatenate([pad_lo, t1, t2, pad_hi], axis=1)
    blk_b = jnp.concatenate([pad_lo, -t2, t1, pad_hi], axis=1)
    return jnp.concatenate([w_dkv[:, :MLA_KV_RANK], blk_a, blk_b], axis=1).astype(BF16)


def _layout_w_ukv(w_ukv):
    r = w_ukv.shape[0]
    w = w_ukv.reshape(r, MLA_HEADS, MLA_NOPE_DIM + MLA_V_DIM)
    wk = jnp.concatenate(
        [w[:, :, :MLA_NOPE_DIM], jnp.zeros((r, MLA_HEADS, LANES - MLA_NOPE_DIM), w.dtype)],
        axis=2).reshape(r, MLA_HEADS * LANES)
    wv = w[:, :, MLA_NOPE_DIM:].reshape(r, MLA_HEADS * MLA_V_DIM)
    return wk.astype(BF16), wv.astype(BF16)


def _layout_w_uq(w_uq):
    r = w_uq.shape[0]
    half = MLA_ROPE_DIM // 2
    w = w_uq.reshape(r, MLA_HEADS, MLA_NOPE_DIM + MLA_ROPE_DIM)
    t1 = w[:, :, MLA_NOPE_DIM:MLA_NOPE_DIM + half]
    pad_hi = jnp.zeros((r, MLA_HEADS, LANES - MLA_NOPE_DIM - MLA_ROPE_DIM - half), w.dtype)
    return jnp.concatenate([w, t1, pad_hi], axis=2).reshape(r, MLA_HEADS * LANES).astype(BF16)


def kernel(x, positions, attn_norm, mlp_norm, sb_w_qkv, sb_w_o, kv_norm, mla_w_dkv,
           mla_kv_lat_norm, mla_w_ukv, mla_w_dq, mla_q_lat_norm, mla_w_uq, mla_w_o,
           mlp_w1, mlp_w2, final_norm):
    batch, seq, d = x.shape
    n = batch * seq
    xs = x.reshape(n, d)
    pos = positions.astype(F32).reshape(n, 1)
    inv_lane = _rope_lane_freqs()

    for layer in range(N_A_LAYERS):
        qkv = _norm_matmul(xs, attn_norm[layer].reshape(1, d), sb_w_qkv[layer].astype(BF16))
        attn = _sb_attention(qkv.reshape(batch, seq, 3 * d), batch, seq)
        xs = _proj_mlp(xs, attn.reshape(n, d), sb_w_o[layer].astype(BF16),
                       mlp_norm[layer].reshape(1, d),
                       mlp_w1[layer].astype(BF16), mlp_w2[layer].astype(BF16))

    wk, wv = _layout_w_ukv(mla_w_ukv)
    k_all, v_all, cos_t, sin_t = _shared_kv(
        xs, pos, inv_lane, kv_norm.reshape(1, d), _layout_w_dkv(mla_w_dkv),
        mla_kv_lat_norm.reshape(1, MLA_KV_RANK), wk, wv)
    k_all = k_all.reshape(batch, seq, MLA_HEADS * LANES)
    v_all = v_all.reshape(batch, seq, MLA_HEADS * MLA_V_DIM)

    for layer in range(N_A_LAYERS, DEPTH):
        j = layer - N_A_LAYERS
        q = _mla_q(xs, cos_t, sin_t, attn_norm[layer].reshape(1, d),
                   mla_w_dq[j].astype(BF16), mla_q_lat_norm[j].reshape(1, MLA_Q_RANK),
                   _layout_w_uq(mla_w_uq[j]))
        attn = _mla_attention(q.reshape(batch, seq, MLA_HEADS * LANES), k_all, v_all, batch, seq)
        xs = _proj_mlp(xs, attn.reshape(n, MLA_HEADS * MLA_V_DIM), mla_w_o[j].astype(BF16),
                       mlp_norm[layer].reshape(1, d),
                       mlp_w1[layer].astype(BF16), mlp_w2[layer].astype(BF16),
                       final_g=final_norm.reshape(1, d) if layer == DEPTH - 1 else None)
    return xs.reshape(batch, seq, d)
```

```python
import functools
import math

import jax
import jax.numpy as jnp
from jax import lax
from jax.experimental import pallas as pl
from jax.experimental.pallas import tpu as pltpu

D_MODEL = 1024
DEPTH = 4
CHUNK = 64
N_A_LAYERS = DEPTH // 2
SB_HEADS = 16
SB_HEAD_DIM = D_MODEL // SB_HEADS
MLA_HEADS = 16
MLA_NOPE_DIM = 64
MLA_ROPE_DIM = 32
MLA_V_DIM = 64
MLA_Q_RANK = 384
MLA_KV_RANK = 256
D_FF = 4 * D_MODEL
ROPE_THETA = 10000.0
NORM_EPS = 1e-6

LANES = 128
ROW_TILE = 1024
SB_TILE = 128
SB_HEADS_PER_STEP = 16
SB_DEAD_DROP = 104.0
SB_FIRST_SWEEP = 3
SB_LOOP_SWEEP = 2
MLA_TILE = 512
MLA_HEADS_PER_STEP = 4
MLA_BLOCK_TILES = 4
VMEM_LIMIT = 56 * 1024 * 1024
MASK_VALUE = -1e30
LOG2_E = math.log2(math.e)

F32 = jnp.float32
BF16 = jnp.bfloat16


def _rms(x, g):
    return x * lax.rsqrt(jnp.mean(x * x, axis=-1, keepdims=True) + NORM_EPS) * g


def _dot(a, b):
    return jnp.dot(a, b, preferred_element_type=F32)


def _dot_nt(a, b):
    return lax.dot_general(a, b, (((1,), (1,)), ((), ())), preferred_element_type=F32)


def _params():
    return pltpu.CompilerParams(
        dimension_semantics=None, vmem_limit_bytes=VMEM_LIMIT)


def _resident(shape):
    return pl.BlockSpec(shape, lambda *_: (0,) * len(shape),
                        pipeline_mode=pl.Buffered(1))


def _norm_matmul_kernel(x_ref, g_ref, w_ref, o_ref, *, col_chunk):
    h = _rms(x_ref[...], g_ref[...]).astype(BF16)
    n_out = o_ref.shape[1]
    for c in range(n_out // col_chunk):
        sl = slice(c * col_chunk, (c + 1) * col_chunk)
        o_ref[:, sl] = _dot(h, w_ref[:, sl]).astype(o_ref.dtype)


def _norm_matmul(x, g, w):
    n, d = x.shape
    n_out = w.shape[1]
    return pl.pallas_call(
        functools.partial(_norm_matmul_kernel, col_chunk=1024),
        grid=(n // ROW_TILE,),
        in_specs=[pl.BlockSpec((ROW_TILE, d), lambda i: (i, 0)),
                  _resident((1, d)),
                  _resident((d, n_out))],
        out_specs=pl.BlockSpec((ROW_TILE, n_out), lambda i: (i, 0)),
        out_shape=jax.ShapeDtypeStruct((n, n_out), BF16),
        compiler_params=_params(),
        name="norm_qkv",
    )(x, g, w)


def _sb_attn_kernel(q_ref, k_ref, v_ref, o_ref, *, tile, heads):
    qi = pl.program_id(2)
    pairs = heads // 2
    lane = lax.broadcasted_iota(jnp.int32, (1, LANES), 1)
    halves = (lane < SB_HEAD_DIM, lane >= SB_HEAD_DIM)
    row = lax.broadcasted_iota(jnp.int32, (tile, tile), 0)
    col = lax.broadcasted_iota(jnp.int32, (tile, tile), 1)
    from_key = (row >= col).astype(BF16)
    causal = col < row
    causal2 = jnp.concatenate([causal, causal], axis=0)

    def masked(x, keep):
        return jnp.where(keep, x, jnp.zeros_like(x))

    def blk(j):
        return slice(j * LANES, (j + 1) * LANES)

    q_pairs = []
    for j in range(pairs):
        q = q_ref[0, :, blk(j)] * (SB_HEAD_DIM ** -0.5)
        q_pairs.append(jnp.concatenate([masked(q, halves[0]), masked(q, halves[1])], axis=0))

    def sweep(kb0, count, accs, cs, diag_first):
        kbs = [kb0 - i for i in range(count)]
        starts = [pl.multiple_of(jnp.maximum(kb, 0) * tile, tile) for kb in kbs]
        zs = [[_dot_nt(q_pairs[j], k_ref[0, pl.ds(st, tile), blk(j)])
               for j in range(pairs)] for st in starts]
        splits = []
        for i in range(count):
            for z in zs[i]:
                drop = jnp.maximum(z, 0.0) + jnp.log(1.0 + jnp.exp2(jnp.abs(z) * -LOG2_E))
                if diag_first and i == 0:
                    drop = jnp.where(causal2, drop, 0.0)
                splits.append(drop.astype(BF16))
        totals = _dot(jnp.concatenate(splits, axis=0), from_key)
        accs, cs = list(accs), list(cs)
        for i in range(count):
            for j in range(pairs):
                base = (i * pairs + j) * 2 * tile
                total = totals[base:base + 2 * tile]
                v = v_ref[0, pl.ds(starts[i], tile), blk(j)]
                if diag_first and i == 0:
                    w = jnp.where(causal2, jnp.exp(zs[i][j] - total), 0.0)
                    cs[j] = total[:, 0:1]
                else:
                    w = jnp.exp(zs[i][j] - total - cs[j])
                    cs[j] = cs[j] + total[:, 0:1]
                    v = jnp.where(kbs[i] >= 0, v, jnp.zeros_like(v))
                accs[j] = accs[j] + _dot(
                    jnp.concatenate([w[:tile], w[tile:]], axis=1).astype(BF16),
                    jnp.concatenate([masked(v, halves[0]), masked(v, halves[1])], axis=0))
        return tuple(accs), tuple(cs)

    def alive_flag(cs):
        low = functools.reduce(jnp.minimum, cs)
        return (jnp.min(low) < SB_DEAD_DROP).astype(jnp.int32)

    accs = tuple(jnp.zeros((tile, LANES), F32) for _ in range(pairs))
    cs = tuple(jnp.zeros((2 * tile, 1), F32) for _ in range(pairs))
    accs, cs = sweep(qi, SB_FIRST_SWEEP, accs, cs, True)

    def cond(state):
        kb, _, _, alive = state
        return jnp.logical_and(kb >= 0, alive > 0)

    def body(state):
        kb, accs, cs, _ = state
        accs, cs = sweep(kb, SB_LOOP_SWEEP, accs, cs, False)
        return kb - SB_LOOP_SWEEP, accs, cs, alive_flag(cs)

    _, accs, _, _ = lax.while_loop(
        cond, body, (qi - SB_FIRST_SWEEP, accs, cs, alive_flag(cs)))
    for j, acc in enumerate(accs):
        o_ref[0, :, blk(j)] = acc.astype(o_ref.dtype)


def _sb_attention(qkv, batch, seq):
    width = SB_HEADS_PER_STEP * SB_HEAD_DIM
    groups = D_MODEL // width
    t = SB_TILE
    return pl.pallas_call(
        functools.partial(_sb_attn_kernel, tile=t, heads=SB_HEADS_PER_STEP),
        grid=(batch, groups, seq // t),
        in_specs=[pl.BlockSpec((1, t, width), lambda b, p, i: (b, i, p)),
                  pl.BlockSpec((1, seq, width), lambda b, p, i: (b, 0, groups + p),
                               pipeline_mode=pl.Buffered(1)),
                  pl.BlockSpec((1, seq, width), lambda b, p, i: (b, 0, 2 * groups + p),
                               pipeline_mode=pl.Buffered(1))],
        out_specs=pl.BlockSpec((1, t, width), lambda b, p, i: (b, i, p)),
        out_shape=jax.ShapeDtypeStruct((batch, seq, D_MODEL), BF16),
        compiler_params=_params(),
        name="sb_attention",
    )(qkv, qkv, qkv)


def _proj_mlp_kernel(x_ref, a_ref, wo_ref, g_ref, w1_ref, w2_ref, *rest, ff_chunk, final):
    if final:
        gf_ref, o_ref = rest
    else:
        (o_ref,) = rest
    x1 = x_ref[...] + _dot(a_ref[...], wo_ref[...])
    h = _rms(x1, g_ref[...]).astype(BF16)
    acc = x1
    for c in range(D_FF // ff_chunk):
        sl = slice(c * ff_chunk, (c + 1) * ff_chunk)
        u = jnp.maximum(_dot(h, w1_ref[:, sl]), 0.0)
        acc = acc + _dot((u * u).astype(BF16), w2_ref[sl, :])
    if final:
        acc = _rms(acc, gf_ref[...])
    o_ref[...] = acc


def _proj_mlp(x, a, wo, g, w1, w2, final_g=None):
    n, d = x.shape
    final = final_g is not None
    row = lambda i: (i, 0)
    in_specs = [pl.BlockSpec((ROW_TILE, d), row),
                pl.BlockSpec((ROW_TILE, a.shape[1]), row),
                _resident(wo.shape), _resident((1, d)),
                _resident(w1.shape), _resident(w2.shape)]
    args = [x, a, wo, g, w1, w2]
    if final:
        in_specs.append(_resident((1, d)))
        args.append(final_g)
    return pl.pallas_call(
        functools.partial(_proj_mlp_kernel, ff_chunk=1024, final=final),
        grid=(n // ROW_TILE,),
        in_specs=in_specs,
        out_specs=pl.BlockSpec((ROW_TILE, d), row),
        out_shape=jax.ShapeDtypeStruct((n, d), F32),
        compiler_params=_params(),
        name="proj_mlp_final" if final else "proj_mlp",
    )(*args)


def _rope_tables(pos, inv_lane):
    lane = lax.broadcasted_iota(jnp.int32, (1, LANES), 1)
    rope = jnp.logical_and(lane >= MLA_NOPE_DIM, lane < MLA_NOPE_DIM + MLA_ROPE_DIM)
    ang = pos * inv_lane
    cos_t = jnp.where(lane < MLA_NOPE_DIM, 1.0, jnp.where(rope, jnp.cos(ang), 0.0))
    sin_t = jnp.where(rope, jnp.sin(ang), 0.0)
    return cos_t, sin_t


def _shared_kv_kernel(x_ref, pos_ref, inv_ref, g_ref, wd_ref, gl_ref, wk_ref, wv_ref,
                      k_ref, v_ref, cos_ref, sin_ref):
    h = _rms(x_ref[...], g_ref[...]).astype(BF16)
    down = _dot(h, wd_ref[...])
    c = _rms(down[:, :MLA_KV_RANK], gl_ref[...]).astype(BF16)
    cos_t, sin_t = _rope_tables(pos_ref[...], inv_ref[...])
    cos_ref[...] = cos_t
    sin_ref[...] = sin_t
    k_rope = (down[:, MLA_KV_RANK:MLA_KV_RANK + LANES] * cos_t
              + down[:, MLA_KV_RANK + LANES:] * sin_t)
    k_nope = _dot(c, wk_ref[...])
    for hd in range(MLA_HEADS):
        sl = slice(hd * LANES, (hd + 1) * LANES)
        k_ref[:, sl] = (k_nope[:, sl] + k_rope).astype(k_ref.dtype)
    v_ref[...] = _dot(c, wv_ref[...]).astype(v_ref.dtype)


def _shared_kv(x, pos, inv_lane, g, wd, gl, wk, wv):
    n, d = x.shape
    row = lambda i: (i, 0)
    return pl.pallas_call(
        _shared_kv_kernel,
        grid=(n // ROW_TILE,),
        in_specs=[pl.BlockSpec((ROW_TILE, d), row),
                  pl.BlockSpec((ROW_TILE, 1), row),
                  _resident((1, LANES)), _resident((1, d)), _resident(wd.shape),
                  _resident((1, MLA_KV_RANK)), _resident(wk.shape), _resident(wv.shape)],
        out_specs=[pl.BlockSpec((ROW_TILE, MLA_HEADS * LANES), row),
                   pl.BlockSpec((ROW_TILE, MLA_HEADS * MLA_V_DIM), row),
                   pl.BlockSpec((ROW_TILE, LANES), row),
                   pl.BlockSpec((ROW_TILE, LANES), row)],
        out_shape=[jax.ShapeDtypeStruct((n, MLA_HEADS * LANES), BF16),
                   jax.ShapeDtypeStruct((n, MLA_HEADS * MLA_V_DIM), BF16),
                   jax.ShapeDtypeStruct((n, LANES), F32),
                   jax.ShapeDtypeStruct((n, LANES), F32)],
        compiler_params=_params(),
        name="mla_shared_kv",
    )(x, pos, inv_lane, g, wd, gl, wk, wv)


def _mla_q_kernel(x_ref, cos_ref, sin_ref, g_ref, wdq_ref, gq_ref, wa_ref, q_ref,
                  *, heads_per_chunk):
    h = _rms(x_ref[...], g_ref[...]).astype(BF16)
    cq = _rms(_dot(h, wdq_ref[...]), gq_ref[...]).astype(BF16)
    cos_t, sin_t = cos_ref[...], sin_ref[...]
    scale = (MLA_NOPE_DIM + MLA_ROPE_DIM) ** -0.5 * math.log2(math.e)
    cos_t = cos_t * scale
    lane = lax.broadcasted_iota(jnp.int32, (1, LANES), 1)
    low_half = lane < MLA_NOPE_DIM + MLA_ROPE_DIM // 2
    sin_t = jnp.where(low_half, -sin_t, sin_t) * scale
    half = MLA_ROPE_DIM // 2
    width = heads_per_chunk * LANES
    for c in range(MLA_HEADS // heads_per_chunk):
        qa = _dot(cq, wa_ref[:, c * width:(c + 1) * width])
        for j in range(heads_per_chunk):
            qh = qa[:, j * LANES:(j + 1) * LANES]
            partner = pltpu.roll(qh, LANES - half, 1)
            q_ref[:, c * width + j * LANES:c * width + (j + 1) * LANES] = (
                qh * cos_t + partner * sin_t).astype(q_ref.dtype)


def _mla_q(x, cos_t, sin_t, g, wdq, gq, wa):
    n, d = x.shape
    row = lambda i: (i, 0)
    return pl.pallas_call(
        functools.partial(_mla_q_kernel, heads_per_chunk=4),
        grid=(n // ROW_TILE,),
        in_specs=[pl.BlockSpec((ROW_TILE, d), row),
                  pl.BlockSpec((ROW_TILE, LANES), row),
                  pl.BlockSpec((ROW_TILE, LANES), row), _resident((1, d)), _resident(wdq.shape),
                  _resident((1, MLA_Q_RANK)), _resident(wa.shape)],
        out_specs=pl.BlockSpec((ROW_TILE, MLA_HEADS * LANES), row),
        out_shape=jax.ShapeDtypeStruct((n, MLA_HEADS * LANES), BF16),
        compiler_params=_params(),
        name="mla_q",
    )(x, cos_t, sin_t, g, wdq, gq, wa)


def _mla_attn_kernel(q_ref, k_ref, v_ref, o_ref, *, tile, heads):
    qi = pl.program_id(2)
    pairs = heads // 2
    lane = lax.broadcasted_iota(jnp.int32, (1, LANES), 1)
    halves = (lane < MLA_V_DIM, lane >= MLA_V_DIM)
    row = lax.broadcasted_iota(jnp.int32, (tile, tile), 0)
    col = lax.broadcasted_iota(jnp.int32, (tile, tile), 1)
    allowed = (col // CHUNK) <= (row // CHUNK)

    def masked(x, keep):
        return jnp.where(keep, x, jnp.zeros_like(x))

    def blk(j):
        return slice(j * LANES, (j + 1) * LANES)

    key_row = lax.broadcasted_iota(jnp.int32, (2 * tile, 1), 0)
    ones_cols = jnp.where((key_row < tile) == halves[0], 1.0, 0.0).astype(BF16)

    def sweep(kbs, diag_last, carry):
        ms, accs = carry
        starts = [pl.multiple_of(kb * tile, tile) for kb in kbs]
        ss = [[_dot_nt(q_ref[0, :, blk(h)], k_ref[0, pl.ds(st, tile), blk(h)])
               for st in starts] for h in range(heads)]
        if diag_last:
            for s_head in ss:
                s_head[-1] = jnp.where(allowed, s_head[-1], MASK_VALUE)
        count = len(kbs)
        ps, alphas, new_ms = [], [], []
        for h in range(heads):
            top = functools.reduce(jnp.maximum, ss[h])
            m_new = jnp.maximum(ms[h], jnp.max(top, axis=-1, keepdims=True))
            alphas.append(jnp.exp2(ms[h] - m_new))
            new_ms.append(m_new)
            ps.append([jnp.exp2(s - m_new).astype(BF16) for s in ss[h]])
        new_accs = []
        for j in range(pairs):
            lhs, rhs = [], []
            for n, st in enumerate(starts):
                v = v_ref[0, pl.ds(st, tile), blk(j)]
                vals = jnp.concatenate([masked(v, halves[0]), masked(v, halves[1])], axis=0)
                lhs += [ps[2 * j][n], ps[2 * j + 1][n]]
                rhs.append(jnp.concatenate([vals, ones_cols], axis=1))
            pv = _dot(jnp.concatenate(lhs, axis=1),
                      jnp.concatenate(rhs, axis=0))
            alpha = jnp.where(halves[0], alphas[2 * j], alphas[2 * j + 1])
            new_accs.append(jnp.concatenate([alpha, alpha], axis=1) * accs[j] + pv)
        return tuple(new_ms), tuple(new_accs)

    carry = (tuple(jnp.full((tile, 1), MASK_VALUE, F32) for _ in range(heads)),
             tuple(jnp.zeros((tile, 2 * LANES), F32) for _ in range(pairs)))
    nblk = MLA_BLOCK_TILES
    carry = lax.fori_loop(
        0, qi // nblk, lambda i, cr: sweep([nblk * i + t for t in range(nblk)], False, cr), carry)

    def last_block(rest):
        return lambda cr: sweep([qi - rest + t for t in range(rest)] + [qi], True, cr)

    _, accs = lax.switch(qi % nblk, [last_block(rest) for rest in range(nblk)], carry)
    for j in range(pairs):
        o_ref[0, :, blk(j)] = (accs[j][:, :LANES] / accs[j][:, LANES:]).astype(o_ref.dtype)


def _mla_attention(q, k, v, batch, seq):
    groups = MLA_HEADS // MLA_HEADS_PER_STEP
    t = MLA_TILE
    qk_width = MLA_HEADS_PER_STEP * LANES
    v_width = MLA_HEADS_PER_STEP * MLA_V_DIM
    return pl.pallas_call(
        functools.partial(_mla_attn_kernel, tile=t, heads=MLA_HEADS_PER_STEP),
        grid=(batch, groups, seq // t),
        in_specs=[pl.BlockSpec((1, t, qk_width), lambda b, p, i: (b, i, p)),
                  pl.BlockSpec((1, seq, qk_width), lambda b, p, i: (b, 0, p)),
                  pl.BlockSpec((1, seq, v_width), lambda b, p, i: (b, 0, p))],
        out_specs=pl.BlockSpec((1, t, v_width), lambda b, p, i: (b, i, p)),
        out_shape=jax.ShapeDtypeStruct((batch, seq, MLA_HEADS * MLA_V_DIM), BF16),
        compiler_params=_params(),
        name="mla_attention",
    )(q, k, v)


def _rope_lane_freqs():
    inv_freq = ROPE_THETA ** (-jnp.arange(0, MLA_ROPE_DIM, 2, dtype=F32) / MLA_ROPE_DIM)
    half = MLA_ROPE_DIM // 2
    out = jnp.zeros((1, LANES), F32)
    out = out.at[0, MLA_NOPE_DIM:MLA_NOPE_DIM + half].set(inv_freq)
    out = out.at[0, MLA_NOPE_DIM + half:MLA_NOPE_DIM + 2 * half].set(inv_freq)
    return out


def _layout_w_dkv(w_dkv):
    half = MLA_ROPE_DIM // 2
    d = w_dkv.shape[0]
    t1 = w_dkv[:, MLA_KV_RANK:MLA_KV_RANK + half]
    t2 = w_dkv[:, MLA_KV_RANK + half:]
    pad_lo = jnp.zeros((d, MLA_NOPE_DIM), w_dkv.dtype)
    pad_hi = jnp.zeros((d, LANES - MLA_NOPE_DIM - MLA_ROPE_DIM), w_dkv.dtype)
    blk_a = jnp.concatenate([pad_lo, t1, t2, pad_hi], axis=1)
    blk_b = jnp.concatenate([pad_lo, -t2, t1, pad_hi], axis=1)
    return jnp.concatenate([w_dkv[:, :MLA_KV_RANK], blk_a, blk_b], axis=1).astype(BF16)


def _layout_w_ukv(w_ukv):
    r = w_ukv.shape[0]
    w = w_ukv.reshape(r, MLA_HEADS, MLA_NOPE_DIM + MLA_V_DIM)
    wk = jnp.concatenate(
        [w[:, :, :MLA_NOPE_DIM], jnp.zeros((r, MLA_HEADS, LANES - MLA_NOPE_DIM), w.dtype)],
        axis=2).reshape(r, MLA_HEADS * LANES)
    wv = w[:, :, MLA_NOPE_DIM:].reshape(r, MLA_HEADS * MLA_V_DIM)
    return wk.astype(BF16), wv.astype(BF16)


def _layout_w_uq(w_uq):
    r = w_uq.shape[0]
    half = MLA_ROPE_DIM // 2
    w = w_uq.reshape(r, MLA_HEADS, MLA_NOPE_DIM + MLA_ROPE_DIM)
    t1 = w[:, :, MLA_NOPE_DIM:MLA_NOPE_DIM + half]
    pad_hi = jnp.zeros((r, MLA_HEADS, LANES - MLA_NOPE_DIM - MLA_ROPE_DIM - half), w.dtype)
    return jnp.concatenate([w, t1, pad_hi], axis=2).reshape(r, MLA_HEADS * LANES).astype(BF16)


def kernel(x, positions, attn_norm, mlp_norm, sb_w_qkv, sb_w_o, kv_norm, mla_w_dkv,
           mla_kv_lat_norm, mla_w_ukv, mla_w_dq, mla_q_lat_norm, mla_w_uq, mla_w_o,
           mlp_w1, mlp_w2, final_norm):
    batch, seq, d = x.shape
    n = batch * seq
    xs = x.reshape(n, d)
    pos = positions.astype(F32).reshape(n, 1)
    inv_lane = _rope_lane_freqs()

    for layer in range(N_A_LAYERS):
        qkv = _norm_matmul(xs, attn_norm[layer].reshape(1, d), sb_w_qkv[layer].astype(BF16))
        attn = _sb_attention(qkv.reshape(batch, seq, 3 * d), batch, seq)
        xs = _proj_mlp(xs, attn.reshape(n, d), sb_w_o[layer].astype(BF16),
                       mlp_norm[layer].reshape(1, d),
                       mlp_w1[layer].astype(BF16), mlp_w2[layer].astype(BF16))

    wk, wv = _layout_w_ukv(mla_w_ukv)
    k_all, v_all, cos_t, sin_t = _shared_kv(
        xs, pos, inv_lane, kv_norm.reshape(1, d), _layout_w_dkv(mla_w_dkv),
        mla_kv_lat_norm.reshape(1, MLA_KV_RANK), wk, wv)
    k_all = k_all.reshape(batch, seq, MLA_HEADS * LANES)
    v_all = v_all.reshape(batch, seq, MLA_HEADS * MLA_V_DIM)

    for layer in range(N_A_LAYERS, DEPTH):
        j = layer - N_A_LAYERS
        q = _mla_q(xs, cos_t, sin_t, attn_norm[layer].reshape(1, d),
                   mla_w_dq[j].astype(BF16), mla_q_lat_norm[j].reshape(1, MLA_Q_RANK),
                   _layout_w_uq(mla_w_uq[j]))
        attn = _mla_attention(q.reshape(batch, seq, MLA_HEADS * LANES), k_all, v_all, batch, seq)
        xs = _proj_mlp(xs, attn.reshape(n, MLA_HEADS * MLA_V_DIM), mla_w_o[j].astype(BF16),
                       mlp_norm[layer].reshape(1, d),
                       mlp_w1[layer].astype(BF16), mlp_w2[layer].astype(BF16),
                       final_g=final_norm.reshape(1, d) if layer == DEPTH - 1 else None)
    return xs.reshape(batch, seq, d)
```

```python
import functools
import math

import jax
import jax.numpy as jnp
from jax import lax
from jax.experimental import pallas as pl
from jax.experimental.pallas import tpu as pltpu

D_MODEL = 1024
DEPTH = 4
CHUNK = 64
N_A_LAYERS = DEPTH // 2
SB_HEADS = 16
SB_HEAD_DIM = D_MODEL // SB_HEADS
MLA_HEADS = 16
MLA_NOPE_DIM = 64
MLA_ROPE_DIM = 32
MLA_V_DIM = 64
MLA_Q_RANK = 384
MLA_KV_RANK = 256
D_FF = 4 * D_MODEL
ROPE_THETA = 10000.0
NORM_EPS = 1e-6

LANES = 128
ROW_TILE = 1024
COL_CHUNK = 1024
SB_TILE = 128
SB_HEADS_PER_STEP = 16
SB_DEAD_DROP = 104.0
SB_FIRST_SWEEP = 3
SB_LOOP_SWEEP = 2
MLA_Q_HEADS_PER_DOT = 4
MLA_TILE = 512
MLA_HEADS_PER_STEP = 4
MLA_BLOCK_TILES = 4
VMEM_LIMIT = 56 * 1024 * 1024
MASK_VALUE = -1e30
LOG2_E = math.log2(math.e)

F32 = jnp.float32
BF16 = jnp.bfloat16


def _rms(x, g):
    return x * lax.rsqrt(jnp.mean(x * x, axis=-1, keepdims=True) + NORM_EPS) * g


def _dot(a, b):
    return jnp.dot(a, b, preferred_element_type=F32)


def _dot_nt(a, b):
    return lax.dot_general(a, b, (((1,), (1,)), ((), ())), preferred_element_type=F32)


def _params():
    return pltpu.CompilerParams(
        dimension_semantics=None, vmem_limit_bytes=VMEM_LIMIT)


def _resident(shape):
    return pl.BlockSpec(shape, lambda *_: (0,) * len(shape),
                        pipeline_mode=pl.Buffered(1))


def _norm_matmul_kernel(x_ref, g_ref, w_ref, o_ref, *, col_chunk):
    h = _rms(x_ref[...], g_ref[...]).astype(BF16)
    n_out = o_ref.shape[1]
    for c in range(n_out // col_chunk):
        sl = slice(c * col_chunk, (c + 1) * col_chunk)
        o_ref[:, sl] = _dot(h, w_ref[:, sl]).astype(o_ref.dtype)


def _norm_matmul(x, g, w):
    n, d = x.shape
    n_out = w.shape[1]
    return pl.pallas_call(
        functools.partial(_norm_matmul_kernel, col_chunk=COL_CHUNK),
        grid=(n // ROW_TILE,),
        in_specs=[pl.BlockSpec((ROW_TILE, d), lambda i: (i, 0)),
                  _resident((1, d)),
                  _resident((d, n_out))],
        out_specs=pl.BlockSpec((ROW_TILE, n_out), lambda i: (i, 0)),
        out_shape=jax.ShapeDtypeStruct((n, n_out), BF16),
        compiler_params=_params(),
        name="norm_qkv",
    )(x, g, w)


def _sb_attn_kernel(q_ref, k_ref, v_ref, o_ref, *, tile, heads):
    qi = pl.program_id(2)
    pairs = heads // 2
    lane = lax.broadcasted_iota(jnp.int32, (1, LANES), 1)
    halves = (lane < SB_HEAD_DIM, lane >= SB_HEAD_DIM)
    row = lax.broadcasted_iota(jnp.int32, (tile, tile), 0)
    col = lax.broadcasted_iota(jnp.int32, (tile, tile), 1)
    from_key = (row >= col).astype(BF16)
    causal = col < row
    causal2 = jnp.concatenate([causal, causal], axis=0)

    def masked(x, keep):
        return jnp.where(keep, x, jnp.zeros_like(x))

    def blk(j):
        return slice(j * LANES, (j + 1) * LANES)

    q_pairs = []
    for j in range(pairs):
        q = q_ref[0, :, blk(j)] * (SB_HEAD_DIM ** -0.5)
        q_pairs.append(jnp.concatenate([masked(q, halves[0]), masked(q, halves[1])], axis=0))

    def sweep(kb0, count, accs, cs, diag_first):
        kbs = [kb0 - i for i in range(count)]
        starts = [pl.multiple_of(jnp.maximum(kb, 0) * tile, tile) for kb in kbs]
        zs = [[_dot_nt(q_pairs[j], k_ref[0, pl.ds(st, tile), blk(j)])
               for j in range(pairs)] for st in starts]
        splits = []
        for i in range(count):
            for z in zs[i]:
                drop = jnp.maximum(z, 0.0) + jnp.log(1.0 + jnp.exp2(jnp.abs(z) * -LOG2_E))
                if diag_first and i == 0:
                    drop = jnp.where(causal2, drop, 0.0)
                splits.append(drop.astype(BF16))
        totals = _dot(jnp.concatenate(splits, axis=0), from_key)
        accs, cs = list(accs), list(cs)
        for i in range(count):
            for j in range(pairs):
                base = (i * pairs + j) * 2 * tile
                total = totals[base:base + 2 * tile]
                v = v_ref[0, pl.ds(starts[i], tile), blk(j)]
                if diag_first and i == 0:
                    w = jnp.where(causal2, jnp.exp(zs[i][j] - total), 0.0)
                    cs[j] = total[:, 0:1]
                else:
                    w = jnp.exp(zs[i][j] - total - cs[j])
                    cs[j] = cs[j] + total[:, 0:1]
                    v = jnp.where(kbs[i] >= 0, v, jnp.zeros_like(v))
                accs[j] = accs[j] + _dot(
                    jnp.concatenate([w[:tile], w[tile:]], axis=1).astype(BF16),
                    jnp.concatenate([masked(v, halves[0]), masked(v, halves[1])], axis=0))
        return tuple(accs), tuple(cs)

    def alive_flag(cs):
        low = functools.reduce(jnp.minimum, cs)
        return (jnp.min(low) < SB_DEAD_DROP).astype(jnp.int32)

    accs = tuple(jnp.zeros((tile, LANES), F32) for _ in range(pairs))
    cs = tuple(jnp.zeros((2 * tile, 1), F32) for _ in range(pairs))
    accs, cs = sweep(qi, SB_FIRST_SWEEP, accs, cs, True)

    def cond(state):
        kb, _, _, alive = state
        return jnp.logical_and(kb >= 0, alive > 0)

    def body(state):
        kb, accs, cs, _ = state
        accs, cs = sweep(kb, SB_LOOP_SWEEP, accs, cs, False)
        return kb - SB_LOOP_SWEEP, accs, cs, alive_flag(cs)

    _, accs, _, _ = lax.while_loop(
        cond, body, (qi - SB_FIRST_SWEEP, accs, cs, alive_flag(cs)))
    for j, acc in enumerate(accs):
        o_ref[0, :, blk(j)] = acc.astype(o_ref.dtype)


def _sb_attention(qkv, batch, seq):
    width = SB_HEADS_PER_STEP * SB_HEAD_DIM
    groups = D_MODEL // width
    t = SB_TILE
    return pl.pallas_call(
        functools.partial(_sb_attn_kernel, tile=t, heads=SB_HEADS_PER_STEP),
        grid=(batch, groups, seq // t),
        in_specs=[pl.BlockSpec((1, t, width), lambda b, p, i: (b, i, p)),
                  pl.BlockSpec((1, seq, width), lambda b, p, i: (b, 0, groups + p),
                               pipeline_mode=pl.Buffered(1)),
                  pl.BlockSpec((1, seq, width), lambda b, p, i: (b, 0, 2 * groups + p),
                               pipeline_mode=pl.Buffered(1))],
        out_specs=pl.BlockSpec((1, t, width), lambda b, p, i: (b, i, p)),
        out_shape=jax.ShapeDtypeStruct((batch, seq, D_MODEL), BF16),
        compiler_params=_params(),
        name="sb_attention",
    )(qkv, qkv, qkv)


def _proj_mlp_kernel(x_ref, a_ref, wo_ref, g_ref, w1_ref, w2_ref, *rest, ff_chunk, final):
    if final:
        gf_ref, o_ref = rest
    else:
        (o_ref,) = rest
    x1 = x_ref[...] + _dot(a_ref[...], wo_ref[...])
    h = _rms(x1, g_ref[...]).astype(BF16)
    acc = x1
    for c in range(D_FF // ff_chunk):
        sl = slice(c * ff_chunk, (c + 1) * ff_chunk)
        u = jnp.maximum(_dot(h, w1_ref[:, sl]), 0.0)
        acc = acc + _dot((u * u).astype(BF16), w2_ref[sl, :])
    if final:
        acc = _rms(acc, gf_ref[...])
    o_ref[...] = acc


def _proj_mlp(x, a, wo, g, w1, w2, final_g=None):
    n, d = x.shape
    final = final_g is not None
    row = lambda i: (i, 0)
    in_specs = [pl.BlockSpec((ROW_TILE, d), row),
                pl.BlockSpec((ROW_TILE, a.shape[1]), row),
                _resident(wo.shape), _resident((1, d)),
                _resident(w1.shape), _resident(w2.shape)]
    args = [x, a, wo, g, w1, w2]
    if final:
        in_specs.append(_resident((1, d)))
        args.append(final_g)
    return pl.pallas_call(
        functools.partial(_proj_mlp_kernel, ff_chunk=COL_CHUNK, final=final),
        grid=(n // ROW_TILE,),
        in_specs=in_specs,
        out_specs=pl.BlockSpec((ROW_TILE, d), row),
        out_shape=jax.ShapeDtypeStruct((n, d), F32),
        compiler_params=_params(),
        name="proj_mlp_final" if final else "proj_mlp",
    )(*args)


def _rope_tables(pos, inv_lane):
    lane = lax.broadcasted_iota(jnp.int32, (1, LANES), 1)
    rope = jnp.logical_and(lane >= MLA_NOPE_DIM, lane < MLA_NOPE_DIM + MLA_ROPE_DIM)
    ang = pos * inv_lane
    cos_t = jnp.where(lane < MLA_NOPE_DIM, 1.0, jnp.where(rope, jnp.cos(ang), 0.0))
    sin_t = jnp.where(rope, jnp.sin(ang), 0.0)
    return cos_t, sin_t


def _shared_kv_kernel(x_ref, pos_ref, inv_ref, g_ref, wd_ref, gl_ref, wk_ref, wv_ref,
                      k_ref, v_ref, cos_ref, sin_ref):
    h = _rms(x_ref[...], g_ref[...]).astype(BF16)
    down = _dot(h, wd_ref[...])
    c = _rms(down[:, :MLA_KV_RANK], gl_ref[...]).astype(BF16)
    cos_t, sin_t = _rope_tables(pos_ref[...], inv_ref[...])
    cos_ref[...] = cos_t
    sin_ref[...] = sin_t
    k_rope = (down[:, MLA_KV_RANK:MLA_KV_RANK + LANES] * cos_t
              + down[:, MLA_KV_RANK + LANES:] * sin_t)
    k_nope = _dot(c, wk_ref[...])
    for hd in range(MLA_HEADS):
        sl = slice(hd * LANES, (hd + 1) * LANES)
        k_ref[:, sl] = (k_nope[:, sl] + k_rope).astype(k_ref.dtype)
    v_ref[...] = _dot(c, wv_ref[...]).astype(v_ref.dtype)


def _shared_kv(x, pos, inv_lane, g, wd, gl, wk, wv):
    n, d = x.shape
    row = lambda i: (i, 0)
    return pl.pallas_call(
        _shared_kv_kernel,
        grid=(n // ROW_TILE,),
        in_specs=[pl.BlockSpec((ROW_TILE, d), row),
                  pl.BlockSpec((ROW_TILE, 1), row),
                  _resident((1, LANES)), _resident((1, d)), _resident(wd.shape),
                  _resident((1, MLA_KV_RANK)), _resident(wk.shape), _resident(wv.shape)],
        out_specs=[pl.BlockSpec((ROW_TILE, MLA_HEADS * LANES), row),
                   pl.BlockSpec((ROW_TILE, MLA_HEADS * MLA_V_DIM), row),
                   pl.BlockSpec((ROW_TILE, LANES), row),
                   pl.BlockSpec((ROW_TILE, LANES), row)],
        out_shape=[jax.ShapeDtypeStruct((n, MLA_HEADS * LANES), BF16),
                   jax.ShapeDtypeStruct((n, MLA_HEADS * MLA_V_DIM), BF16),
                   jax.ShapeDtypeStruct((n, LANES), F32),
                   jax.ShapeDtypeStruct((n, LANES), F32)],
        compiler_params=_params(),
        name="mla_shared_kv",
    )(x, pos, inv_lane, g, wd, gl, wk, wv)


def _mla_q_kernel(x_ref, cos_ref, sin_ref, g_ref, wdq_ref, gq_ref, wa_ref, q_ref,
                  *, heads_per_chunk):
    h = _rms(x_ref[...], g_ref[...]).astype(BF16)
    cq = _rms(_dot(h, wdq_ref[...]), gq_ref[...]).astype(BF16)
    cos_t, sin_t = cos_ref[...], sin_ref[...]
    scale = (MLA_NOPE_DIM + MLA_ROPE_DIM) ** -0.5 * math.log2(math.e)
    cos_t = cos_t * scale
    lane = lax.broadcasted_iota(jnp.int32, (1, LANES), 1)
    low_half = lane < MLA_NOPE_DIM + MLA_ROPE_DIM // 2
    sin_t = jnp.where(low_half, -sin_t, sin_t) * scale
    half = MLA_ROPE_DIM // 2
    width = heads_per_chunk * LANES
    for c in range(MLA_HEADS // heads_per_chunk):
        qa = _dot(cq, wa_ref[:, c * width:(c + 1) * width])
        for j in range(heads_per_chunk):
            qh = qa[:, j * LANES:(j + 1) * LANES]
            partner = pltpu.roll(qh, LANES - half, 1)
            q_ref[:, c * width + j * LANES:c * width + (j + 1) * LANES] = (
                qh * cos_t + partner * sin_t).astype(q_ref.dtype)


def _mla_q(x, cos_t, sin_t, g, wdq, gq, wa):
    n, d = x.shape
    row = lambda i: (i, 0)
    return pl.pallas_call(
        functools.partial(_mla_q_kernel, heads_per_chunk=MLA_Q_HEADS_PER_DOT),
        grid=(n // ROW_TILE,),
        in_specs=[pl.BlockSpec((ROW_TILE, d), row),
                  pl.BlockSpec((ROW_TILE, LANES), row),
                  pl.BlockSpec((ROW_TILE, LANES), row), _resident((1, d)), _resident(wdq.shape),
                  _resident((1, MLA_Q_RANK)), _resident(wa.shape)],
        out_specs=pl.BlockSpec((ROW_TILE, MLA_HEADS * LANES), row),
        out_shape=jax.ShapeDtypeStruct((n, MLA_HEADS * LANES), BF16),
        compiler_params=_params(),
        name="mla_q",
    )(x, cos_t, sin_t, g, wdq, gq, wa)


def _mla_attn_kernel(q_ref, k_ref, v_ref, o_ref, *, tile, heads):
    qi = pl.program_id(2)
    pairs = heads // 2
    lane = lax.broadcasted_iota(jnp.int32, (1, LANES), 1)
    halves = (lane < MLA_V_DIM, lane >= MLA_V_DIM)
    row = lax.broadcasted_iota(jnp.int32, (tile, tile), 0)
    col = lax.broadcasted_iota(jnp.int32, (tile, tile), 1)
    allowed = (col // CHUNK) <= (row // CHUNK)

    def masked(x, keep):
        return jnp.where(keep, x, jnp.zeros_like(x))

    def blk(j):
        return slice(j * LANES, (j + 1) * LANES)

    key_row = lax.broadcasted_iota(jnp.int32, (2 * tile, 1), 0)
    ones_cols = jnp.where((key_row < tile) == halves[0], 1.0, 0.0).astype(BF16)

    def sweep(kbs, diag_last, carry):
        ms, accs = carry
        starts = [pl.multiple_of(kb * tile, tile) for kb in kbs]
        ss = [[_dot_nt(q_ref[0, :, blk(h)], k_ref[0, pl.ds(st, tile), blk(h)])
               for st in starts] for h in range(heads)]
        if diag_last:
            for s_head in ss:
                s_head[-1] = jnp.where(allowed, s_head[-1], MASK_VALUE)
        count = len(kbs)
        ps, alphas, new_ms = [], [], []
        for h in range(heads):
            top = functools.reduce(jnp.maximum, ss[h])
            m_new = jnp.maximum(ms[h], jnp.max(top, axis=-1, keepdims=True))
            alphas.append(jnp.exp2(ms[h] - m_new))
            new_ms.append(m_new)
            ps.append([jnp.exp2(s - m_new).astype(BF16) for s in ss[h]])
        new_accs = []
        for j in range(pairs):
            lhs, rhs = [], []
            for n, st in enumerate(starts):
                v = v_ref[0, pl.ds(st, tile), blk(j)]
                vals = jnp.concatenate([masked(v, halves[0]), masked(v, halves[1])], axis=0)
                lhs += [ps[2 * j][n], ps[2 * j + 1][n]]
                rhs.append(jnp.concatenate([vals, ones_cols], axis=1))
            pv = _dot(jnp.concatenate(lhs, axis=1),
                      jnp.concatenate(rhs, axis=0))
            alpha = jnp.where(halves[0], alphas[2 * j], alphas[2 * j + 1])
            new_accs.append(jnp.concatenate([alpha, alpha], axis=1) * accs[j] + pv)
        return tuple(new_ms), tuple(new_accs)

    carry = (tuple(jnp.full((tile, 1), MASK_VALUE, F32) for _ in range(heads)),
             tuple(jnp.zeros((tile, 2 * LANES), F32) for _ in range(pairs)))
    nblk = MLA_BLOCK_TILES
    carry = lax.fori_loop(
        0, qi // nblk, lambda i, cr: sweep([nblk * i + t for t in range(nblk)], False, cr), carry)

    def last_block(rest):
        return lambda cr: sweep([qi - rest + t for t in range(rest)] + [qi], True, cr)

    _, accs = lax.switch(qi % nblk, [last_block(rest) for rest in range(nblk)], carry)
    for j in range(pairs):
        o_ref[0, :, blk(j)] = (accs[j][:, :LANES] / accs[j][:, LANES:]).astype(o_ref.dtype)


def _mla_attention(q, k, v, batch, seq):
    groups = MLA_HEADS // MLA_HEADS_PER_STEP
    t = MLA_TILE
    qk_width = MLA_HEADS_PER_STEP * LANES
    v_width = MLA_HEADS_PER_STEP * MLA_V_DIM
    return pl.pallas_call(
        functools.partial(_mla_attn_kernel, tile=t, heads=MLA_HEADS_PER_STEP),
        grid=(batch, groups, seq // t),
        in_specs=[pl.BlockSpec((1, t, qk_width), lambda b, p, i: (b, i, p)),
                  pl.BlockSpec((1, seq, qk_width), lambda b, p, i: (b, 0, p)),
                  pl.BlockSpec((1, seq, v_width), lambda b, p, i: (b, 0, p))],
        out_specs=pl.BlockSpec((1, t, v_width), lambda b, p, i: (b, i, p)),
        out_shape=jax.ShapeDtypeStruct((batch, seq, MLA_HEADS * MLA_V_DIM), BF16),
        compiler_params=_params(),
        name="mla_attention",
    )(q, k, v)


def _rope_lane_freqs():
    inv_freq = ROPE_THETA ** (-jnp.arange(0, MLA_ROPE_DIM, 2, dtype=F32) / MLA_ROPE_DIM)
    half = MLA_ROPE_DIM // 2
    out = jnp.zeros((1, LANES), F32)
    out = out.at[0, MLA_NOPE_DIM:MLA_NOPE_DIM + half].set(inv_freq)
    out = out.at[0, MLA_NOPE_DIM + half:MLA_NOPE_DIM + 2 * half].set(inv_freq)
    return out


def _layout_w_dkv(w_dkv):
    half = MLA_ROPE_DIM // 2
    d = w_dkv.shape[0]
    t1 = w_dkv[:, MLA_KV_RANK:MLA_KV_RANK + half]
    t2 = w_dkv[:, MLA_KV_RANK + half:]
    pad_lo = jnp.zeros((d, MLA_NOPE_DIM), w_dkv.dtype)
    pad_hi = jnp.zeros((d, LANES - MLA_NOPE_DIM - MLA_ROPE_DIM), w_dkv.dtype)
    blk_a = jnp.concatenate([pad_lo, t1, t2, pad_hi], axis=1)
    blk_b = jnp.concatenate([pad_lo, -t2, t1, pad_hi], axis=1)
    return jnp.concatenate([w_dkv[:, :MLA_KV_RANK], blk_a, blk_b], axis=1).astype(BF16)


def _layout_w_ukv(w_ukv):
    r = w_ukv.shape[0]
    w = w_ukv.reshape(r, MLA_HEADS, MLA_NOPE_DIM + MLA_V_DIM)
    wk = jnp.concatenate(
        [w[:, :, :MLA_NOPE_DIM], jnp.zeros((r, MLA_HEADS, LANES - MLA_NOPE_DIM), w.dtype)],
        axis=2).reshape(r, MLA_HEADS * LANES)
    wv = w[:, :, MLA_NOPE_DIM:].reshape(r, MLA_HEADS * MLA_V_DIM)
    return wk.astype(BF16), wv.astype(BF16)


def _layout_w_uq(w_uq):
    r = w_uq.shape[0]
    half = MLA_ROPE_DIM // 2
    w = w_uq.reshape(r, MLA_HEADS, MLA_NOPE_DIM + MLA_ROPE_DIM)
    t1 = w[:, :, MLA_NOPE_DIM:MLA_NOPE_DIM + half]
    pad_hi = jnp.zeros((r, MLA_HEADS, LANES - MLA_NOPE_DIM - MLA_ROPE_DIM - half), w.dtype)
    return jnp.concatenate([w, t1, pad_hi], axis=2).reshape(r, MLA_HEADS * LANES).astype(BF16)


def kernel(x, positions, attn_norm, mlp_norm, sb_w_qkv, sb_w_o, kv_norm, mla_w_dkv,
           mla_kv_lat_norm, mla_w_ukv, mla_w_dq, mla_q_lat_norm, mla_w_uq, mla_w_o,
           mlp_w1, mlp_w2, final_norm):
    batch, seq, d = x.shape
    n = batch * seq
    assert d == D_MODEL and n % ROW_TILE == 0
    assert seq % MLA_TILE == 0 and seq % SB_TILE == 0
    xs = x.reshape(n, d)
    pos = positions.astype(F32).reshape(n, 1)
    inv_lane = _rope_lane_freqs()

    for layer in range(N_A_LAYERS):
        qkv = _norm_matmul(xs, attn_norm[layer].reshape(1, d), sb_w_qkv[layer].astype(BF16))
        attn = _sb_attention(qkv.reshape(batch, seq, 3 * d), batch, seq)
        xs = _proj_mlp(xs, attn.reshape(n, d), sb_w_o[layer].astype(BF16),
                       mlp_norm[layer].reshape(1, d),
                       mlp_w1[layer].astype(BF16), mlp_w2[layer].astype(BF16))

    wk, wv = _layout_w_ukv(mla_w_ukv)
    k_all, v_all, cos_t, sin_t = _shared_kv(
        xs, pos, inv_lane, kv_norm.reshape(1, d), _layout_w_dkv(mla_w_dkv),
        mla_kv_lat_norm.reshape(1, MLA_KV_RANK), wk, wv)
    k_all = k_all.reshape(batch, seq, MLA_HEADS * LANES)
    v_all = v_all.reshape(batch, seq, MLA_HEADS * MLA_V_DIM)

    for layer in range(N_A_LAYERS, DEPTH):
        j = layer - N_A_LAYERS
        q = _mla_q(xs, cos_t, sin_t, attn_norm[layer].reshape(1, d),
                   mla_w_dq[j].astype(BF16), mla_q_lat_norm[j].reshape(1, MLA_Q_RANK),
                   _layout_w_uq(mla_w_uq[j]))
        attn = _mla_attention(q.reshape(batch, seq, MLA_HEADS * LANES), k_all, v_all, batch, seq)
        xs = _proj_mlp(xs, attn.reshape(n, MLA_HEADS * MLA_V_DIM), mla_w_o[j].astype(BF16),
                       mlp_norm[layer].reshape(1, d),
                       mlp_w1[layer].astype(BF16), mlp_w2[layer].astype(BF16),
                       final_g=final_norm.reshape(1, d) if layer == DEPTH - 1 else None)
    return xs.reshape(batch, seq, d)
```

```python
import functools
import math

import jax
import jax.numpy as jnp
from jax import lax
from jax.experimental import pallas as pl
from jax.experimental.pallas import tpu as pltpu

D_MODEL = 1024
DEPTH = 4
CHUNK = 64
N_A_LAYERS = DEPTH // 2
SB_HEADS = 16
SB_HEAD_DIM = D_MODEL // SB_HEADS
MLA_HEADS = 16
MLA_NOPE_DIM = 64
MLA_ROPE_DIM = 32
MLA_V_DIM = 64
MLA_Q_RANK = 384
MLA_KV_RANK = 256
D_FF = 4 * D_MODEL
ROPE_THETA = 10000.0
NORM_EPS = 1e-6

LANES = 128
ROW_TILE = 1024
COL_CHUNK = 1024
SB_TILE = 128
SB_HEADS_PER_STEP = 16
SB_DEAD_DROP = 104.0
SB_FIRST_SWEEP = 3
SB_LOOP_SWEEP = 2
MLA_Q_HEADS_PER_DOT = 4
MLA_TILE = 512
MLA_HEADS_PER_STEP = 4
MLA_BLOCK_TILES = 4
VMEM_LIMIT = 56 * 1024 * 1024
MASK_VALUE = -1e30
LOG2_E = math.log2(math.e)

F32 = jnp.float32
BF16 = jnp.bfloat16


def _rms(x, g):
    return x * lax.rsqrt(jnp.mean(x * x, axis=-1, keepdims=True) + NORM_EPS) * g


def _dot(a, b):
    return jnp.dot(a, b, preferred_element_type=F32)


def _dot_nt(a, b):
    return lax.dot_general(a, b, (((1,), (1,)), ((), ())), preferred_element_type=F32)


def _params():
    return pltpu.CompilerParams(
        dimension_semantics=None, vmem_limit_bytes=VMEM_LIMIT)


def _resident(shape):
    return pl.BlockSpec(shape, lambda *_: (0,) * len(shape),
                        pipeline_mode=pl.Buffered(1))


def _norm_matmul_kernel(x_ref, g_ref, w_ref, o_ref, *, col_chunk):
    h = _rms(x_ref[...], g_ref[...]).astype(BF16)
    n_out = o_ref.shape[1]
    for c in range(n_out // col_chunk):
        sl = slice(c * col_chunk, (c + 1) * col_chunk)
        o_ref[:, sl] = _dot(h, w_ref[:, sl]).astype(o_ref.dtype)


def _norm_matmul(x, g, w):
    n, d = x.shape
    n_out = w.shape[1]
    return pl.pallas_call(
        functools.partial(_norm_matmul_kernel, col_chunk=COL_CHUNK),
        grid=(n // ROW_TILE,),
        in_specs=[pl.BlockSpec((ROW_TILE, d), lambda i: (i, 0)),
                  _resident((1, d)),
                  _resident((d, n_out))],
        out_specs=pl.BlockSpec((ROW_TILE, n_out), lambda i: (i, 0)),
        out_shape=jax.ShapeDtypeStruct((n, n_out), BF16),
        compiler_params=_params(),
        name="norm_qkv",
    )(x, g, w)


def _sb_attn_kernel(q_ref, k_ref, v_ref, o_ref, *, tile, heads):
    qi = pl.program_id(2)
    pairs = heads // 2
    lane = lax.broadcasted_iota(jnp.int32, (1, LANES), 1)
    halves = (lane < SB_HEAD_DIM, lane >= SB_HEAD_DIM)
    row = lax.broadcasted_iota(jnp.int32, (tile, tile), 0)
    col = lax.broadcasted_iota(jnp.int32, (tile, tile), 1)
    from_key = (row >= col).astype(BF16)
    none = jnp.zeros_like(from_key)
    from_key_pair = jnp.concatenate([jnp.concatenate([from_key, none], axis=1),
                                     jnp.concatenate([none, from_key], axis=1)], axis=0)
    causal = col < row
    causal2 = jnp.concatenate([causal, causal], axis=0)

    def masked(x, keep):
        return jnp.where(keep, x, jnp.zeros_like(x))

    def blk(j):
        return slice(j * LANES, (j + 1) * LANES)

    q_pairs = []
    for j in range(pairs):
        q = q_ref[0, :, blk(j)] * (SB_HEAD_DIM ** -0.5)
        q_pairs.append(jnp.concatenate([masked(q, halves[0]), masked(q, halves[1])], axis=0))

    def sweep(kb0, count, accs, cs, diag_first):
        kbs = [kb0 - i for i in range(count)]
        starts = [pl.multiple_of(jnp.maximum(kb, 0) * tile, tile) for kb in kbs]
        zs = []
        for j in range(pairs):
            keys = jnp.concatenate([k_ref[0, pl.ds(st, tile), blk(j)] for st in starts], axis=0)
            z_all = _dot_nt(q_pairs[j], keys)
            zs.append([z_all[:, i * tile:(i + 1) * tile] for i in range(count)])
        drops = []
        for j in range(pairs):
            for i in range(count):
                z = zs[j][i]
                drop = jnp.maximum(z, 0.0) + jnp.log(1.0 + jnp.exp2(jnp.abs(z) * -LOG2_E))
                if diag_first and i == 0:
                    drop = jnp.where(causal2, drop, 0.0)
                drops.append(drop.astype(BF16))
        two = [jnp.concatenate(drops[j * count + i:j * count + i + 2], axis=1)
               for j in range(pairs) for i in range(0, count - 1, 2)]
        two_totals = _dot(jnp.concatenate(two, axis=0), from_key_pair)
        if count % 2:
            one = [drops[j * count + count - 1] for j in range(pairs)]
            one_totals = _dot(jnp.concatenate(one, axis=0), from_key)
        accs, cs = list(accs), list(cs)
        for j in range(pairs):
            for i in range(count):
                if i < count - count % 2:
                    base = (j * (count // 2) + i // 2) * 2 * tile
                    total = two_totals[base:base + 2 * tile, (i % 2) * tile:(i % 2 + 1) * tile]
                else:
                    total = one_totals[j * 2 * tile:(j + 1) * 2 * tile]
                v = v_ref[0, pl.ds(starts[i], tile), blk(j)]
                if diag_first and i == 0:
                    w = jnp.where(causal2, jnp.exp(zs[j][i] - total), 0.0)
                    cs[j] = total[:, 0:1]
                else:
                    w = jnp.exp(zs[j][i] - total - cs[j])
                    cs[j] = cs[j] + total[:, 0:1]
                    v = jnp.where(kbs[i] >= 0, v, jnp.zeros_like(v))
                accs[j] = accs[j] + _dot(
                    jnp.concatenate([w[:tile], w[tile:]], axis=1).astype(BF16),
                    jnp.concatenate([masked(v, halves[0]), masked(v, halves[1])], axis=0))
        return tuple(accs), tuple(cs)

    def alive_flag(cs):
        low = functools.reduce(jnp.minimum, cs)
        return (jnp.min(low) < SB_DEAD_DROP).astype(jnp.int32)

    accs = tuple(jnp.zeros((tile, LANES), F32) for _ in range(pairs))
    cs = tuple(jnp.zeros((2 * tile, 1), F32) for _ in range(pairs))
    accs, cs = sweep(qi, SB_FIRST_SWEEP, accs, cs, True)

    def cond(state):
        kb, _, _, alive = state
        return jnp.logical_and(kb >= 0, alive > 0)

    def body(state):
        kb, accs, cs, _ = state
        accs, cs = sweep(kb, SB_LOOP_SWEEP, accs, cs, False)
        return kb - SB_LOOP_SWEEP, accs, cs, alive_flag(cs)

    _, accs, _, _ = lax.while_loop(
        cond, body, (qi - SB_FIRST_SWEEP, accs, cs, alive_flag(cs)))
    for j, acc in enumerate(accs):
        o_ref[0, :, blk(j)] = acc.astype(o_ref.dtype)


def _sb_attention(qkv, batch, seq):
    width = SB_HEADS_PER_STEP * SB_HEAD_DIM
    groups = D_MODEL // width
    t = SB_TILE
    return pl.pallas_call(
        functools.partial(_sb_attn_kernel, tile=t, heads=SB_HEADS_PER_STEP),
        grid=(batch, groups, seq // t),
        in_specs=[pl.BlockSpec((1, t, width), lambda b, p, i: (b, i, p)),
                  pl.BlockSpec((1, seq, width), lambda b, p, i: (b, 0, groups + p),
                               pipeline_mode=pl.Buffered(1)),
                  pl.BlockSpec((1, seq, width), lambda b, p, i: (b, 0, 2 * groups + p),
                               pipeline_mode=pl.Buffered(1))],
        out_specs=pl.BlockSpec((1, t, width), lambda b, p, i: (b, i, p)),
        out_shape=jax.ShapeDtypeStruct((batch, seq, D_MODEL), BF16),
        compiler_params=_params(),
        name="sb_attention",
    )(qkv, qkv, qkv)


def _proj_mlp_kernel(x_ref, a_ref, wo_ref, g_ref, w1_ref, w2_ref, *rest, ff_chunk, final):
    if final:
        gf_ref, o_ref = rest
    else:
        (o_ref,) = rest
    x1 = x_ref[...] + _dot(a_ref[...], wo_ref[...])
    h = _rms(x1, g_ref[...]).astype(BF16)
    acc = x1
    for c in range(D_FF // ff_chunk):
        sl = slice(c * ff_chunk, (c + 1) * ff_chunk)
        u = jnp.maximum(_dot(h, w1_ref[:, sl]), 0.0)
        acc = acc + _dot((u * u).astype(BF16), w2_ref[sl, :])
    if final:
        acc = _rms(acc, gf_ref[...])
    o_ref[...] = acc


def _proj_mlp(x, a, wo, g, w1, w2, final_g=None):
    n, d = x.shape
    final = final_g is not None
    row = lambda i: (i, 0)
    in_specs = [pl.BlockSpec((ROW_TILE, d), row),
                pl.BlockSpec((ROW_TILE, a.shape[1]), row),
                _resident(wo.shape), _resident((1, d)),
                _resident(w1.shape), _resident(w2.shape)]
    args = [x, a, wo, g, w1, w2]
    if final:
        in_specs.append(_resident((1, d)))
        args.append(final_g)
    return pl.pallas_call(
        functools.partial(_proj_mlp_kernel, ff_chunk=COL_CHUNK, final=final),
        grid=(n // ROW_TILE,),
        in_specs=in_specs,
        out_specs=pl.BlockSpec((ROW_TILE, d), row),
        out_shape=jax.ShapeDtypeStruct((n, d), F32),
        compiler_params=_params(),
        name="proj_mlp_final" if final else "proj_mlp",
    )(*args)


def _rope_tables(pos, inv_lane):
    lane = lax.broadcasted_iota(jnp.int32, (1, LANES), 1)
    rope = jnp.logical_and(lane >= MLA_NOPE_DIM, lane < MLA_NOPE_DIM + MLA_ROPE_DIM)
    ang = pos * inv_lane
    cos_t = jnp.where(lane < MLA_NOPE_DIM, 1.0, jnp.where(rope, jnp.cos(ang), 0.0))
    sin_t = jnp.where(rope, jnp.sin(ang), 0.0)
    return cos_t, sin_t


def _shared_kv_kernel(x_ref, pos_ref, inv_ref, g_ref, wd_ref, gl_ref, wk_ref, wv_ref,
                      k_ref, v_ref, cos_ref, sin_ref):
    h = _rms(x_ref[...], g_ref[...]).astype(BF16)
    down = _dot(h, wd_ref[...])
    c = _rms(down[:, :MLA_KV_RANK], gl_ref[...]).astype(BF16)
    cos_t, sin_t = _rope_tables(pos_ref[...], inv_ref[...])
    cos_ref[...] = cos_t
    sin_ref[...] = sin_t
    k_rope = (down[:, MLA_KV_RANK:MLA_KV_RANK + LANES] * cos_t
              + down[:, MLA_KV_RANK + LANES:] * sin_t)
    k_nope = _dot(c, wk_ref[...])
    for hd in range(MLA_HEADS):
        sl = slice(hd * LANES, (hd + 1) * LANES)
        k_ref[:, sl] = (k_nope[:, sl] + k_rope).astype(k_ref.dtype)
    v_ref[...] = _dot(c, wv_ref[...]).astype(v_ref.dtype)


def _shared_kv(x, pos, inv_lane, g, wd, gl, wk, wv):
    n, d = x.shape
    row = lambda i: (i, 0)
    return pl.pallas_call(
        _shared_kv_kernel,
        grid=(n // ROW_TILE,),
        in_specs=[pl.BlockSpec((ROW_TILE, d), row),
                  pl.BlockSpec((ROW_TILE, 1), row),
                  _resident((1, LANES)), _resident((1, d)), _resident(wd.shape),
                  _resident((1, MLA_KV_RANK)), _resident(wk.shape), _resident(wv.shape)],
        out_specs=[pl.BlockSpec((ROW_TILE, MLA_HEADS * LANES), row),
                   pl.BlockSpec((ROW_TILE, MLA_HEADS * MLA_V_DIM), row),
                   pl.BlockSpec((ROW_TILE, LANES), row),
                   pl.BlockSpec((ROW_TILE, LANES), row)],
        out_shape=[jax.ShapeDtypeStruct((n, MLA_HEADS * LANES), BF16),
                   jax.ShapeDtypeStruct((n, MLA_HEADS * MLA_V_DIM), BF16),
                   jax.ShapeDtypeStruct((n, LANES), F32),
                   jax.ShapeDtypeStruct((n, LANES), F32)],
        compiler_params=_params(),
        name="mla_shared_kv",
    )(x, pos, inv_lane, g, wd, gl, wk, wv)


def _mla_q_kernel(x_ref, cos_ref, sin_ref, g_ref, wdq_ref, gq_ref, wa_ref, q_ref,
                  *, heads_per_chunk):
    h = _rms(x_ref[...], g_ref[...]).astype(BF16)
    cq = _rms(_dot(h, wdq_ref[...]), gq_ref[...]).astype(BF16)
    cos_t, sin_t = cos_ref[...], sin_ref[...]
    scale = (MLA_NOPE_DIM + MLA_ROPE_DIM) ** -0.5 * math.log2(math.e)
    cos_t = cos_t * scale
    lane = lax.broadcasted_iota(jnp.int32, (1, LANES), 1)
    low_half = lane < MLA_NOPE_DIM + MLA_ROPE_DIM // 2
    sin_t = jnp.where(low_half, -sin_t, sin_t) * scale
    half = MLA_ROPE_DIM // 2
    width = heads_per_chunk * LANES
    for c in range(MLA_HEADS // heads_per_chunk):
        qa = _dot(cq, wa_ref[:, c * width:(c + 1) * width])
        for j in range(heads_per_chunk):
            qh = qa[:, j * LANES:(j + 1) * LANES]
            partner = pltpu.roll(qh, LANES - half, 1)
            q_ref[:, c * width + j * LANES:c * width + (j + 1) * LANES] = (
                qh * cos_t + partner * sin_t).astype(q_ref.dtype)


def _mla_q(x, cos_t, sin_t, g, wdq, gq, wa):
    n, d = x.shape
    row = lambda i: (i, 0)
    return pl.pallas_call(
        functools.partial(_mla_q_kernel, heads_per_chunk=MLA_Q_HEADS_PER_DOT),
        grid=(n // ROW_TILE,),
        in_specs=[pl.BlockSpec((ROW_TILE, d), row),
                  pl.BlockSpec((ROW_TILE, LANES), row),
                  pl.BlockSpec((ROW_TILE, LANES), row), _resident((1, d)), _resident(wdq.shape),
                  _resident((1, MLA_Q_RANK)), _resident(wa.shape)],
        out_specs=pl.BlockSpec((ROW_TILE, MLA_HEADS * LANES), row),
        out_shape=jax.ShapeDtypeStruct((n, MLA_HEADS * LANES), BF16),
        compiler_params=_params(),
        name="mla_q",
    )(x, cos_t, sin_t, g, wdq, gq, wa)


def _mla_attn_kernel(q_ref, k_ref, v_ref, o_ref, *, tile, heads):
    qi = pl.program_id(2)
    pairs = heads // 2
    lane = lax.broadcasted_iota(jnp.int32, (1, LANES), 1)
    halves = (lane < MLA_V_DIM, lane >= MLA_V_DIM)
    row = lax.broadcasted_iota(jnp.int32, (tile, tile), 0)
    col = lax.broadcasted_iota(jnp.int32, (tile, tile), 1)
    allowed = (col // CHUNK) <= (row // CHUNK)

    def masked(x, keep):
        return jnp.where(keep, x, jnp.zeros_like(x))

    def blk(j):
        return slice(j * LANES, (j + 1) * LANES)

    key_row = lax.broadcasted_iota(jnp.int32, (2 * tile, 1), 0)
    ones_cols = jnp.where((key_row < tile) == halves[0], 1.0, 0.0).astype(BF16)

    def sweep(kbs, diag_last, carry):
        ms, accs = carry
        starts = [pl.multiple_of(kb * tile, tile) for kb in kbs]
        ss = [[_dot_nt(q_ref[0, :, blk(h)], k_ref[0, pl.ds(st, tile), blk(h)])
               for st in starts] for h in range(heads)]
        if diag_last:
            for s_head in ss:
                s_head[-1] = jnp.where(allowed, s_head[-1], MASK_VALUE)
        count = len(kbs)
        ps, alphas, new_ms = [], [], []
        for h in range(heads):
            top = functools.reduce(jnp.maximum, ss[h])
            m_new = jnp.maximum(ms[h], jnp.max(top, axis=-1, keepdims=True))
            alphas.append(jnp.exp2(ms[h] - m_new))
            new_ms.append(m_new)
            ps.append([jnp.exp2(s - m_new).astype(BF16) for s in ss[h]])
        new_accs = []
        for j in range(pairs):
            lhs, rhs = [], []
            for n, st in enumerate(starts):
                v = v_ref[0, pl.ds(st, tile), blk(j)]
                vals = jnp.concatenate([masked(v, halves[0]), masked(v, halves[1])], axis=0)
                lhs += [ps[2 * j][n], ps[2 * j + 1][n]]
                rhs.append(jnp.concatenate([vals, ones_cols], axis=1))
            pv = _dot(jnp.concatenate(lhs, axis=1),
                      jnp.concatenate(rhs, axis=0))
            alpha = jnp.where(halves[0], alphas[2 * j], alphas[2 * j + 1])
            new_accs.append(jnp.concatenate([alpha, alpha], axis=1) * accs[j] + pv)
        return tuple(new_ms), tuple(new_accs)

    carry = (tuple(jnp.full((tile, 1), MASK_VALUE, F32) for _ in range(heads)),
             tuple(jnp.zeros((tile, 2 * LANES), F32) for _ in range(pairs)))
    nblk = MLA_BLOCK_TILES
    carry = lax.fori_loop(
        0, qi // nblk, lambda i, cr: sweep([nblk * i + t for t in range(nblk)], False, cr), carry)

    def last_block(rest):
        return lambda cr: sweep([qi - rest + t for t in range(rest)] + [qi], True, cr)

    _, accs = lax.switch(qi % nblk, [last_block(rest) for rest in range(nblk)], carry)
    for j in range(pairs):
        o_ref[0, :, blk(j)] = (accs[j][:, :LANES] / accs[j][:, LANES:]).astype(o_ref.dtype)


def _mla_attention(q, k, v, batch, seq):
    groups = MLA_HEADS // MLA_HEADS_PER_STEP
    t = MLA_TILE
    qk_width = MLA_HEADS_PER_STEP * LANES
    v_width = MLA_HEADS_PER_STEP * MLA_V_DIM
    return pl.pallas_call(
        functools.partial(_mla_attn_kernel, tile=t, heads=MLA_HEADS_PER_STEP),
        grid=(batch, groups, seq // t),
        in_specs=[pl.BlockSpec((1, t, qk_width), lambda b, p, i: (b, i, p)),
                  pl.BlockSpec((1, seq, qk_width), lambda b, p, i: (b, 0, p)),
                  pl.BlockSpec((1, seq, v_width), lambda b, p, i: (b, 0, p))],
        out_specs=pl.BlockSpec((1, t, v_width), lambda b, p, i: (b, i, p)),
        out_shape=jax.ShapeDtypeStruct((batch, seq, MLA_HEADS * MLA_V_DIM), BF16),
        compiler_params=_params(),
        name="mla_attention",
    )(q, k, v)


def _rope_lane_freqs():
    inv_freq = ROPE_THETA ** (-jnp.arange(0, MLA_ROPE_DIM, 2, dtype=F32) / MLA_ROPE_DIM)
    half = MLA_ROPE_DIM // 2
    out = jnp.zeros((1, LANES), F32)
    out = out.at[0, MLA_NOPE_DIM:MLA_NOPE_DIM + half].set(inv_freq)
    out = out.at[0, MLA_NOPE_DIM + half:MLA_NOPE_DIM + 2 * half].set(inv_freq)
    return out


def _layout_w_dkv(w_dkv):
    half = MLA_ROPE_DIM // 2
    d = w_dkv.shape[0]
    t1 = w_dkv[:, MLA_KV_RANK:MLA_KV_RANK + half]
    t2 = w_dkv[:, MLA_KV_RANK + half:]
    pad_lo = jnp.zeros((d, MLA_NOPE_DIM), w_dkv.dtype)
    pad_hi = jnp.zeros((d, LANES - MLA_NOPE_DIM - MLA_ROPE_DIM), w_dkv.dtype)
    blk_a = jnp.concatenate([pad_lo, t1, t2, pad_hi], axis=1)
    blk_b = jnp.concatenate([pad_lo, -t2, t1, pad_hi], axis=1)
    return jnp.concatenate([w_dkv[:, :MLA_KV_RANK], blk_a, blk_b], axis=1).astype(BF16)


def _layout_w_ukv(w_ukv):
    r = w_ukv.shape[0]
    w = w_ukv.reshape(r, MLA_HEADS, MLA_NOPE_DIM + MLA_V_DIM)
    wk = jnp.concatenate(
        [w[:, :, :MLA_NOPE_DIM], jnp.zeros((r, MLA_HEADS, LANES - MLA_NOPE_DIM), w.dtype)],
        axis=2).reshape(r, MLA_HEADS * LANES)
    wv = w[:, :, MLA_NOPE_DIM:].reshape(r, MLA_HEADS * MLA_V_DIM)
    return wk.astype(BF16), wv.astype(BF16)


def _layout_w_uq(w_uq):
    r = w_uq.shape[0]
    half = MLA_ROPE_DIM // 2
    w = w_uq.reshape(r, MLA_HEADS, MLA_NOPE_DIM + MLA_ROPE_DIM)
    t1 = w[:, :, MLA_NOPE_DIM:MLA_NOPE_DIM + half]
    pad_hi = jnp.zeros((r, MLA_HEADS, LANES - MLA_NOPE_DIM - MLA_ROPE_DIM - half), w.dtype)
    return jnp.concatenate([w, t1, pad_hi], axis=2).reshape(r, MLA_HEADS * LANES).astype(BF16)


def kernel(x, positions, attn_norm, mlp_norm, sb_w_qkv, sb_w_o, kv_norm, mla_w_dkv,
           mla_kv_lat_norm, mla_w_ukv, mla_w_dq, mla_q_lat_norm, mla_w_uq, mla_w_o,
           mlp_w1, mlp_w2, final_norm):
    batch, seq, d = x.shape
    n = batch * seq
    assert d == D_MODEL and n % ROW_TILE == 0
    assert seq % MLA_TILE == 0 and seq % SB_TILE == 0
    xs = x.reshape(n, d)
    pos = positions.astype(F32).reshape(n, 1)
    inv_lane = _rope_lane_freqs()

    for layer in range(N_A_LAYERS):
        qkv = _norm_matmul(xs, attn_norm[layer].reshape(1, d), sb_w_qkv[layer].astype(BF16))
        attn = _sb_attention(qkv.reshape(batch, seq, 3 * d), batch, seq)
        xs = _proj_mlp(xs, attn.reshape(n, d), sb_w_o[layer].astype(BF16),
                       mlp_norm[layer].reshape(1, d),
                       mlp_w1[layer].astype(BF16), mlp_w2[layer].astype(BF16))

    wk, wv = _layout_w_ukv(mla_w_ukv)
    k_all, v_all, cos_t, sin_t = _shared_kv(
        xs, pos, inv_lane, kv_norm.reshape(1, d), _layout_w_dkv(mla_w_dkv),
        mla_kv_lat_norm.reshape(1, MLA_KV_RANK), wk, wv)
    k_all = k_all.reshape(batch, seq, MLA_HEADS * LANES)
    v_all = v_all.reshape(batch, seq, MLA_HEADS * MLA_V_DIM)

    for layer in range(N_A_LAYERS, DEPTH):
        j = layer - N_A_LAYERS
        q = _mla_q(xs, cos_t, sin_t, attn_norm[layer].reshape(1, d),
                   mla_w_dq[j].astype(BF16), mla_q_lat_norm[j].reshape(1, MLA_Q_RANK),
                   _layout_w_uq(mla_w_uq[j]))
        attn = _mla_attention(q.reshape(batch, seq, MLA_HEADS * LANES), k_all, v_all, batch, seq)
        xs = _proj_mlp(xs, attn.reshape(n, MLA_HEADS * MLA_V_DIM), mla_w_o[j].astype(BF16),
                       mlp_norm[layer].reshape(1, d),
                       mlp_w1[layer].astype(BF16), mlp_w2[layer].astype(BF16),
                       final_g=final_norm.reshape(1, d) if layer == DEPTH - 1 else None)
    return xs.reshape(batch, seq, d)
```
